```python
import math
import jax
import jax.numpy as jnp
from jax import lax
import numpy as np

D_MODEL = 4096
BATCH = 4
SEQ = 2048
DEPTH = 4
DEC_BATCH = 8
DEC_SEQ = 1
PAST_LEN = 8192
PAGE_SIZE = 128

N_MIXERS = 3
N_A = (DEPTH + 2) // 3
N_B = (DEPTH + 1) // 3
N_C = DEPTH // 3
D_FF = 4 * D_MODEL
LN_EPS = 1e-5
ALPHA = (2.0 * DEPTH) ** 0.25
BETA = (8.0 * DEPTH) ** -0.25
RWKV_HEAD_DIM = 64
RWKV_HEADS = D_MODEL // RWKV_HEAD_DIM
LORA_DECAY = max(32, int(round(1.8 * D_MODEL ** 0.5 / 32)) * 32)
LORA_AAA = max(32, int(round(1.8 * D_MODEL ** 0.5 / 32)) * 32)
LORA_GATE = max(32, int(round(0.6 * D_MODEL ** 0.8 / 32)) * 32)
GN_EPS = 64e-5
S5_GROUP = 16
S5_GROUPS = D_MODEL // S5_GROUP
S5_STATE = 64
S5_CHUNK = 128
ATTN_HEAD_DIM = 128
ATTN_HEADS = D_MODEL // ATTN_HEAD_DIM
IDX_HEADS = 64
IDX_DIM = 128
IDX_SCALE = (IDX_HEADS * IDX_DIM) ** -0.5
TOPK_MAX = 256
Q_BLOCK = 32
REL_BUCKETS = 32
REL_MAX_DIST = 1024

kernel_name = "hybrid_rwkv7_s5_dsa_decoder_step"


def layer_norm(x, g, b):
    xf = x.astype(jnp.float32)
    mu = xf.mean(-1, keepdims=True)
    var = jnp.square(xf - mu).mean(-1, keepdims=True)
    y = (xf - mu) * lax.rsqrt(var + LN_EPS) * g.astype(jnp.float32) + b.astype(jnp.float32)
    return y.astype(x.dtype)


def gather_rows(rows, idx):
    return jax.vmap(lambda r, i: r[i])(rows, idx)


def rwkv7_mix(h, shift0, wkv0, mu, w_rkv, w_o, w0, w1, w2, a0, a1, a2, g1, g2, k_k, k_a, r_k, gn_g, gn_b):
    B, T, D = h.shape
    H, N = RWKV_HEADS, RWKV_HEAD_DIM
    f32 = jnp.float32
    prev = jnp.concatenate([shift0[:, None].astype(h.dtype), h[:, :-1]], axis=1)
    d = prev - h
    mix = lambda i: h + d * mu[i]
    r = mix(0) @ w_rkv[0]
    k = mix(2) @ w_rkv[1]
    v = mix(3) @ w_rkv[2]
    w_log = -jax.nn.softplus(-(w0 + jnp.tanh(mix(1) @ w1) @ w2).astype(f32)) - 0.5
    decay = jnp.exp(-jnp.exp(w_log))
    a = jax.nn.sigmoid((a0 + (mix(4) @ a1) @ a2).astype(f32))
    g = jax.nn.sigmoid(mix(5) @ g1) @ g2
    heads = lambda z: z.astype(f32).reshape(B, T, H, N)
    hp = lambda p: p.astype(f32).reshape(H, N)
    r_h, v_h, a_h, w_h = heads(r), heads(v), heads(a), heads(decay)
    kk = heads(k) * hp(k_k)
    kk = kk / jnp.maximum(jnp.sqrt(jnp.sum(kk * kk, -1, keepdims=True)), 1e-12)
    k_h = heads(k) * (1.0 + (a_h - 1.0) * hp(k_a))

    def step(S, inp):
        r_t, w_t, k_t, v_t, kk_t, a_t = inp
        sa = jnp.einsum('bhij,bhj->bhi', S, -kk_t)
        S = (S * w_t[:, :, None, :] + sa[..., None] * (kk_t * a_t)[:, :, None, :]
             + v_t[..., None] * k_t[:, :, None, :])
        return S, jnp.einsum('bhij,bhj->bhi', S, r_t)

    seq = tuple(jnp.moveaxis(z, 1, 0) for z in (r_h, w_h, k_h, v_h, kk, a_h))
    S_last, ys = lax.scan(step, wkv0.astype(f32), seq)
    y = jnp.moveaxis(ys, 0, 1)
    mu_y = y.mean(-1, keepdims=True)
    var_y = jnp.square(y - mu_y).mean(-1, keepdims=True)
    y = (y - mu_y) * lax.rsqrt(var_y + GN_EPS) * hp(gn_g) + hp(gn_b)
    y = y + jnp.sum(r_h * k_h * r_k.astype(f32), -1, keepdims=True) * v_h
    out = (y.reshape(B, T, D).astype(h.dtype) * g) @ w_o
    return out, h[:, -1], S_last


def s5_mix(h, h0_re, h0_im, w_in, lam_re, lam_im, log_dt, b_re, b_im, c_re, c_im, d_skip, w_glu, w_out):
    B, T, D = h.shape
    G, P, I = S5_GROUPS, S5_STATE, S5_GROUP
    f32 = jnp.float32
    u = (h @ w_in).astype(f32)
    dt = jnp.exp(log_dt.astype(f32))[:, None]
    lr, li = lam_re.astype(f32), lam_im.astype(f32)
    mag = jnp.exp(lr * dt)
    abar_re, abar_im = mag * jnp.cos(li * dt), mag * jnp.sin(li * dt)
    den = lr * lr + li * li
    nr, ni = abar_re - 1.0, abar_im
    coef_re = (nr * lr + ni * li) / den
    coef_im = (ni * lr - nr * li) / den
    br_, bi_ = b_re.astype(f32), b_im.astype(f32)
    Bb_re = coef_re[..., None] * br_ - coef_im[..., None] * bi_
    Bb_im = coef_re[..., None] * bi_ + coef_im[..., None] * br_
    cr, ci = c_re.astype(f32), c_im.astype(f32)
    ch = math.gcd(T, S5_CHUNK)
    n_ch = T // ch
    u_ch = jnp.moveaxis(u.reshape(B, n_ch, ch, G, I), 1, 0)
    a_re_b = jnp.broadcast_to(abar_re, (B, ch, G, P))
    a_im_b = jnp.broadcast_to(abar_im, (B, ch, G, P))

    def combine(e1, e2):
        a1r, a1i, b1r, b1i = e1
        a2r, a2i, b2r, b2i = e2
        return (a2r * a1r - a2i * a1i, a2r * a1i + a2i * a1r,
                a2r * b1r - a2i * b1i + b2r, a2r * b1i + a2i * b1r + b2i)

    def chunk_step(carry, u_c):
        hr, hi = carry
        bur = jnp.einsum('bcgi,gni->bcgn', u_c, Bb_re)
        bui = jnp.einsum('bcgi,gni->bcgn', u_c, Bb_im)
        Ar, Ai, sr, si = lax.associative_scan(combine, (a_re_b, a_im_b, bur, bui), axis=1)
        st_re = Ar * hr[:, None] - Ai * hi[:, None] + sr
        st_im = Ar * hi[:, None] + Ai * hr[:, None] + si
        y = jnp.einsum('bcgn,gin->bcgi', st_re, cr) - jnp.einsum('bcgn,gin->bcgi', st_im, ci)
        return (st_re[:, -1], st_im[:, -1]), y

    (hr, hi), ys = lax.scan(chunk_step, (h0_re.astype(f32), h0_im.astype(f32)), u_ch)
    y = jnp.moveaxis(ys, 0, 1).reshape(B, T, D) + d_skip.astype(f32) * u
    z = jax.nn.gelu(y).astype(h.dtype)
    va, vb = jnp.split(z @ w_glu, 2, axis=-1)
    out = (va * jax.nn.sigmoid(vb)) @ w_out
    return out, hr, hi


def rel_bucket(n):
    exact = REL_BUCKETS // 2
    nf = jnp.maximum(n, 1).astype(jnp.float32)
    large = exact + (jnp.log(nf / exact) / math.log(REL_MAX_DIST / exact) * (REL_BUCKETS - exact)).astype(jnp.int32)
    large = jnp.minimum(large, REL_BUCKETS - 1)
    return jnp.where(n < exact, n, large)


def dsa_project(h, w_qkv, idx_w_q, idx_w_k, idx_w_w):
    B, T, _ = h.shape
    q, k, v = jnp.split(h @ w_qkv, 3, axis=-1)
    heads = lambda z: z.reshape(B, T, ATTN_HEADS, ATTN_HEAD_DIM)
    q_idx = (h @ idx_w_q).reshape(B, T, IDX_HEADS, IDX_DIM)
    k_idx = h @ idx_w_k
    w_idx = (h @ idx_w_w) * IDX_SCALE
    return heads(q), heads(k), heads(v), q_idx, k_idx, w_idx


def index_scores(q_idx, w_idx, k_idx):
    s = jax.nn.relu(jnp.einsum('bthd,bsd->bths', q_idx, k_idx).astype(jnp.float32))
    return jnp.einsum('bth,bths->bts', w_idx.astype(jnp.float32), s)


def sparse_attend(q, q_pos, idx, k_sel, v_sel, rel_bias):
    dist = q_pos[None, :, None] - idx
    bias = jnp.moveaxis(rel_bias[rel_bucket(dist)], -1, 2).astype(jnp.float32)
    logits = jnp.einsum('bthd,btkhd->bthk', q, k_sel).astype(jnp.float32) * ATTN_HEAD_DIM ** -0.5 + bias
    logits = jnp.where((dist >= 0)[:, :, None, :], logits, -jnp.inf)
    p = jax.nn.softmax(logits, axis=-1).astype(v_sel.dtype)
    return jnp.einsum('bthk,btkhd->bthd', p, v_sel)


def dsa_prompt(h, w_qkv, w_o, idx_w_q, idx_w_k, idx_w_w, rel_bias):
    B, T, D = h.shape
    q, k, v, q_idx, k_idx, w_idx = dsa_project(h, w_qkv, idx_w_q, idx_w_k, idx_w_w)
    k_top = max(1, min(TOPK_MAX, T // 4))
    qb = math.gcd(T, Q_BLOCK)
    nb = T // qb
    blk = lambda z: jnp.moveaxis(z.reshape((B, nb, qb) + z.shape[2:]), 1, 0)
    key_pos = jnp.arange(T)

    def block(args):
        q_b, qi_b, wi_b, pos_b = args
        sc = index_scores(qi_b, wi_b, k_idx)
        sc = jnp.where(key_pos[None, None, :] <= pos_b[None, :, None], sc, -jnp.inf)
        _, sel = lax.top_k(sc, k_top)
        return sparse_attend(q_b, pos_b, sel, gather_rows(k, sel), gather_rows(v, sel), rel_bias)

    o = lax.map(block, (blk(q), blk(q_idx), blk(w_idx), jnp.arange(T).reshape(nb, qb)))
    o = jnp.moveaxis(o, 0, 1).reshape(B, T, D)
    return o @ w_o, k, v, k_idx


def dsa_sample(h, j, cache_k, cache_v, cache_kidx, page_table, w_qkv, w_o, idx_w_q, idx_w_k, idx_w_w, rel_bias):
    B, T, D = h.shape
    q, k, v, q_idx, k_idx, w_idx = dsa_project(h, w_qkv, idx_w_q, idx_w_k, idx_w_w)
    page = cache_kidx.shape[2]
    past = page_table.shape[1] * page
    L = past + T
    past_kidx = cache_kidx[j, page_table].reshape(B, past, IDX_DIM)
    all_kidx = jnp.concatenate([past_kidx, k_idx.astype(past_kidx.dtype)], axis=1)
    q_pos = past + jnp.arange(T)
    sc = index_scores(q_idx, w_idx, all_kidx)
    sc = jnp.where(jnp.arange(L)[None, None, :] <= q_pos[None, :, None], sc, -jnp.inf)
    _, sel = lax.top_k(sc, max(1, min(TOPK_MAX, L // 4)))
    in_past = (sel < past)[..., None, None]
    ps = jnp.minimum(sel, past - 1)
    phys = gather_rows(page_table, ps // page)
    slot = ps % page
    pn = jnp.clip(sel - past, 0, T - 1)
    k_sel = jnp.where(in_past, cache_k[j, phys, slot], gather_rows(k, pn).astype(cache_k.dtype))
    v_sel = jnp.where(in_past, cache_v[j, phys, slot], gather_rows(v, pn).astype(cache_v.dtype))
    o = sparse_attend(q, q_pos, sel, k_sel, v_sel, rel_bias)
    return o.reshape(B, T, D) @ w_o, k, v, k_idx


def setup_inputs(seed: int = 0) -> dict:
    key = jax.random.key(seed)
    ks = iter(jax.random.split(key, 64))
    f32 = jnp.float32
    nrm = lambda shape, s: jax.random.normal(next(ks), shape, f32) * s
    uni = lambda shape, lo, hi: jax.random.uniform(next(ks), shape, f32, lo, hi)
    D = D_MODEL
    H, N = RWKV_HEADS, RWKV_HEAD_DIM
    G, P, I = S5_GROUPS, S5_STATE, S5_GROUP
    n_pages = PAST_LEN // PAGE_SIZE
    used = DEC_BATCH * n_pages
    n_pool = used + max(1, used // 4)
    inputs = {
        "x_prompt": nrm((BATCH, SEQ, D), 1.0),
        "x_sample": nrm((DEC_BATCH, DEC_SEQ, D), 1.0),
        "cache_k": nrm((N_C, n_pool, PAGE_SIZE, ATTN_HEADS, ATTN_HEAD_DIM), 1.0),
        "cache_v": nrm((N_C, n_pool, PAGE_SIZE, ATTN_HEADS, ATTN_HEAD_DIM), 1.0),
        "cache_kidx": nrm((N_C, n_pool, PAGE_SIZE, IDX_DIM), 1.0),
        "state_wkv": nrm((N_A, DEC_BATCH, H, N, N), 0.1),
        "state_shift": nrm((N_A, DEC_BATCH, D), 1.0),
        "state_s5_re": nrm((N_B, DEC_BATCH, G, P), 0.3),
        "state_s5_im": nrm((N_B, DEC_BATCH, G, P), 0.3),
        "page_table": jax.random.permutation(next(ks), n_pool)[:used].reshape(DEC_BATCH, n_pages).astype(jnp.int32),
        "c_prompt": nrm((BATCH, D), 1.0),
        "c_sample": nrm((DEC_BATCH, D), 1.0),
        "ada_w": nrm((DEPTH, D, 6 * D), 0.1 * D ** -0.5),
        "ada_b": nrm((DEPTH, 6 * D), 0.02),
        "ln_g": 1.0 + nrm((DEPTH, 2, D), 0.02),
        "ln_b": nrm((DEPTH, 2, D), 0.02),
        "mlp_w1": nrm((DEPTH, D, D_FF), D ** -0.5),
        "mlp_w2": nrm((DEPTH, D_FF, D), BETA * D_FF ** -0.5),
        "rwkv_mu": uni((N_A, 6, D), 0.0, 1.0),
        "rwkv_w_rkv": nrm((N_A, 3, D, D), D ** -0.5),
        "rwkv_w_o": nrm((N_A, D, D), BETA * D ** -0.5),
        "rwkv_w0": uni((N_A, D), -6.0, 0.0),
        "rwkv_w1": nrm((N_A, D, LORA_DECAY), D ** -0.5),
        "rwkv_w2": nrm((N_A, LORA_DECAY, D), 0.1 * LORA_DECAY ** -0.5),
        "rwkv_a0": nrm((N_A, D), 0.1),
        "rwkv_a1": nrm((N_A, D, LORA_AAA), D ** -0.5),
        "rwkv_a2": nrm((N_A, LORA_AAA, D), 0.1 * LORA_AAA ** -0.5),
        "rwkv_g1": nrm((N_A, D, LORA_GATE), D ** -0.5),
        "rwkv_g2": nrm((N_A, LORA_GATE, D), LORA_GATE ** -0.5),
        "rwkv_k_k": 0.85 + nrm((N_A, D), 0.02),
        "rwkv_k_a": 1.0 + nrm((N_A, D), 0.02),
        "rwkv_r_k": nrm((N_A, H, N), 0.1),
        "rwkv_gn_g": 1.0 + nrm((N_A, D), 0.02),
        "rwkv_gn_b": nrm((N_A, D), 0.02),
        "s5_w_in": nrm((N_B, D, D), D ** -0.5),
        "s5_lambda_re": -0.5 + nrm((N_B, G, P), 0.01),
        "s5_lambda_im": math.pi * jnp.arange(P, dtype=f32) + nrm((N_B, G, P), 0.01),
        "s5_log_dt": uni((N_B, G), math.log(1e-3), math.log(1e-1)),
        "s5_b_re": nrm((N_B, G, P, I), I ** -0.5),
        "s5_b_im": nrm((N_B, G, P, I), I ** -0.5),
        "s5_c_re": nrm((N_B, G, I, P), P ** -0.5),
        "s5_c_im": nrm((N_B, G, I, P), P ** -0.5),
        "s5_d": nrm((N_B, D), 1.0),
        "s5_w_glu": nrm((N_B, D, 2 * D), D ** -0.5),
        "s5_w_out": nrm((N_B, D, D), BETA * D ** -0.5),
        "attn_w_qkv": nrm((N_C, D, 3 * D), D ** -0.5),
        "attn_w_o": nrm((N_C, D, D), BETA * D ** -0.5),
        "idx_w_q": nrm((N_C, D, IDX_HEADS * IDX_DIM), D ** -0.5),
        "idx_w_k": nrm((N_C, D, IDX_DIM), D ** -0.5),
        "idx_w_w": nrm((N_C, D, IDX_HEADS), D ** -0.5),
        "rel_bias": nrm((REL_BUCKETS, ATTN_HEADS), 0.5),
    }
    return inputs


def reference(x_prompt, x_sample, cache_k, cache_v, cache_kidx, state_wkv, state_shift, state_s5_re, state_s5_im,
              page_table, c_prompt, c_sample, ada_w, ada_b, ln_g, ln_b, mlp_w1, mlp_w2,
              rwkv_mu, rwkv_w_rkv, rwkv_w_o, rwkv_w0, rwkv_w1, rwkv_w2, rwkv_a0, rwkv_a1, rwkv_a2,
              rwkv_g1, rwkv_g2, rwkv_k_k, rwkv_k_a, rwkv_r_k, rwkv_gn_g, rwkv_gn_b,
              s5_w_in, s5_lambda_re, s5_lambda_im, s5_log_dt, s5_b_re, s5_b_im, s5_c_re, s5_c_im, s5_d,
              s5_w_glu, s5_w_out, attn_w_qkv, attn_w_o, idx_w_q, idx_w_k, idx_w_w, rel_bias):
    f32 = jnp.float32

    def mixer(i, h, sample):
        kind, j = i % N_MIXERS, i // N_MIXERS
        B = h.shape[0]
        if kind == 0:
            if sample:
                shift0, wkv0 = state_shift[j], state_wkv[j]
            else:
                shift0 = jnp.zeros((B, D_MODEL), h.dtype)
                wkv0 = jnp.zeros((B, RWKV_HEADS, RWKV_HEAD_DIM, RWKV_HEAD_DIM), f32)
            out, shift_new, wkv_new = rwkv7_mix(
                h, shift0, wkv0, rwkv_mu[j], rwkv_w_rkv[j], rwkv_w_o[j], rwkv_w0[j], rwkv_w1[j], rwkv_w2[j],
                rwkv_a0[j], rwkv_a1[j], rwkv_a2[j], rwkv_g1[j], rwkv_g2[j], rwkv_k_k[j], rwkv_k_a[j],
                rwkv_r_k[j], rwkv_gn_g[j], rwkv_gn_b[j])
            return out, (wkv_new, shift_new)
        if kind == 1:
            if sample:
                h0r, h0i = state_s5_re[j], state_s5_im[j]
            else:
                h0r = jnp.zeros((B, S5_GROUPS, S5_STATE), f32)
                h0i = jnp.zeros((B, S5_GROUPS, S5_STATE), f32)
            out, hr, hi = s5_mix(h, h0r, h0i, s5_w_in[j], s5_lambda_re[j], s5_lambda_im[j], s5_log_dt[j],
                                 s5_b_re[j], s5_b_im[j], s5_c_re[j], s5_c_im[j], s5_d[j], s5_w_glu[j], s5_w_out[j])
            return out, (hr, hi)
        if sample:
            out, kn, vn, kin = dsa_sample(h, j, cache_k, cache_v, cache_kidx, page_table, attn_w_qkv[j], attn_w_o[j],
                                          idx_w_q[j], idx_w_k[j], idx_w_w[j], rel_bias)
        else:
            out, kn, vn, kin = dsa_prompt(h, attn_w_qkv[j], attn_w_o[j], idx_w_q[j], idx_w_k[j], idx_w_w[j], rel_bias)
        return out, (kn, vn, kin)

    def run(x, c, sample):
        states = ([], [], [])
        for i in range(DEPTH):
            mod = (jax.nn.silu(c) @ ada_w[i] + ada_b[i])[:, None, :]
            sh1, sc1, g1, sh2, sc2, g2 = jnp.split(mod, 6, axis=-1)
            out, st = mixer(i, x * (1.0 + sc1) + sh1, sample)
            states[i % N_MIXERS].append(st)
            x = layer_norm(ALPHA * x + (1.0 + g1) * out, ln_g[i, 0], ln_b[i, 0])
            hm = x * (1.0 + sc2) + sh2
            f = jnp.square(jax.nn.relu(hm @ mlp_w1[i])) @ mlp_w2[i]
            x = layer_norm(ALPHA * x + (1.0 + g2) * f, ln_g[i, 1], ln_b[i, 1])
        stk = lambda kind, k: jnp.stack([s[k] for s in states[kind]])
        return x, stk(0, 0), stk(0, 1), stk(1, 0), stk(1, 1), stk(2, 0), stk(2, 1), stk(2, 2)

    y_prompt, p_wkv, p_shift, p_s5_re, p_s5_im, p_k, p_v, p_kidx = run(x_prompt, c_prompt, False)
    y_sample, s_wkv, s_shift, s_s5_re, s_s5_im, s_k, s_v, s_kidx = run(x_sample, c_sample, True)
    return (y_prompt, y_sample, p_wkv, p_shift, p_s5_re, p_s5_im, p_k, p_v, p_kidx,
            s_wkv, s_shift, s_s5_re, s_s5_im, s_k, s_v, s_kidx)
```

```python
import functools
import math

import jax
import jax.numpy as jnp
import numpy as np
from jax import lax
from jax.experimental import pallas as pl
from jax.experimental.pallas import tpu as pltpu

F32 = jnp.float32
BF16 = jnp.bfloat16

LANES = 128
SUBLANES = 8
VMEM_LIMIT_BYTES = 56 * 1024 * 1024

LN_EPS = 1e-5
GN_EPS = 64e-5
RWKV_HEAD_DIM = 64
S5_GROUP = 16
S5_SUB = 16
ATTN_HEAD_DIM = 128
IDX_DIM = 128
TOPK_MAX = 256
REL_BUCKETS = 32
REL_MAX_DIST = 1024
NEG_BIG = -1e30


def _cparams(sem):
    return pltpu.CompilerParams(dimension_semantics=sem, vmem_limit_bytes=VMEM_LIMIT_BYTES)


def _act(p, act):
    if act is None:
        return p
    if act == "relu2":
        r = jnp.maximum(p, 0.0)
        return r * r
    if act == "tanh":
        return jnp.tanh(p)
    if act == "sigmoid":
        return jax.nn.sigmoid(p)
    raise ValueError(act)


def _mm_kernel(a_ref, w_ref, o_ref, *scratch, nk, act):
    p = jnp.dot(a_ref[...].astype(BF16), w_ref[...].astype(BF16), preferred_element_type=F32)
    if nk == 1:
        o_ref[...] = _act(p, act).astype(o_ref.dtype)
        return
    acc_ref, = scratch
    k = pl.program_id(2)

    @pl.when(k == 0)
    def _():
        acc_ref[...] = p

    @pl.when(k > 0)
    def _():
        acc_ref[...] += p

    @pl.when(k == nk - 1)
    def _():
        o_ref[...] = _act(acc_ref[...], act).astype(o_ref.dtype)


def _pick(n, pref):
    if n <= pref:
        return n
    t = (pref // LANES) * LANES
    while t >= LANES:
        if n % t == 0:
            return t
        t -= LANES
    return n


def matmul(a, w, widx=(), *, act=None, out_dtype=F32, n_off=0, n_size=None, tm=1024, tn=512, tk=4096):
    M, K = a.shape
    Kw, N = w.shape[-2:]
    assert K == Kw and len(widx) == w.ndim - 2
    n_size = N if n_size is None else n_size
    tm = _pick(M, tm) if M % SUBLANES == 0 else M
    tn = _pick(n_size, tn)
    tk = _pick(K, tk)
    assert M % tm == 0 and n_size % tn == 0 and K % tk == 0 and n_off % tn == 0
    nk = K // tk
    joff = n_off // tn
    lead = tuple(widx)
    w_spec = pl.BlockSpec((None,) * len(lead) + (tk, tn), lambda i, j, k: lead + (k, j + joff))
    return pl.pallas_call(
        functools.partial(_mm_kernel, nk=nk, act=act),
        grid=(M // tm, n_size // tn, nk),
        in_specs=[pl.BlockSpec((tm, tk), lambda i, j, k: (i, k)), w_spec],
        out_specs=pl.BlockSpec((tm, tn), lambda i, j, k: (i, j)),
        out_shape=jax.ShapeDtypeStruct((M, n_size), out_dtype),
        scratch_shapes=[pltpu.VMEM((tm, tn), F32)] if nk > 1 else [],
        compiler_params=_cparams(("parallel", "parallel", "arbitrary")),
    )(a, w)


def _allsum_sublanes(p):
    p = p + pltpu.roll(p, 4, 0)
    p = p + pltpu.roll(p, 2, 0)
    return p + pltpu.roll(p, 1, 0)


def _wkv_kernel(r_ref, w_ref, k_ref, v_ref, nkk_ref, b_ref, s0_ref, y_ref, sT_ref, s_ref, *, tc, n):
    c = pl.program_id(1)

    @pl.when(c == 0)
    def _():
        s_ref[...] = s0_ref[...]

    def step(t, carry):
        w = w_ref[t]
        k = k_ref[t]
        r = r_ref[t]
        nkk = nkk_ref[t]
        b = b_ref[t]

        def row(i, carry2):
            si = s_ref[i]
            sa = _allsum_sublanes(jnp.sum(si * nkk, axis=0))
            vi = v_ref[t, pl.ds(i, 1), :]
            sn = si * w + sa[None] * b + vi[None] * k
            s_ref[i] = sn
            yi = _allsum_sublanes(jnp.sum(sn * r, axis=0))
            y_ref[t, pl.ds(i, 1), :] = yi[0:1]
            return carry2

        lax.fori_loop(0, n, row, 0, unroll=8)
        return carry

    lax.fori_loop(0, tc, step, 0)

    @pl.when(c == pl.num_programs(1) - 1)
    def _():
        sT_ref[...] = s_ref[...]


def wkv_scan(r, w, k, v, nkk, b, s0, *, tc):
    T, n8, _, BH = r.shape
    n = n8 * SUBLANES
    lb = min(LANES, BH)
    assert T % tc == 0 and BH % lb == 0
    op = pl.BlockSpec((tc, n8, SUBLANES, lb), lambda l, c: (c, 0, 0, l))
    vy = pl.BlockSpec((tc, n, lb), lambda l, c: (c, 0, l))
    st = pl.BlockSpec((n, n8, SUBLANES, lb), lambda l, c: (0, 0, 0, l))
    return pl.pallas_call(
        functools.partial(_wkv_kernel, tc=tc, n=n),
        grid=(BH // lb, T // tc),
        in_specs=[op, op, op, vy, op, op, st],
        out_specs=[vy, st],
        out_shape=[jax.ShapeDtypeStruct((T, n, BH), F32), jax.ShapeDtypeStruct(s0.shape, F32)],
        scratch_shapes=[pltpu.VMEM((n, n8, SUBLANES, lb), F32)],
        compiler_params=_cparams(("parallel", "arbitrary")),
    )(r, w, k, v, nkk, b, s0)


_HI = lax.Precision.HIGHEST


def _s5_fold(lam_re, lam_im, log_dt, b_re, b_im, c_re, c_im, C):
    G, P, I = b_re.shape
    dt = jnp.exp(log_dt.astype(F32))[:, None]
    lr, li = lam_re.astype(F32), lam_im.astype(F32)
    mag = jnp.exp(lr * dt)
    abar_re, abar_im = mag * jnp.cos(li * dt), mag * jnp.sin(li * dt)
    den = lr * lr + li * li
    nr, ni = abar_re - 1.0, abar_im
    coef_re = (nr * lr + ni * li) / den
    coef_im = (ni * lr - nr * li) / den
    br_, bi_ = b_re.astype(F32), b_im.astype(F32)
    bb_re = coef_re[..., None] * br_ - coef_im[..., None] * bi_
    bb_im = coef_re[..., None] * bi_ + coef_im[..., None] * br_
    cr, ci = c_re.astype(F32), c_im.astype(F32)
    pw_re, pw_im = [jnp.ones_like(abar_re)], [jnp.zeros_like(abar_im)]
    for _ in range(C):
        pr, pi = pw_re[-1], pw_im[-1]
        pw_re.append(pr * abar_re - pi * abar_im)
        pw_im.append(pr * abar_im + pi * abar_re)
    a_re, a_im = jnp.stack(pw_re, 1), jnp.stack(pw_im, 1)
    ab_re = a_re[:, :C, :, None] * bb_re[:, None] - a_im[:, :C, :, None] * bb_im[:, None]
    ab_im = a_re[:, :C, :, None] * bb_im[:, None] + a_im[:, :C, :, None] * bb_re[:, None]
    kern = (jnp.einsum("gjp,gtpi->gtij", cr, ab_re, precision=_HI)
            - jnp.einsum("gjp,gtpi->gtij", ci, ab_im, precision=_HI))
    tt = np.arange(C)[None, :] - np.arange(C)[:, None]
    m = kern[:, np.clip(tt, 0, C - 1)]
    m = jnp.where((tt >= 0)[None, :, :, None, None], m, 0.0)
    m = jnp.transpose(m, (0, 1, 3, 2, 4)).reshape(G, C * I, C * I)
    rev = np.arange(C - 1, -1, -1)
    w_re = jnp.transpose(ab_re[:, rev], (0, 1, 3, 2)).reshape(G, C * I, P)
    w_im = jnp.transpose(ab_im[:, rev], (0, 1, 3, 2)).reshape(G, C * I, P)
    ar1, ai1 = a_re[:, 1:], a_im[:, 1:]
    v_re = cr[:, None] * ar1[:, :, None, :] - ci[:, None] * ai1[:, :, None, :]
    v_im = -cr[:, None] * ai1[:, :, None, :] - ci[:, None] * ar1[:, :, None, :]
    v_re = jnp.transpose(v_re, (0, 3, 1, 2)).reshape(G, P, C * I)
    v_im = jnp.transpose(v_im, (0, 3, 1, 2)).reshape(G, P, C * I)
    return m, w_re, w_im, v_re, v_im, a_re[:, C][:, None], a_im[:, C][:, None]


def _s5_kernel(u_ref, m_ref, wre_ref, wim_ref, vre_ref, vim_ref, are_ref, aim_ref, x0re_ref, x0im_ref,
               y_ref, xre_ref, xim_ref, bur_ref, bui_ref, *, gb, nc, bp):
    for g in range(gb):
        u = u_ref[g]
        bur_ref[g] = jnp.dot(u, wre_ref[g], precision=_HI, preferred_element_type=F32)
        bui_ref[g] = jnp.dot(u, wim_ref[g], precision=_HI, preferred_element_type=F32)
    ar = are_ref[...]
    ai = aim_ref[...]

    def chunk(c, carry):
        xr, xi = carry
        rows = pl.ds(pl.multiple_of(c * bp, bp), bp)
        br = bur_ref[:, rows, :]
        bi = bui_ref[:, rows, :]
        bur_ref[:, rows, :] = xr
        bui_ref[:, rows, :] = xi
        return ar * xr - ai * xi + br, ar * xi + ai * xr + bi

    xr, xi = lax.fori_loop(0, nc, chunk, (x0re_ref[...], x0im_ref[...]))
    xre_ref[...] = xr
    xim_ref[...] = xi
    for g in range(gb):
        y = jnp.dot(u_ref[g], m_ref[g], precision=_HI, preferred_element_type=F32)
        y += jnp.dot(bur_ref[g], vre_ref[g], precision=_HI, preferred_element_type=F32)
        y += jnp.dot(bui_ref[g], vim_ref[g], precision=_HI, preferred_element_type=F32)
        y_ref[g] = y


def s5_scan(u, folded, x0_re, x0_im, *, gb):
    m, w_re, w_im, v_re, v_im, a_re, a_im = folded
    G, R, CI = u.shape
    P = w_re.shape[-1]
    bp = x0_re.shape[1]
    nc = R // bp
    assert G % gb == 0 and bp % SUBLANES == 0
    blk = lambda *s: pl.BlockSpec((gb,) + s, lambda g: (g,) + (0,) * len(s))
    return pl.pallas_call(
        functools.partial(_s5_kernel, gb=gb, nc=nc, bp=bp),
        grid=(G // gb,),
        in_specs=[blk(R, CI), blk(CI, CI), blk(CI, P), blk(CI, P), blk(P, CI), blk(P, CI),
                  blk(1, P), blk(1, P), blk(bp, P), blk(bp, P)],
        out_specs=[blk(R, CI), blk(bp, P), blk(bp, P)],
        out_shape=[jax.ShapeDtypeStruct((G, R, CI), F32), jax.ShapeDtypeStruct((G, bp, P), F32),
                   jax.ShapeDtypeStruct((G, bp, P), F32)],
        scratch_shapes=[pltpu.VMEM((gb, R, P), F32), pltpu.VMEM((gb, R, P), F32)],
        compiler_params=_cparams(("parallel",)),
    )(u, m, w_re, w_im, v_re, v_im, a_re, a_im, x0_re, x0_im)


def _ln_kernel(x_ref, f_ref, gate_ref, g_ref, b_ref, sc_ref, sh_ref, xo_ref, ho_ref, *, alpha):
    y = alpha * x_ref[...] + gate_ref[...] * f_ref[...]
    mu = jnp.mean(y, axis=-1, keepdims=True)
    yc = y - mu
    var = jnp.mean(yc * yc, axis=-1, keepdims=True)
    xn = yc * lax.rsqrt(var + LN_EPS) * g_ref[...] + b_ref[...]
    xo_ref[...] = xn
    ho_ref[...] = (xn * sc_ref[...] + sh_ref[...]).astype(ho_ref.dtype)


def residual_ln(x, f, gate, g, b, scale, shift, *, rows_per_batch, alpha, tm=256):
    M, D = x.shape
    B = gate.shape[0]
    if rows_per_batch == 1:
        tm = M
        mod = pl.BlockSpec((tm, D), lambda i: (i, 0))
        mods = (gate, scale, shift)
    else:
        tm = min(tm, rows_per_batch)
        assert rows_per_batch % tm == 0
        per = rows_per_batch // tm
        mod = pl.BlockSpec((None, 1, D), lambda i: (i // per, 0, 0))
        mods = tuple(z.reshape(B, 1, D) for z in (gate, scale, shift))
    row = pl.BlockSpec((tm, D), lambda i: (i, 0))
    vec = pl.BlockSpec((1, D), lambda i: (0, 0))
    return pl.pallas_call(
        functools.partial(_ln_kernel, alpha=alpha),
        grid=(M // tm,),
        in_specs=[row, row, mod, vec, vec, mod, mod],
        out_specs=[row, row],
        out_shape=[jax.ShapeDtypeStruct((M, D), F32), jax.ShapeDtypeStruct((M, D), BF16)],
        compiler_params=_cparams(("parallel",)),
    )(x, f, mods[0], g.reshape(1, D), b.reshape(1, D), mods[1], mods[2])


INT_MIN = -(2 ** 31)


def _order_key(x):
    bits = pltpu.bitcast(x, jnp.int32)
    return bits ^ ((bits >> 31) & 0x7FFFFFFF)


def _topk_member(key_ref, k_top):
    rows, L = key_ref.shape
    kf = jnp.float32(k_top)

    def count(pred):
        return jnp.sum(pred.astype(F32), axis=-1, keepdims=True)

    def bit_step(it, res):
        cand = res | (jnp.int32(1) << (31 - it))
        cnt = count(key_ref[...] >= (cand ^ INT_MIN))
        return jnp.where(cnt >= kf, cand, res)

    res = lax.fori_loop(0, 32, bit_step, jnp.zeros((rows, 1), jnp.int32))
    thr = res ^ INT_MIN
    key = key_ref[...]
    gt = key > thr
    eq = key == thr
    need = kf - count(gt)
    pos = lax.broadcasted_iota(jnp.int32, (rows, L), 1)
    nbits = max(1, (L - 1).bit_length())

    def pos_step(it, lim):
        cand = lim | (jnp.int32(1) << (nbits - 1 - it))
        cnt = count((key_ref[...] == thr) & (pos < cand))
        return jnp.where(cnt < need, cand, lim)

    lim = lax.fori_loop(0, nbits, pos_step, jnp.zeros((rows, 1), jnp.int32))
    return gt | (eq & (pos <= lim))


def _idx_kernel(q_ref, k_ref, w_ref, mask_ref, acc_ref, key_ref, *, tq, n_heads, k_top):
    i = pl.program_id(1)
    S = k_ref.shape[0]
    kb = k_ref[...].astype(BF16)
    acc_ref[...] = jnp.zeros_like(acc_ref)

    def head(h, carry):
        q = q_ref[:, pl.ds(pl.multiple_of(h * IDX_DIM, IDX_DIM), IDX_DIM)]
        s = lax.dot_general(q, kb, (((1,), (1,)), ((), ())), preferred_element_type=F32)
        wcol = pltpu.roll(w_ref[...], LANES - h, 1)[:, :1]
        acc_ref[...] += wcol * jnp.maximum(s, 0.0)
        return carry

    lax.fori_loop(0, n_heads, head, 0)
    t_pos = i * tq + lax.broadcasted_iota(jnp.int32, (tq, S), 0)
    s_pos = lax.broadcasted_iota(jnp.int32, (tq, S), 1)
    causal = s_pos <= t_pos
    key_ref[...] = _order_key(jnp.where(causal, acc_ref[...], -jnp.inf))
    member = _topk_member(key_ref, k_top)
    mask_ref[...] = jnp.where(member & causal, 0.0, NEG_BIG)


def dsa_select_prompt(q_idx, k_idx, w_idx, *, B, T, k_top, tq=256):
    n_heads = q_idx.shape[1] // IDX_DIM
    tq = min(tq, T)
    nq = T // tq
    return pl.pallas_call(
        functools.partial(_idx_kernel, tq=tq, n_heads=n_heads, k_top=k_top),
        grid=(B, nq),
        in_specs=[pl.BlockSpec((tq, q_idx.shape[1]), lambda b, i: (b * nq + i, 0)),
                  pl.BlockSpec((T, IDX_DIM), lambda b, i: (b, 0)),
                  pl.BlockSpec((tq, LANES), lambda b, i: (b * nq + i, 0))],
        out_specs=pl.BlockSpec((tq, T), lambda b, i: (b * nq + i, 0)),
        out_shape=jax.ShapeDtypeStruct((B * T, T), F32),
        scratch_shapes=[pltpu.VMEM((tq, T), F32), pltpu.VMEM((tq, T), jnp.int32)],
        compiler_params=_cparams(("parallel", "parallel")),
    )(q_idx, k_idx, w_idx)


def _bucket_thresholds(max_dist):
    exact = REL_BUCKETS // 2
    d = np.arange(max_dist + 1)
    large = exact + np.floor(np.log(np.maximum(d, 1) / exact) / math.log(REL_MAX_DIST / exact)
                             * (REL_BUCKETS - exact) + 1e-9).astype(np.int64)
    bucket = np.where(d < exact, d, np.minimum(large, REL_BUCKETS - 1))
    return [int(np.argmax(bucket >= k)) if (bucket >= k).any() else max_dist + 1 for k in range(1, REL_BUCKETS)]


def rel_bucket_of(dist, max_dist):
    out = jnp.zeros(dist.shape, jnp.int32)
    for thr in _bucket_thresholds(max_dist):
        out = out + (dist >= thr).astype(jnp.int32)
    return out


def _attn_kernel(q_ref, k_ref, v_ref, mask_ref, bkt_ref, tab_ref, o_ref, *, scale):
    tq = q_ref.shape[0]
    S = k_ref.shape[0]
    q = q_ref[...].astype(BF16)
    kb = k_ref[...].astype(BF16)
    logits = lax.dot_general(q, kb, (((1,), (1,)), ((), ())), preferred_element_type=F32) * scale
    tab = jnp.broadcast_to(tab_ref[...], (tq, LANES))
    bias = jnp.concatenate(
        [jnp.take_along_axis(tab, bkt_ref[:, c * LANES:(c + 1) * LANES], axis=1) for c in range(S // LANES)], axis=1)
    logits = logits + bias + mask_ref[...]
    m = jnp.max(logits, axis=-1, keepdims=True)
    p = jnp.exp(logits - m)
    l = jnp.sum(p, axis=-1, keepdims=True)
    o = jnp.dot(p.astype(BF16), v_ref[...].astype(BF16), preferred_element_type=F32)
    o_ref[...] = (o / l).astype(o_ref.dtype)


def dsa_attend_prompt(q, k, v, mask, bucket, bias_tab, *, B, T, tq=256):
    H = q.shape[1] // ATTN_HEAD_DIM
    tq = min(tq, T)
    nq = T // tq
    return pl.pallas_call(
        functools.partial(_attn_kernel, scale=ATTN_HEAD_DIM ** -0.5),
        grid=(B, nq, H),
        in_specs=[pl.BlockSpec((tq, ATTN_HEAD_DIM), lambda b, i, h: (b * nq + i, h)),
                  pl.BlockSpec((T, ATTN_HEAD_DIM), lambda b, i, h: (b, h)),
                  pl.BlockSpec((T, ATTN_HEAD_DIM), lambda b, i, h: (b, h)),
                  pl.BlockSpec((tq, T), lambda b, i, h: (b * nq + i, 0)),
                  pl.BlockSpec((tq, T), lambda b, i, h: (i, 0)),
                  pl.BlockSpec((None, 1, LANES), lambda b, i, h: (h, 0, 0))],
        out_specs=pl.BlockSpec((tq, ATTN_HEAD_DIM), lambda b, i, h: (b * nq + i, h)),
        out_shape=jax.ShapeDtypeStruct(q.shape, BF16),
        compiler_params=_cparams(("parallel", "parallel", "arbitrary")),
    )(q, k, v, mask, bucket, bias_tab)


def _page_score_kernel(pt_ref, q_ref, w_ref, kidx_ref, sc_ref):
    kb = kidx_ref[...].astype(BF16)
    s = lax.dot_general(q_ref[...], kb, (((1,), (1,)), ((), ())), preferred_element_type=F32)
    sc_ref[...] = jnp.sum(w_ref[...] * jnp.maximum(s, 0.0), axis=0, keepdims=True)


def dsa_page_scores(q_idx, w_idx, cache_kidx, page_table, j):
    B, HI, _ = q_idx.shape
    page = cache_kidx.shape[2]
    n_pages = page_table.shape[1]
    grid_spec = pltpu.PrefetchScalarGridSpec(
        num_scalar_prefetch=1,
        grid=(B, n_pages),
        in_specs=[pl.BlockSpec((None, HI, IDX_DIM), lambda b, p, pt: (b, 0, 0)),
                  pl.BlockSpec((None, HI, 1), lambda b, p, pt: (b, 0, 0)),
                  pl.BlockSpec((None, None, page, IDX_DIM), lambda b, p, pt: (j, pt[b, p], 0, 0))],
        out_specs=pl.BlockSpec((None, 1, page), lambda b, p, pt: (b, 0, p)),
    )
    return pl.pallas_call(
        _page_score_kernel,
        grid_spec=grid_spec,
        out_shape=jax.ShapeDtypeStruct((B, 1, n_pages * page), F32),
        compiler_params=_cparams(("parallel", "arbitrary")),
    )(page_table, q_idx, w_idx, cache_kidx)


def _sample_select_kernel(sc_ref, q_ref, w_ref, knew_ref, mask_ref, key_ref, *, past, k_top):
    B = sc_ref.shape[0]
    key_ref[:, :past] = _order_key(sc_ref[...])
    lane = lax.broadcasted_iota(jnp.int32, (1, LANES), 1)
    for b in range(B):
        kn = knew_ref[b].astype(BF16).astype(F32)
        s = jnp.sum(q_ref[b].astype(F32) * kn, axis=-1, keepdims=True)
        sc_new = jnp.sum(w_ref[b] * jnp.maximum(s, 0.0), axis=0, keepdims=True)
        tail = jnp.where(lane == 0, sc_new, -jnp.inf)
        key_ref[b:b + 1, past:] = _order_key(tail)
    member = _topk_member(key_ref, k_top)
    mask_ref[...] = jnp.where(member, 0.0, NEG_BIG)


def dsa_select_sample(scores, q_idx, w_idx, k_idx_new, *, k_top):
    B, past = scores.shape
    L = past + LANES
    return pl.pallas_call(
        functools.partial(_sample_select_kernel, past=past, k_top=k_top),
        out_shape=jax.ShapeDtypeStruct((B, L), F32),
        scratch_shapes=[pltpu.VMEM((B, L), jnp.int32)],
        compiler_params=pltpu.CompilerParams(vmem_limit_bytes=VMEM_LIMIT_BYTES),
    )(scores, q_idx, w_idx, k_idx_new)


def _page_attn_kernel(pt_ref, q_ref, k_ref, v_ref, bias_ref, mask_ref, knew_ref, vnew_ref, bnew_ref, mnew_ref,
                      o_ref, m_ref, l_ref, acc_ref, *, scale, H):
    p_idx = pl.program_id(1)

    @pl.when(p_idx == 0)
    def _():
        m_ref[...] = jnp.full_like(m_ref, NEG_BIG)
        l_ref[...] = jnp.zeros_like(l_ref)
        acc_ref[...] = jnp.zeros_like(acc_ref)

    q = q_ref[...].astype(BF16)
    cols = k_ref.shape[0]
    la = lax.dot_general(q, k_ref[...].astype(BF16), (((1,), (1,)), ((), ())), preferred_element_type=F32)
    la = la * scale + bias_ref[...]
    row_h = lax.broadcasted_iota(jnp.int32, (H, cols), 0)
    col_h = lax.broadcasted_iota(jnp.int32, (H, cols), 1) % H
    valid = (row_h == col_h) & (mask_ref[...] == 0.0)
    la = jnp.where(valid, la, NEG_BIG)
    m_old = m_ref[...]
    m_new = jnp.maximum(m_old, jnp.max(la, axis=-1, keepdims=True))
    alpha = jnp.exp(m_old - m_new)
    p = jnp.where(valid, jnp.exp(la - m_new), 0.0)
    l_ref[...] = alpha * l_ref[...] + jnp.sum(p, axis=-1, keepdims=True)
    acc_ref[...] = alpha * acc_ref[...] + jnp.dot(p.astype(BF16), v_ref[...].astype(BF16), preferred_element_type=F32)
    m_ref[...] = m_new

    @pl.when(p_idx == pl.num_programs(1) - 1)
    def _():
        kn = knew_ref[...].astype(BF16).astype(F32)
        s_new = jnp.sum(q.astype(F32) * kn, axis=-1, keepdims=True) * scale + bnew_ref[...]
        ok = mnew_ref[...][:, :1] == 0.0
        s_new = jnp.where(ok, s_new, NEG_BIG)
        m_o = m_ref[...]
        m_f = jnp.maximum(m_o, s_new)
        a_f = jnp.exp(m_o - m_f)
        p_new = jnp.where(ok, jnp.exp(s_new - m_f), 0.0)
        l_f = a_f * l_ref[...] + p_new
        acc = a_f * acc_ref[...] + p_new.astype(BF16).astype(F32) * vnew_ref[...].astype(BF16).astype(F32)
        o_ref[...] = acc / l_f


def dsa_attend_sample(q, cache_k, cache_v, page_table, j, bias_rows, mask_rows, k_new, v_new, bias_new, mask_new):
    B, H, Dh = q.shape
    cols = cache_k.shape[2]
    n_pages = page_table.shape[1]
    per_b = lambda *s: pl.BlockSpec((None,) + s, lambda b, p, pt: (b,) + (0,) * len(s))
    grid_spec = pltpu.PrefetchScalarGridSpec(
        num_scalar_prefetch=1,
        grid=(B, n_pages),
        in_specs=[per_b(H, Dh),
                  pl.BlockSpec((None, None, cols, Dh), lambda b, p, pt: (j, pt[b, p], 0, 0)),
                  pl.BlockSpec((None, None, cols, Dh), lambda b, p, pt: (j, pt[b, p], 0, 0)),
                  pl.BlockSpec((None, 1, cols), lambda b, p, pt: (p, 0, 0)),
                  pl.BlockSpec((None, None, 1, cols), lambda b, p, pt: (b, p, 0, 0)),
                  per_b(H, Dh), per_b(H, Dh),
                  pl.BlockSpec((H, 1), lambda b, p, pt: (0, 0)),
                  per_b(1, LANES)],
        out_specs=per_b(H, Dh),
        scratch_shapes=[pltpu.VMEM((H, 1), F32), pltpu.VMEM((H, 1), F32), pltpu.VMEM((H, Dh), F32)],
    )
    return pl.pallas_call(
        functools.partial(_page_attn_kernel, scale=Dh ** -0.5, H=H),
        grid_spec=grid_spec,
        out_shape=jax.ShapeDtypeStruct((B, H, Dh), F32),
        compiler_params=_cparams(("parallel", "arbitrary")),
    )(page_table, q, cache_k, cache_v, bias_rows, mask_rows, k_new, v_new, bias_new, mask_new)


def _rwkv_layer(h, B, T, shift0, wkv0, j, p):
    M, D = h.shape
    N = RWKV_HEAD_DIM
    H = D // N
    h3 = h.reshape(B, T, D)
    prev = jnp.concatenate([shift0[:, None].astype(F32), h3[:, :-1]], axis=1)
    d = prev - h3
    mu = p["rwkv_mu"][j]
    mix = [(h3 + d * mu[i]).astype(BF16).reshape(M, D) for i in range(6)]
    r = matmul(mix[0], p["rwkv_w_rkv"], (j, 0))
    k = matmul(mix[2], p["rwkv_w_rkv"], (j, 1))
    v = matmul(mix[3], p["rwkv_w_rkv"], (j, 2))
    wl = matmul(matmul(mix[1], p["rwkv_w1"], (j,), act="tanh", out_dtype=BF16), p["rwkv_w2"], (j,))
    al = matmul(matmul(mix[4], p["rwkv_a1"], (j,), out_dtype=BF16), p["rwkv_a2"], (j,))
    g = matmul(matmul(mix[5], p["rwkv_g1"], (j,), act="sigmoid", out_dtype=BF16), p["rwkv_g2"], (j,))
    w_log = -jax.nn.softplus(-(p["rwkv_w0"][j] + wl)) - 0.5
    decay = jnp.exp(-jnp.exp(w_log))
    a = jax.nn.sigmoid(p["rwkv_a0"][j] + al)
    heads = lambda z: z.reshape(B, T, H, N)
    hp = lambda z: z.astype(F32).reshape(H, N)
    r_h, k0, v_h, a_h, w_h = heads(r), heads(k), heads(v), heads(a), heads(decay)
    kk = k0 * hp(p["rwkv_k_k"][j])
    kk = kk / jnp.maximum(jnp.sqrt(jnp.sum(kk * kk, -1, keepdims=True)), 1e-12)
    k_h = k0 * (1.0 + (a_h - 1.0) * hp(p["rwkv_k_a"][j]))
    tl = lambda z: jnp.transpose(z, (1, 3, 0, 2)).reshape(T, N // SUBLANES, SUBLANES, B * H)
    if wkv0 is None:
        s0 = jnp.zeros((N, N // SUBLANES, SUBLANES, B * H), F32)
    else:
        s0 = jnp.transpose(wkv0.astype(F32), (2, 3, 0, 1)).reshape(N, N // SUBLANES, SUBLANES, B * H)
    v_t = jnp.transpose(v_h, (1, 3, 0, 2)).reshape(T, N, B * H)
    y, s_last = wkv_scan(tl(r_h), tl(w_h), tl(k_h), v_t, tl(-kk), tl(kk * a_h), s0, tc=math.gcd(T, 16))
    y = jnp.transpose(y.reshape(T, N, B, H), (2, 0, 3, 1))
    wkv_new = jnp.transpose(s_last.reshape(N, N, B, H), (2, 3, 0, 1))
    mu_y = y.mean(-1, keepdims=True)
    var_y = jnp.square(y - mu_y).mean(-1, keepdims=True)
    y = (y - mu_y) * lax.rsqrt(var_y + GN_EPS) * hp(p["rwkv_gn_g"][j]) + hp(p["rwkv_gn_b"][j])
    y = y + jnp.sum(r_h * k_h * p["rwkv_r_k"][j].astype(F32), -1, keepdims=True) * v_h
    out = matmul((y.reshape(M, D) * g).astype(BF16), p["rwkv_w_o"], (j,))
    return out, h3[:, -1], wkv_new


def _s5_layer(hb, B, T, x0_re, x0_im, j, p):
    M, D = hb.shape
    I = S5_GROUP
    G = D // I
    P = p["s5_lambda_re"].shape[-1]
    C = S5_SUB if T % S5_SUB == 0 else 1
    nc = T // C
    bp = -(-B // SUBLANES) * SUBLANES
    u = matmul(hb, p["s5_w_in"], (j,))
    folded = _s5_fold(p["s5_lambda_re"][j], p["s5_lambda_im"][j], p["s5_log_dt"][j], p["s5_b_re"][j],
                      p["s5_b_im"][j], p["s5_c_re"][j], p["s5_c_im"][j], C)
    ug = jnp.transpose(u.reshape(B, nc, C, G, I), (3, 1, 0, 2, 4))
    ug = jnp.pad(ug, ((0, 0), (0, 0), (0, bp - B), (0, 0), (0, 0))).reshape(G, nc * bp, C * I)
    if x0_re is None:
        x0r = x0i = jnp.zeros((G, bp, P), F32)
    else:
        st = lambda z: jnp.pad(jnp.transpose(z.astype(F32), (1, 0, 2)), ((0, 0), (0, bp - B), (0, 0)))
        x0r, x0i = st(x0_re), st(x0_im)
    gb = 4 if C > 1 else 8
    yg, xr, xi = s5_scan(ug, folded, x0r, x0i, gb=gb)
    y = jnp.transpose(yg.reshape(G, nc, bp, C, I)[:, :, :B], (2, 1, 3, 0, 4)).reshape(M, D)
    y = y + p["s5_d"][j].astype(F32) * u
    z = jax.nn.gelu(y).astype(BF16)
    ab = matmul(z, p["s5_w_glu"], (j,))
    gl = (ab[:, :D] * jax.nn.sigmoid(ab[:, D:])).astype(BF16)
    out = matmul(gl, p["s5_w_out"], (j,))
    fin = lambda z: jnp.transpose(z[:, :B], (1, 0, 2))
    return out, fin(xr), fin(xi)


def _dsa_project(hb, j, p):
    D = hb.shape[1]
    q = matmul(hb, p["attn_w_qkv"], (j,), n_off=0, n_size=D)
    k = matmul(hb, p["attn_w_qkv"], (j,), n_off=D, n_size=D)
    v = matmul(hb, p["attn_w_qkv"], (j,), n_off=2 * D, n_size=D)
    q_idx = matmul(hb, p["idx_w_q"], (j,), out_dtype=BF16)
    k_idx = matmul(hb, p["idx_w_k"], (j,))
    n_ih = p["idx_w_w"].shape[-1]
    w_idx = matmul(hb, p["idx_w_w"], (j,)) * ((n_ih * IDX_DIM) ** -0.5)
    return q, k, v, q_idx, k_idx, w_idx


def _dsa_prompt_layer(hb, B, T, j, p):
    M, D = hb.shape
    H = D // ATTN_HEAD_DIM
    q, k, v, q_idx, k_idx, w_idx = _dsa_project(hb, j, p)
    k_top = max(1, min(TOPK_MAX, T // 4))
    w_pad = jnp.pad(w_idx, ((0, 0), (0, LANES - w_idx.shape[1])))
    mask = dsa_select_prompt(q_idx, k_idx, w_pad, B=B, T=T, k_top=k_top)
    pos = jnp.arange(T, dtype=jnp.int32)
    bucket = rel_bucket_of(pos[:, None] - pos[None, :], T)
    tab = jnp.pad(p["rel_bias"].astype(F32).T, ((0, 0), (0, LANES - REL_BUCKETS))).reshape(H, 1, LANES)
    o = dsa_attend_prompt(q, k, v, mask, bucket, tab, B=B, T=T)
    out = matmul(o, p["attn_w_o"], (j,))
    return out, k.reshape(B, T, H, ATTN_HEAD_DIM), v.reshape(B, T, H, ATTN_HEAD_DIM), k_idx.reshape(B, T, IDX_DIM)


def _dsa_sample_layer(hb, B, j, p, cache_k, cache_v, cache_kidx, page_table):
    M, D = hb.shape
    H = D // ATTN_HEAD_DIM
    q, k, v, q_idx, k_idx, w_idx = _dsa_project(hb, j, p)
    page = cache_kidx.shape[2]
    n_pages = page_table.shape[1]
    past = n_pages * page
    n_ih = w_idx.shape[1]
    q_idx3 = q_idx.reshape(B, n_ih, IDX_DIM)
    w_idx3 = w_idx.reshape(B, n_ih, 1)
    scores = dsa_page_scores(q_idx3, w_idx3, cache_kidx, page_table, j).reshape(B, past)
    k_top = max(1, min(TOPK_MAX, (past + 1) // 4))
    mask = dsa_select_sample(scores, q_idx3, w_idx3, k_idx.reshape(B, 1, IDX_DIM), k_top=k_top)
    dist = past - jnp.arange(past, dtype=jnp.int32)
    bias_rows = p["rel_bias"].astype(F32)[rel_bucket_of(dist, past)]
    bias_rows = bias_rows.reshape(n_pages, 1, page * H)
    mask_rows = jnp.repeat(mask[:, :past], H, axis=1).reshape(B, n_pages, 1, page * H)
    pool = cache_k.shape[1]
    ck = cache_k.reshape(cache_k.shape[0], pool, page * H, ATTN_HEAD_DIM)
    cv = cache_v.reshape(cache_v.shape[0], pool, page * H, ATTN_HEAD_DIM)
    hd = lambda z: z.reshape(B, H, ATTN_HEAD_DIM)
    bias_new = p["rel_bias"].astype(F32)[0].reshape(H, 1)
    mask_new = jnp.broadcast_to(mask[:, past:past + 1], (B, LANES)).reshape(B, 1, LANES)
    o = dsa_attend_sample(hd(q), ck, cv, page_table, j, bias_rows, mask_rows, hd(k), hd(v), bias_new, mask_new)
    out = matmul(o.reshape(M, D).astype(BF16), p["attn_w_o"], (j,))
    return out, k.reshape(B, 1, H, ATTN_HEAD_DIM), v.reshape(B, 1, H, ATTN_HEAD_DIM), k_idx.reshape(B, 1, IDX_DIM)


def _run(x, mods, sample, p, caches, states):
    B, T, D = x.shape
    M = B * T
    depth = len(mods)
    alpha = (2.0 * depth) ** 0.25
    modulate = lambda z, sc, sh: (z.reshape(B, T, D) * (1.0 + sc)[:, None] + sh[:, None]).reshape(M, D)
    xf = x.reshape(M, D).astype(F32)
    h = modulate(xf, mods[0][1], mods[0][0])
    hb = h.astype(BF16)
    st = ([], [], [])
    for i in range(depth):
        sh1, sc1, g1, sh2, sc2, g2 = mods[i]
        kind, j = i % 3, i // 3
        if kind == 0:
            if i > 0:
                h = modulate(xf, sc1, sh1)
            s0, w0 = (states["shift"][j], states["wkv"][j]) if sample else (jnp.zeros((B, D), F32), None)
            out, shift_new, wkv_new = _rwkv_layer(h, B, T, s0, w0, j, p)
            st[0].append((wkv_new, shift_new))
        elif kind == 1:
            x0r, x0i = (states["s5_re"][j], states["s5_im"][j]) if sample else (None, None)
            out, hr, hi = _s5_layer(hb, B, T, x0r, x0i, j, p)
            st[1].append((hr, hi))
        else:
            if sample:
                out, kn, vn, kin = _dsa_sample_layer(hb, B, j, p, *caches)
            else:
                out, kn, vn, kin = _dsa_prompt_layer(hb, B, T, j, p)
            st[2].append((kn, vn, kin))
        xf, hb = residual_ln(xf, out, 1.0 + g1, p["ln_g"][i, 0], p["ln_b"][i, 0], 1.0 + sc2, sh2,
                             rows_per_batch=T, alpha=alpha)
        f = matmul(matmul(hb, p["mlp_w1"], (i,), act="relu2", out_dtype=BF16), p["mlp_w2"], (i,), tk=2048, tn=1024)
        if i + 1 < depth:
            nsc, nsh = 1.0 + mods[i + 1][1], mods[i + 1][0]
        else:
            nsc, nsh = jnp.ones_like(sc1), jnp.zeros_like(sh1)
        xf, hb = residual_ln(xf, f, 1.0 + g2, p["ln_g"][i, 1], p["ln_b"][i, 1], nsc, nsh,
                             rows_per_batch=T, alpha=alpha)
    stk = lambda kind, k: jnp.stack([s[k] for s in st[kind]])
    return (xf.reshape(B, T, D), stk(0, 0), stk(0, 1), stk(1, 0), stk(1, 1), stk(2, 0), stk(2, 1), stk(2, 2))


def kernel(x_prompt, x_sample, cache_k, cache_v, cache_kidx, state_wkv, state_shift, state_s5_re, state_s5_im,
           page_table, c_prompt, c_sample, ada_w, ada_b, ln_g, ln_b, mlp_w1, mlp_w2,
           rwkv_mu, rwkv_w_rkv, rwkv_w_o, rwkv_w0, rwkv_w1, rwkv_w2, rwkv_a0, rwkv_a1, rwkv_a2,
           rwkv_g1, rwkv_g2, rwkv_k_k, rwkv_k_a, rwkv_r_k, rwkv_gn_g, rwkv_gn_b,
           s5_w_in, s5_lambda_re, s5_lambda_im, s5_log_dt, s5_b_re, s5_b_im, s5_c_re, s5_c_im, s5_d,
           s5_w_glu, s5_w_out, attn_w_qkv, attn_w_o, idx_w_q, idx_w_k, idx_w_w, rel_bias):
    p = dict(ln_g=ln_g, ln_b=ln_b, mlp_w1=mlp_w1, mlp_w2=mlp_w2, rwkv_mu=rwkv_mu, rwkv_w_rkv=rwkv_w_rkv,
             rwkv_w_o=rwkv_w_o, rwkv_w0=rwkv_w0, rwkv_w1=rwkv_w1, rwkv_w2=rwkv_w2, rwkv_a0=rwkv_a0,
             rwkv_a1=rwkv_a1, rwkv_a2=rwkv_a2, rwkv_g1=rwkv_g1, rwkv_g2=rwkv_g2, rwkv_k_k=rwkv_k_k,
             rwkv_k_a=rwkv_k_a, rwkv_r_k=rwkv_r_k, rwkv_gn_g=rwkv_gn_g, rwkv_gn_b=rwkv_gn_b,
             s5_w_in=s5_w_in, s5_lambda_re=s5_lambda_re, s5_lambda_im=s5_lambda_im, s5_log_dt=s5_log_dt,
             s5_b_re=s5_b_re, s5_b_im=s5_b_im, s5_c_re=s5_c_re, s5_c_im=s5_c_im, s5_d=s5_d,
             s5_w_glu=s5_w_glu, s5_w_out=s5_w_out, attn_w_qkv=attn_w_qkv, attn_w_o=attn_w_o,
             idx_w_q=idx_w_q, idx_w_k=idx_w_k, idx_w_w=idx_w_w, rel_bias=rel_bias)
    depth = ada_w.shape[0]
    Bp, Bs = c_prompt.shape[0], c_sample.shape[0]
    c_all = jax.nn.silu(jnp.concatenate([c_prompt, c_sample], axis=0).astype(F32))
    pad = -(-c_all.shape[0] // 16) * 16 - c_all.shape[0]
    c_all = jnp.pad(c_all, ((0, pad), (0, 0)))
    mods_p, mods_s = [], []
    for i in range(depth):
        mod = matmul(c_all, ada_w, (i,)) + ada_b[i]
        six = jnp.split(mod, 6, axis=-1)
        mods_p.append([z[:Bp] for z in six])
        mods_s.append([z[Bp:Bp + Bs] for z in six])
    states = dict(wkv=state_wkv, shift=state_shift, s5_re=state_s5_re, s5_im=state_s5_im)
    caches = (cache_k, cache_v, cache_kidx, page_table)
    out_p = _run(x_prompt, mods_p, False, p, caches, states)
    out_s = _run(x_sample, mods_s, True, p, caches, states)
    return (out_p[0], out_s[0]) + out_p[1:] + out_s[1:]
```

```python
import functools
import math

import jax
import jax.numpy as jnp
import numpy as np
from jax import lax
from jax.experimental import pallas as pl
from jax.experimental.pallas import tpu as pltpu

F32 = jnp.float32
BF16 = jnp.bfloat16

LANES = 128
SUBLANES = 8
VMEM_LIMIT_BYTES = 56 * 1024 * 1024

LN_EPS = 1e-5
GN_EPS = 64e-5
RWKV_HEAD_DIM = 64
S5_GROUP = 16
S5_SUB = 16
ATTN_HEAD_DIM = 128
IDX_DIM = 128
TOPK_MAX = 256
REL_BUCKETS = 32
REL_MAX_DIST = 1024
NEG_BIG = -1e30


def _cparams(sem):
    return pltpu.CompilerParams(dimension_semantics=sem, vmem_limit_bytes=VMEM_LIMIT_BYTES)


def _act(p, act):
    if act is None:
        return p
    if act == "relu2":
        r = jnp.maximum(p, 0.0)
        return r * r
    if act == "tanh":
        return jnp.tanh(p)
    if act == "sigmoid":
        return jax.nn.sigmoid(p)
    raise ValueError(act)


def _mm_kernel(a_ref, w_ref, o_ref, *scratch, nk, act):
    p = jnp.dot(a_ref[...].astype(BF16), w_ref[...].astype(BF16), preferred_element_type=F32)
    if nk == 1:
        o_ref[...] = _act(p, act).astype(o_ref.dtype)
        return
    acc_ref, = scratch
    k = pl.program_id(2)

    @pl.when(k == 0)
    def _():
        acc_ref[...] = p

    @pl.when(k > 0)
    def _():
        acc_ref[...] += p

    @pl.when(k == nk - 1)
    def _():
        o_ref[...] = _act(acc_ref[...], act).astype(o_ref.dtype)


def _pick(n, pref):
    if n <= pref:
        return n
    t = (pref // LANES) * LANES
    while t >= LANES:
        if n % t == 0:
            return t
        t -= LANES
    return n


def matmul(a, w, widx=(), *, act=None, out_dtype=F32, n_off=0, n_size=None, tm=1024, tn=512, tk=4096):
    M, K = a.shape
    Kw, N = w.shape[-2:]
    assert K == Kw and len(widx) == w.ndim - 2
    n_size = N if n_size is None else n_size
    tm = _pick(M, tm) if M % SUBLANES == 0 else M
    tn = _pick(n_size, tn)
    tk = _pick(K, tk)
    assert M % tm == 0 and n_size % tn == 0 and K % tk == 0 and n_off % tn == 0
    nk = K // tk
    joff = n_off // tn
    lead = tuple(widx)
    w_spec = pl.BlockSpec((None,) * len(lead) + (tk, tn), lambda i, j, k: lead + (k, j + joff))
    return pl.pallas_call(
        functools.partial(_mm_kernel, nk=nk, act=act),
        grid=(M // tm, n_size // tn, nk),
        in_specs=[pl.BlockSpec((tm, tk), lambda i, j, k: (i, k)), w_spec],
        out_specs=pl.BlockSpec((tm, tn), lambda i, j, k: (i, j)),
        out_shape=jax.ShapeDtypeStruct((M, n_size), out_dtype),
        scratch_shapes=[pltpu.VMEM((tm, tn), F32)] if nk > 1 else [],
        compiler_params=_cparams(("parallel", "parallel", "arbitrary")),
    )(a, w)


def _allsum_sublanes(p):
    p = p + pltpu.roll(p, 4, 0)
    p = p + pltpu.roll(p, 2, 0)
    return p + pltpu.roll(p, 1, 0)


def _wkv_kernel(r_ref, wl_ref, k_ref, v_ref, al_ref, g_ref, par_ref, s0_ref, z_ref, sT_ref,
                s_ref, op_ref, vt_ref, yt_ref, sc_ref, *, tc, n, nb, nh):
    c = pl.program_id(1)
    n8 = n // SUBLANES
    lw = nb * nh

    @pl.when(c == 0)
    def _():
        s_ref[...] = s0_ref[...]

    w0, a0, k_k, k_a, r_k, gn_g, gn_b = (par_ref[q] for q in range(7))
    cat = lambda ref, t: jnp.concatenate([ref[q, t] for q in range(nb)], axis=-1)
    fold = lambda x: x.reshape(n8, SUBLANES, lw)
    allsum = lambda x: _allsum_sublanes(jnp.sum(fold(x), axis=0))

    def prep(t, carry):
        r = cat(r_ref, t)
        k = cat(k_ref, t)
        w_log = -jax.nn.softplus(-(w0 + cat(wl_ref, t))) - 0.5
        w = jnp.exp(-jnp.exp(w_log))
        a = jax.nn.sigmoid(a0 + cat(al_ref, t))
        kk = k * k_k
        nrm = jnp.maximum(jnp.sqrt(allsum(kk * kk)), 1e-12)
        kk3 = fold(kk) / nrm[None]
        kh = k * (1.0 + (a - 1.0) * k_a)
        b3 = kk3 * fold(a)
        op_ref[t, 0] = -kk3
        op_ref[t, 1] = fold(w * r)
        op_ref[t, 2] = fold(w)
        op_ref[t, 3] = b3
        op_ref[t, 4] = fold(kh)
        vt_ref[t] = cat(v_ref, t)
        sc_ref[t, 0] = _allsum_sublanes(jnp.sum(b3 * fold(r), axis=0))
        sc_ref[t, 1] = allsum(kh * r)
        sc_ref[t, 2] = allsum(r * kh * r_k)
        return carry

    lax.fori_loop(0, tc, prep, 0, unroll=min(tc, 4))

    def step(t, carry):
        br = sc_ref[t, 0, 0:1]
        kr = sc_ref[t, 1, 0:1]

        def row(i, carry2):
            si = s_ref[i]
            sa = _allsum_sublanes(jnp.sum(si * op_ref[t, 0], axis=0))
            y0 = _allsum_sublanes(jnp.sum(si * op_ref[t, 1], axis=0))
            vi = vt_ref[t, pl.ds(i, 1), :]
            s_ref[i] = si * op_ref[t, 2] + sa[None] * op_ref[t, 3] + vi[None] * op_ref[t, 4]
            yt_ref[t, pl.ds(i, 1), :] = y0[0:1] + sa[0:1] * br + vi * kr
            return carry2

        lax.fori_loop(0, n, row, 0, unroll=8)
        return carry

    lax.fori_loop(0, tc, step, 0)

    def post(t, carry):
        y = yt_ref[t]
        mu = allsum(y) * (1.0 / n)
        yc = fold(y) - mu[None]
        var = _allsum_sublanes(jnp.sum(yc * yc, axis=0)) * (1.0 / n)
        yn = (yc * lax.rsqrt(var + GN_EPS)[None]).reshape(n, lw) * gn_g + gn_b
        z = (yn + (fold(vt_ref[t]) * sc_ref[t, 2][None]).reshape(n, lw)) * cat(g_ref, t)
        for q in range(nb):
            z_ref[q, t] = z[:, q * nh:(q + 1) * nh].astype(z_ref.dtype)
        return carry

    lax.fori_loop(0, tc, post, 0, unroll=min(tc, 4))

    @pl.when(c == pl.num_programs(1) - 1)
    def _():
        sT_ref[...] = s_ref[...]


def wkv_scan(r, wl, k, v, al, g, par, s0, *, tc, nb):
    B, T, n, nh = r.shape
    n8 = n // SUBLANES
    lw = nb * nh
    assert T % tc == 0 and B % nb == 0
    op = pl.BlockSpec((nb, tc, n, nh), lambda l, c: (l, c, 0, 0))
    st = pl.BlockSpec((n, n8, SUBLANES, lw), lambda l, c: (0, 0, 0, l))
    return pl.pallas_call(
        functools.partial(_wkv_kernel, tc=tc, n=n, nb=nb, nh=nh),
        grid=(B // nb, T // tc),
        in_specs=[op, op, op, op, op, op, pl.BlockSpec((7, n, lw), lambda l, c: (0, 0, 0)), st],
        out_specs=[op, st],
        out_shape=[jax.ShapeDtypeStruct((B, T, n, nh), BF16), jax.ShapeDtypeStruct(s0.shape, F32)],
        scratch_shapes=[pltpu.VMEM((n, n8, SUBLANES, lw), F32), pltpu.VMEM((tc, 5, n8, SUBLANES, lw), F32),
                        pltpu.VMEM((tc, n, lw), F32), pltpu.VMEM((tc, n, lw), F32),
                        pltpu.VMEM((tc, 3, SUBLANES, lw), F32)],
        compiler_params=_cparams(("parallel", "arbitrary")),
    )(r, wl, k, v, al, g, par, s0)


_HI = lax.Precision.HIGHEST


def _s5_fold(lam_re, lam_im, log_dt, b_re, b_im, c_re, c_im, C):
    G, P, I = b_re.shape
    dt = jnp.exp(log_dt.astype(F32))[:, None]
    lr, li = lam_re.astype(F32), lam_im.astype(F32)
    mag = jnp.exp(lr * dt)
    abar_re, abar_im = mag * jnp.cos(li * dt), mag * jnp.sin(li * dt)
    den = lr * lr + li * li
    nr, ni = abar_re - 1.0, abar_im
    coef_re = (nr * lr + ni * li) / den
    coef_im = (ni * lr - nr * li) / den
    br_, bi_ = b_re.astype(F32), b_im.astype(F32)
    bb_re = coef_re[..., None] * br_ - coef_im[..., None] * bi_
    bb_im = coef_re[..., None] * bi_ + coef_im[..., None] * br_
    cr, ci = c_re.astype(F32), c_im.astype(F32)
    pw_re, pw_im = [jnp.ones_like(abar_re)], [jnp.zeros_like(abar_im)]
    for _ in range(C):
        pr, pi = pw_re[-1], pw_im[-1]
        pw_re.append(pr * abar_re - pi * abar_im)
        pw_im.append(pr * abar_im + pi * abar_re)
    a_re, a_im = jnp.stack(pw_re, 1), jnp.stack(pw_im, 1)
    ab_re = a_re[:, :C, :, None] * bb_re[:, None] - a_im[:, :C, :, None] * bb_im[:, None]
    ab_im = a_re[:, :C, :, None] * bb_im[:, None] + a_im[:, :C, :, None] * bb_re[:, None]
    kern = (jnp.einsum("gjp,gtpi->gtij", cr, ab_re, precision=_HI)
            - jnp.einsum("gjp,gtpi->gtij", ci, ab_im, precision=_HI))
    tt = np.arange(C)[None, :] - np.arange(C)[:, None]
    m = kern[:, np.clip(tt, 0, C - 1)]
    m = jnp.where((tt >= 0)[None, :, :, None, None], m, 0.0)
    m = jnp.transpose(m, (0, 1, 3, 2, 4)).reshape(G, C * I, C * I)
    rev = np.arange(C - 1, -1, -1)
    w_re = jnp.transpose(ab_re[:, rev], (0, 1, 3, 2)).reshape(G, C * I, P)
    w_im = jnp.transpose(ab_im[:, rev], (0, 1, 3, 2)).reshape(G, C * I, P)
    ar1, ai1 = a_re[:, 1:], a_im[:, 1:]
    v_re = cr[:, None] * ar1[:, :, None, :] - ci[:, None] * ai1[:, :, None, :]
    v_im = -cr[:, None] * ai1[:, :, None, :] - ci[:, None] * ar1[:, :, None, :]
    v_re = jnp.transpose(v_re, (0, 3, 1, 2)).reshape(G, P, C * I)
    v_im = jnp.transpose(v_im, (0, 3, 1, 2)).reshape(G, P, C * I)
    return m, w_re, w_im, v_re, v_im, a_re[:, C][:, None], a_im[:, C][:, None]


def _s5_kernel(u_ref, m_ref, wre_ref, wim_ref, vre_ref, vim_ref, are_ref, aim_ref, x0re_ref, x0im_ref,
               y_ref, xre_ref, xim_ref, bur_ref, bui_ref, *, gb, nc, bp):
    for g in range(gb):
        u = u_ref[g]
        bur_ref[g] = jnp.dot(u, wre_ref[g], precision=_HI, preferred_element_type=F32)
        bui_ref[g] = jnp.dot(u, wim_ref[g], precision=_HI, preferred_element_type=F32)
    ar = are_ref[...]
    ai = aim_ref[...]

    def chunk(c, carry):
        xr, xi = carry
        rows = pl.ds(pl.multiple_of(c * bp, bp), bp)
        br = bur_ref[:, rows, :]
        bi = bui_ref[:, rows, :]
        bur_ref[:, rows, :] = xr
        bui_ref[:, rows, :] = xi
        return ar * xr - ai * xi + br, ar * xi + ai * xr + bi

    xr, xi = lax.fori_loop(0, nc, chunk, (x0re_ref[...], x0im_ref[...]))
    xre_ref[...] = xr
    xim_ref[...] = xi
    for g in range(gb):
        y = jnp.dot(u_ref[g], m_ref[g], precision=_HI, preferred_element_type=F32)
        y += jnp.dot(bur_ref[g], vre_ref[g], precision=_HI, preferred_element_type=F32)
        y += jnp.dot(bui_ref[g], vim_ref[g], precision=_HI, preferred_element_type=F32)
        y_ref[g] = y


def s5_scan(u, folded, x0_re, x0_im, *, gb):
    m, w_re, w_im, v_re, v_im, a_re, a_im = folded
    G, R, CI = u.shape
    P = w_re.shape[-1]
    bp = x0_re.shape[1]
    nc = R // bp
    assert G % gb == 0 and bp % SUBLANES == 0
    blk = lambda *s: pl.BlockSpec((gb,) + s, lambda g: (g,) + (0,) * len(s))
    return pl.pallas_call(
        functools.partial(_s5_kernel, gb=gb, nc=nc, bp=bp),
        grid=(G // gb,),
        in_specs=[blk(R, CI), blk(CI, CI), blk(CI, P), blk(CI, P), blk(P, CI), blk(P, CI),
                  blk(1, P), blk(1, P), blk(bp, P), blk(bp, P)],
        out_specs=[blk(R, CI), blk(bp, P), blk(bp, P)],
        out_shape=[jax.ShapeDtypeStruct((G, R, CI), F32), jax.ShapeDtypeStruct((G, bp, P), F32),
                   jax.ShapeDtypeStruct((G, bp, P), F32)],
        scratch_shapes=[pltpu.VMEM((gb, R, P), F32), pltpu.VMEM((gb, R, P), F32)],
        compiler_params=_cparams(("parallel",)),
    )(u, m, w_re, w_im, v_re, v_im, a_re, a_im, x0_re, x0_im)


def _ln_kernel(x_ref, f_ref, gate_ref, g_ref, b_ref, sc_ref, sh_ref, xo_ref, ho_ref, *, alpha):
    y = alpha * x_ref[...] + gate_ref[...] * f_ref[...]
    mu = jnp.mean(y, axis=-1, keepdims=True)
    yc = y - mu
    var = jnp.mean(yc * yc, axis=-1, keepdims=True)
    xn = yc * lax.rsqrt(var + LN_EPS) * g_ref[...] + b_ref[...]
    xo_ref[...] = xn
    ho_ref[...] = (xn * sc_ref[...] + sh_ref[...]).astype(ho_ref.dtype)


def residual_ln(x, f, gate, g, b, scale, shift, *, rows_per_batch, alpha, tm=256):
    M, D = x.shape
    B = gate.shape[0]
    if rows_per_batch == 1:
        tm = M
        mod = pl.BlockSpec((tm, D), lambda i: (i, 0))
        mods = (gate, scale, shift)
    else:
        tm = min(tm, rows_per_batch)
        assert rows_per_batch % tm == 0
        per = rows_per_batch // tm
        mod = pl.BlockSpec((None, 1, D), lambda i: (i // per, 0, 0))
        mods = tuple(z.reshape(B, 1, D) for z in (gate, scale, shift))
    row = pl.BlockSpec((tm, D), lambda i: (i, 0))
    vec = pl.BlockSpec((1, D), lambda i: (0, 0))
    return pl.pallas_call(
        functools.partial(_ln_kernel, alpha=alpha),
        grid=(M // tm,),
        in_specs=[row, row, mod, vec, vec, mod, mod],
        out_specs=[row, row],
        out_shape=[jax.ShapeDtypeStruct((M, D), F32), jax.ShapeDtypeStruct((M, D), BF16)],
        compiler_params=_cparams(("parallel",)),
    )(x, f, mods[0], g.reshape(1, D), b.reshape(1, D), mods[1], mods[2])


INT_MIN = -(2 ** 31)


def _order_key(x):
    bits = pltpu.bitcast(x, jnp.int32)
    return bits ^ ((bits >> 31) & 0x7FFFFFFF)


def _topk_member(key_ref, k_top):
    rows, L = key_ref.shape
    kf = jnp.float32(k_top)

    def count(pred):
        return jnp.sum(pred.astype(F32), axis=-1, keepdims=True)

    def bit_step(it, res):
        cand = res | (jnp.int32(1) << (31 - it))
        cnt = count(key_ref[...] >= (cand ^ INT_MIN))
        return jnp.where(cnt >= kf, cand, res)

    res = lax.fori_loop(0, 32, bit_step, jnp.zeros((rows, 1), jnp.int32))
    thr = res ^ INT_MIN
    key = key_ref[...]
    gt = key > thr
    eq = key == thr
    need = kf - count(gt)
    pos = lax.broadcasted_iota(jnp.int32, (rows, L), 1)
    nbits = max(1, (L - 1).bit_length())

    def pos_step(it, lim):
        cand = lim | (jnp.int32(1) << (nbits - 1 - it))
        cnt = count((key_ref[...] == thr) & (pos < cand))
        return jnp.where(cnt < need, cand, lim)

    lim = lax.fori_loop(0, nbits, pos_step, jnp.zeros((rows, 1), jnp.int32))
    return gt | (eq & (pos <= lim))


def _idx_kernel(q_ref, k_ref, w_ref, mask_ref, acc_ref, key_ref, *, tq, n_heads, k_top):
    i = pl.program_id(1)
    S = k_ref.shape[0]

    def tile(ii):
        se = (ii + 1) * tq
        kb = k_ref[:se, :].astype(BF16)
        acc = acc_ref.at[:, :se]
        keys = key_ref.at[:, :se]
        acc[...] = jnp.zeros((tq, se), F32)

        def head(h, carry):
            q = q_ref[:, pl.ds(pl.multiple_of(h * IDX_DIM, IDX_DIM), IDX_DIM)]
            s = lax.dot_general(q, kb, (((1,), (1,)), ((), ())), preferred_element_type=F32)
            wcol = pltpu.roll(w_ref[...], LANES - h, 1)[:, :1]
            acc[...] += wcol * jnp.maximum(s, 0.0)
            return carry

        lax.fori_loop(0, n_heads, head, 0)
        t_pos = ii * tq + lax.broadcasted_iota(jnp.int32, (tq, se), 0)
        s_pos = lax.broadcasted_iota(jnp.int32, (tq, se), 1)
        causal = s_pos <= t_pos
        keys[...] = _order_key(jnp.where(causal, acc[...], -jnp.inf))
        member = _topk_member(keys, k_top)
        mask_ref[:, :se] = jnp.where(member & causal, 0.0, NEG_BIG)
        if se < S:
            mask_ref[:, se:] = jnp.full((tq, S - se), NEG_BIG, F32)

    for ii in range(S // tq):
        pl.when(i == ii)(functools.partial(tile, ii))


def dsa_select_prompt(q_idx, k_idx, w_idx, *, B, T, k_top, tq=256):
    n_heads = q_idx.shape[1] // IDX_DIM
    tq = min(tq, T)
    nq = T // tq
    return pl.pallas_call(
        functools.partial(_idx_kernel, tq=tq, n_heads=n_heads, k_top=k_top),
        grid=(B, nq),
        in_specs=[pl.BlockSpec((tq, q_idx.shape[1]), lambda b, i: (b * nq + i, 0)),
                  pl.BlockSpec((T, IDX_DIM), lambda b, i: (b, 0)),
                  pl.BlockSpec((tq, LANES), lambda b, i: (b * nq + i, 0))],
        out_specs=pl.BlockSpec((tq, T), lambda b, i: (b * nq + i, 0)),
        out_shape=jax.ShapeDtypeStruct((B * T, T), F32),
        scratch_shapes=[pltpu.VMEM((tq, T), F32), pltpu.VMEM((tq, T), jnp.int32)],
        compiler_params=_cparams(("parallel", "parallel")),
    )(q_idx, k_idx, w_idx)


def _bucket_thresholds(max_dist):
    exact = REL_BUCKETS // 2
    d = np.arange(max_dist + 1)
    large = exact + np.floor(np.log(np.maximum(d, 1) / exact) / math.log(REL_MAX_DIST / exact)
                             * (REL_BUCKETS - exact) + 1e-9).astype(np.int64)
    bucket = np.where(d < exact, d, np.minimum(large, REL_BUCKETS - 1))
    return [int(np.argmax(bucket >= k)) if (bucket >= k).any() else max_dist + 1 for k in range(1, REL_BUCKETS)]


def rel_bucket_of(dist, max_dist):
    out = jnp.zeros(dist.shape, jnp.int32)
    for thr in _bucket_thresholds(max_dist):
        out = out + (dist >= thr).astype(jnp.int32)
    return out


def _attn_kernel(q_ref, k_ref, v_ref, mask_ref, bkt_ref, tab_ref, o_ref, *, scale):
    tq = q_ref.shape[0]
    S = k_ref.shape[0]
    i = pl.program_id(1)

    def tile(ii):
        se = (ii + 1) * tq
        q = q_ref[...].astype(BF16)
        kb = k_ref[:se, :].astype(BF16)
        logits = lax.dot_general(q, kb, (((1,), (1,)), ((), ())), preferred_element_type=F32) * scale
        tab = jnp.broadcast_to(tab_ref[...], (tq, LANES))
        bias = jnp.concatenate(
            [jnp.take_along_axis(tab, bkt_ref[:, c * LANES:(c + 1) * LANES], axis=1) for c in range(se // LANES)],
            axis=1)
        logits = logits + bias + mask_ref[:, :se]
        m = jnp.max(logits, axis=-1, keepdims=True)
        p = jnp.exp(logits - m)
        l = jnp.sum(p, axis=-1, keepdims=True)
        o = jnp.dot(p.astype(BF16), v_ref[:se, :].astype(BF16), preferred_element_type=F32)
        o_ref[...] = (o / l).astype(o_ref.dtype)

    for ii in range(S // tq):
        pl.when(i == ii)(functools.partial(tile, ii))


def dsa_attend_prompt(q, k, v, mask, bucket, bias_tab, *, B, T, tq=256):
    H = q.shape[1] // ATTN_HEAD_DIM
    tq = min(tq, T)
    nq = T // tq
    return pl.pallas_call(
        functools.partial(_attn_kernel, scale=ATTN_HEAD_DIM ** -0.5),
        grid=(B, nq, H),
        in_specs=[pl.BlockSpec((tq, ATTN_HEAD_DIM), lambda b, i, h: (b * nq + i, h)),
                  pl.BlockSpec((T, ATTN_HEAD_DIM), lambda b, i, h: (b, h)),
                  pl.BlockSpec((T, ATTN_HEAD_DIM), lambda b, i, h: (b, h)),
                  pl.BlockSpec((tq, T), lambda b, i, h: (b * nq + i, 0)),
                  pl.BlockSpec((tq, T), lambda b, i, h: (i, 0)),
                  pl.BlockSpec((None, 1, LANES), lambda b, i, h: (h, 0, 0))],
        out_specs=pl.BlockSpec((tq, ATTN_HEAD_DIM), lambda b, i, h: (b * nq + i, h)),
        out_shape=jax.ShapeDtypeStruct(q.shape, BF16),
        compiler_params=_cparams(("parallel", "parallel", "arbitrary")),
    )(q, k, v, mask, bucket, bias_tab)


def _page_score_kernel(pt_ref, q_ref, w_ref, kidx_ref, sc_ref):
    kb = kidx_ref[...].astype(BF16)
    s = lax.dot_general(q_ref[...], kb, (((1,), (1,)), ((), ())), preferred_element_type=F32)
    sc_ref[...] = jnp.sum(w_ref[...] * jnp.maximum(s, 0.0), axis=0, keepdims=True)


def dsa_page_scores(q_idx, w_idx, cache_kidx, page_table, j):
    B, HI, _ = q_idx.shape
    page = cache_kidx.shape[2]
    n_pages = page_table.shape[1]
    grid_spec = pltpu.PrefetchScalarGridSpec(
        num_scalar_prefetch=1,
        grid=(B, n_pages),
        in_specs=[pl.BlockSpec((None, HI, IDX_DIM), lambda b, p, pt: (b, 0, 0)),
                  pl.BlockSpec((None, HI, 1), lambda b, p, pt: (b, 0, 0)),
                  pl.BlockSpec((None, None, page, IDX_DIM), lambda b, p, pt: (j, pt[b, p], 0, 0))],
        out_specs=pl.BlockSpec((None, 1, page), lambda b, p, pt: (b, 0, p)),
    )
    return pl.pallas_call(
        _page_score_kernel,
        grid_spec=grid_spec,
        out_shape=jax.ShapeDtypeStruct((B, 1, n_pages * page), F32),
        compiler_params=_cparams(("parallel", "arbitrary")),
    )(page_table, q_idx, w_idx, cache_kidx)


def _sample_select_kernel(sc_ref, q_ref, w_ref, knew_ref, mask_ref, key_ref, *, past, k_top):
    B = sc_ref.shape[0]
    key_ref[:, :past] = _order_key(sc_ref[...])
    lane = lax.broadcasted_iota(jnp.int32, (1, LANES), 1)
    for b in range(B):
        kn = knew_ref[b].astype(BF16).astype(F32)
        s = jnp.sum(q_ref[b].astype(F32) * kn, axis=-1, keepdims=True)
        sc_new = jnp.sum(w_ref[b] * jnp.maximum(s, 0.0), axis=0, keepdims=True)
        tail = jnp.where(lane == 0, sc_new, -jnp.inf)
        key_ref[b:b + 1, past:] = _order_key(tail)
    member = _topk_member(key_ref, k_top)
    mask_ref[...] = jnp.where(member, 0.0, NEG_BIG)


def dsa_select_sample(scores, q_idx, w_idx, k_idx_new, *, k_top):
    B, past = scores.shape
    L = past + LANES
    return pl.pallas_call(
        functools.partial(_sample_select_kernel, past=past, k_top=k_top),
        out_shape=jax.ShapeDtypeStruct((B, L), F32),
        scratch_shapes=[pltpu.VMEM((B, L), jnp.int32)],
        compiler_params=pltpu.CompilerParams(vmem_limit_bytes=VMEM_LIMIT_BYTES),
    )(scores, q_idx, w_idx, k_idx_new)


def _page_attn_kernel(pt_ref, q_ref, k_ref, v_ref, bias_ref, mask_ref, knew_ref, vnew_ref, bnew_ref, mnew_ref,
                      o_ref, m_ref, l_ref, acc_ref, *, scale, H):
    p_idx = pl.program_id(1)

    @pl.when(p_idx == 0)
    def _():
        m_ref[...] = jnp.full_like(m_ref, NEG_BIG)
        l_ref[...] = jnp.zeros_like(l_ref)
        acc_ref[...] = jnp.zeros_like(acc_ref)

    q = q_ref[...].astype(BF16)
    cols = k_ref.shape[0]
    la = lax.dot_general(q, k_ref[...].astype(BF16), (((1,), (1,)), ((), ())), preferred_element_type=F32)
    la = la * scale + bias_ref[...]
    row_h = lax.broadcasted_iota(jnp.int32, (H, cols), 0)
    col_h = lax.broadcasted_iota(jnp.int32, (H, cols), 1) % H
    valid = (row_h == col_h) & (mask_ref[...] == 0.0)
    la = jnp.where(valid, la, NEG_BIG)
    m_old = m_ref[...]
    m_new = jnp.maximum(m_old, jnp.max(la, axis=-1, keepdims=True))
    alpha = jnp.exp(m_old - m_new)
    p = jnp.where(valid, jnp.exp(la - m_new), 0.0)
    l_ref[...] = alpha * l_ref[...] + jnp.sum(p, axis=-1, keepdims=True)
    acc_ref[...] = alpha * acc_ref[...] + jnp.dot(p.astype(BF16), v_ref[...].astype(BF16), preferred_element_type=F32)
    m_ref[...] = m_new

    @pl.when(p_idx == pl.num_programs(1) - 1)
    def _():
        kn = knew_ref[...].astype(BF16).astype(F32)
        s_new = jnp.sum(q.astype(F32) * kn, axis=-1, keepdims=True) * scale + bnew_ref[...]
        ok = mnew_ref[...][:, :1] == 0.0
        s_new = jnp.where(ok, s_new, NEG_BIG)
        m_o = m_ref[...]
        m_f = jnp.maximum(m_o, s_new)
        a_f = jnp.exp(m_o - m_f)
        p_new = jnp.where(ok, jnp.exp(s_new - m_f), 0.0)
        l_f = a_f * l_ref[...] + p_new
        acc = a_f * acc_ref[...] + p_new.astype(BF16).astype(F32) * vnew_ref[...].astype(BF16).astype(F32)
        o_ref[...] = acc / l_f


def dsa_attend_sample(q, cache_k, cache_v, page_table, j, bias_rows, mask_rows, k_new, v_new, bias_new, mask_new):
    B, H, Dh = q.shape
    cols = cache_k.shape[2]
    n_pages = page_table.shape[1]
    per_b = lambda *s: pl.BlockSpec((None,) + s, lambda b, p, pt: (b,) + (0,) * len(s))
    grid_spec = pltpu.PrefetchScalarGridSpec(
        num_scalar_prefetch=1,
        grid=(B, n_pages),
        in_specs=[per_b(H, Dh),
                  pl.BlockSpec((None, None, cols, Dh), lambda b, p, pt: (j, pt[b, p], 0, 0)),
                  pl.BlockSpec((None, None, cols, Dh), lambda b, p, pt: (j, pt[b, p], 0, 0)),
                  pl.BlockSpec((None, 1, cols), lambda b, p, pt: (p, 0, 0)),
                  pl.BlockSpec((None, None, 1, cols), lambda b, p, pt: (b, p, 0, 0)),
                  per_b(H, Dh), per_b(H, Dh),
                  pl.BlockSpec((H, 1), lambda b, p, pt: (0, 0)),
                  per_b(1, LANES)],
        out_specs=per_b(H, Dh),
        scratch_shapes=[pltpu.VMEM((H, 1), F32), pltpu.VMEM((H, 1), F32), pltpu.VMEM((H, Dh), F32)],
    )
    return pl.pallas_call(
        functools.partial(_page_attn_kernel, scale=Dh ** -0.5, H=H),
        grid_spec=grid_spec,
        out_shape=jax.ShapeDtypeStruct((B, H, Dh), F32),
        compiler_params=_cparams(("parallel", "arbitrary")),
    )(page_table, q, cache_k, cache_v, bias_rows, mask_rows, k_new, v_new, bias_new, mask_new)


def _rwkv_layer(h, B, T, shift0, wkv0, j, p):
    M, D = h.shape
    N = RWKV_HEAD_DIM
    H = D // N
    h3 = h.reshape(B, T, D)
    prev = jnp.concatenate([shift0[:, None].astype(F32), h3[:, :-1]], axis=1)
    d = prev - h3
    mu = p["rwkv_mu"][j]
    mix = [(h3 + d * mu[i]).astype(BF16).reshape(M, D) for i in range(6)]
    colp = lambda w: jnp.swapaxes(w.reshape(w.shape[0], H, N), 1, 2).reshape(w.shape[0], D)
    rowp = lambda w: jnp.swapaxes(w.reshape(H, N, w.shape[1]), 0, 1).reshape(D, w.shape[1])
    seq = lambda z: z.reshape(B, T, N, H)
    r = matmul(mix[0], colp(p["rwkv_w_rkv"][j, 0]))
    k = matmul(mix[2], colp(p["rwkv_w_rkv"][j, 1]))
    v = matmul(mix[3], colp(p["rwkv_w_rkv"][j, 2]))
    wl = matmul(matmul(mix[1], p["rwkv_w1"], (j,), act="tanh", out_dtype=BF16), colp(p["rwkv_w2"][j]))
    al = matmul(matmul(mix[4], p["rwkv_a1"], (j,), out_dtype=BF16), colp(p["rwkv_a2"][j]))
    g = matmul(matmul(mix[5], p["rwkv_g1"], (j,), act="sigmoid", out_dtype=BF16), colp(p["rwkv_g2"][j]))
    nb = min(B, max(1, LANES // H))
    vec = lambda z: jnp.tile(z.astype(F32).reshape(H, N).T, (1, nb))
    par = jnp.stack([vec(p[q][j]) for q in ("rwkv_w0", "rwkv_a0", "rwkv_k_k", "rwkv_k_a", "rwkv_r_k",
                                            "rwkv_gn_g", "rwkv_gn_b")])
    if wkv0 is None:
        s0 = jnp.zeros((N, N // SUBLANES, SUBLANES, B * H), F32)
    else:
        s0 = jnp.transpose(wkv0.astype(F32), (2, 3, 0, 1)).reshape(N, N // SUBLANES, SUBLANES, B * H)
    z, s_last = wkv_scan(seq(r), seq(wl), seq(k), seq(v), seq(al), seq(g), par, s0, tc=math.gcd(T, 16), nb=nb)
    wkv_new = jnp.transpose(s_last.reshape(N, N, B, H), (2, 3, 0, 1))
    out = matmul(z.reshape(M, D), rowp(p["rwkv_w_o"][j]))
    return out, h3[:, -1], wkv_new


def _s5_layer(hb, B, T, x0_re, x0_im, j, p):
    M, D = hb.shape
    I = S5_GROUP
    G = D // I
    P = p["s5_lambda_re"].shape[-1]
    C = S5_SUB if T % S5_SUB == 0 else 1
    nc = T // C
    bp = -(-B // SUBLANES) * SUBLANES
    u = matmul(hb, p["s5_w_in"], (j,))
    folded = _s5_fold(p["s5_lambda_re"][j], p["s5_lambda_im"][j], p["s5_log_dt"][j], p["s5_b_re"][j],
                      p["s5_b_im"][j], p["s5_c_re"][j], p["s5_c_im"][j], C)
    ug = jnp.transpose(u.reshape(B, nc, C, G, I), (3, 1, 0, 2, 4))
    ug = jnp.pad(ug, ((0, 0), (0, 0), (0, bp - B), (0, 0), (0, 0))).reshape(G, nc * bp, C * I)
    if x0_re is None:
        x0r = x0i = jnp.zeros((G, bp, P), F32)
    else:
        st = lambda z: jnp.pad(jnp.transpose(z.astype(F32), (1, 0, 2)), ((0, 0), (0, bp - B), (0, 0)))
        x0r, x0i = st(x0_re), st(x0_im)
    gb = 4 if C > 1 else 8
    yg, xr, xi = s5_scan(ug, folded, x0r, x0i, gb=gb)
    y = jnp.transpose(yg.reshape(G, nc, bp, C, I)[:, :, :B], (2, 1, 3, 0, 4)).reshape(M, D)
    y = y + p["s5_d"][j].astype(F32) * u
    z = jax.nn.gelu(y).astype(BF16)
    ab = matmul(z, p["s5_w_glu"], (j,))
    gl = (ab[:, :D] * jax.nn.sigmoid(ab[:, D:])).astype(BF16)
    out = matmul(gl, p["s5_w_out"], (j,))
    fin = lambda z: jnp.transpose(z[:, :B], (1, 0, 2))
    return out, fin(xr), fin(xi)


def _dsa_project(hb, j, p):
    D = hb.shape[1]
    q = matmul(hb, p["attn_w_qkv"], (j,), n_off=0, n_size=D)
    k = matmul(hb, p["attn_w_qkv"], (j,), n_off=D, n_size=D)
    v = matmul(hb, p["attn_w_qkv"], (j,), n_off=2 * D, n_size=D)
    q_idx = matmul(hb, p["idx_w_q"], (j,), out_dtype=BF16)
    k_idx = matmul(hb, p["idx_w_k"], (j,))
    n_ih = p["idx_w_w"].shape[-1]
    w_idx = matmul(hb, p["idx_w_w"], (j,)) * ((n_ih * IDX_DIM) ** -0.5)
    return q, k, v, q_idx, k_idx, w_idx


def _dsa_prompt_layer(hb, B, T, j, p):
    M, D = hb.shape
    H = D // ATTN_HEAD_DIM
    q, k, v, q_idx, k_idx, w_idx = _dsa_project(hb, j, p)
    k_top = max(1, min(TOPK_MAX, T // 4))
    w_pad = jnp.pad(w_idx, ((0, 0), (0, LANES - w_idx.shape[1])))
    mask = dsa_select_prompt(q_idx, k_idx, w_pad, B=B, T=T, k_top=k_top)
    pos = jnp.arange(T, dtype=jnp.int32)
    bucket = rel_bucket_of(pos[:, None] - pos[None, :], T)
    tab = jnp.pad(p["rel_bias"].astype(F32).T, ((0, 0), (0, LANES - REL_BUCKETS))).reshape(H, 1, LANES)
    o = dsa_attend_prompt(q, k, v, mask, bucket, tab, B=B, T=T)
    out = matmul(o, p["attn_w_o"], (j,))
    return out, k.reshape(B, T, H, ATTN_HEAD_DIM), v.reshape(B, T, H, ATTN_HEAD_DIM), k_idx.reshape(B, T, IDX_DIM)


def _dsa_sample_layer(hb, B, j, p, cache_k, cache_v, cache_kidx, page_table):
    M, D = hb.shape
    H = D // ATTN_HEAD_DIM
    q, k, v, q_idx, k_idx, w_idx = _dsa_project(hb, j, p)
    page = cache_kidx.shape[2]
    n_pages = page_table.shape[1]
    past = n_pages * page
    n_ih = w_idx.shape[1]
    q_idx3 = q_idx.reshape(B, n_ih, IDX_DIM)
    w_idx3 = w_idx.reshape(B, n_ih, 1)
    scores = dsa_page_scores(q_idx3, w_idx3, cache_kidx, page_table, j).reshape(B, past)
    k_top = max(1, min(TOPK_MAX, (past + 1) // 4))
    mask = dsa_select_sample(scores, q_idx3, w_idx3, k_idx.reshape(B, 1, IDX_DIM), k_top=k_top)
    dist = past - jnp.arange(past, dtype=jnp.int32)
    bias_rows = p["rel_bias"].astype(F32)[rel_bucket_of(dist, past)]
    bias_rows = bias_rows.reshape(n_pages, 1, page * H)
    mask_rows = jnp.repeat(mask[:, :past], H, axis=1).reshape(B, n_pages, 1, page * H)
    pool = cache_k.shape[1]
    ck = cache_k.reshape(cache_k.shape[0], pool, page * H, ATTN_HEAD_DIM)
    cv = cache_v.reshape(cache_v.shape[0], pool, page * H, ATTN_HEAD_DIM)
    hd = lambda z: z.reshape(B, H, ATTN_HEAD_DIM)
    bias_new = p["rel_bias"].astype(F32)[0].reshape(H, 1)
    mask_new = jnp.broadcast_to(mask[:, past:past + 1], (B, LANES)).reshape(B, 1, LANES)
    o = dsa_attend_sample(hd(q), ck, cv, page_table, j, bias_rows, mask_rows, hd(k), hd(v), bias_new, mask_new)
    out = matmul(o.reshape(M, D).astype(BF16), p["attn_w_o"], (j,))
    return out, k.reshape(B, 1, H, ATTN_HEAD_DIM), v.reshape(B, 1, H, ATTN_HEAD_DIM), k_idx.reshape(B, 1, IDX_DIM)


def _run(x, mods, sample, p, caches, states):
    B, T, D = x.shape
    M = B * T
    depth = len(mods)
    alpha = (2.0 * depth) ** 0.25
    modulate = lambda z, sc, sh: (z.reshape(B, T, D) * (1.0 + sc)[:, None] + sh[:, None]).reshape(M, D)
    xf = x.reshape(M, D).astype(F32)
    h = modulate(xf, mods[0][1], mods[0][0])
    hb = h.astype(BF16)
    st = ([], [], [])
    for i in range(depth):
        sh1, sc1, g1, sh2, sc2, g2 = mods[i]
        kind, j = i % 3, i // 3
        if kind == 0:
            if i > 0:
                h = modulate(xf, sc1, sh1)
            s0, w0 = (states["shift"][j], states["wkv"][j]) if sample else (jnp.zeros((B, D), F32), None)
            out, shift_new, wkv_new = _rwkv_layer(h, B, T, s0, w0, j, p)
            st[0].append((wkv_new, shift_new))
        elif kind == 1:
            x0r, x0i = (states["s5_re"][j], states["s5_im"][j]) if sample else (None, None)
            out, hr, hi = _s5_layer(hb, B, T, x0r, x0i, j, p)
            st[1].append((hr, hi))
        else:
            if sample:
                out, kn, vn, kin = _dsa_sample_layer(hb, B, j, p, *caches)
            else:
                out, kn, vn, kin = _dsa_prompt_layer(hb, B, T, j, p)
            st[2].append((kn, vn, kin))
        xf, hb = residual_ln(xf, out, 1.0 + g1, p["ln_g"][i, 0], p["ln_b"][i, 0], 1.0 + sc2, sh2,
                             rows_per_batch=T, alpha=alpha)
        f = matmul(matmul(hb, p["mlp_w1"], (i,), act="relu2", out_dtype=BF16), p["mlp_w2"], (i,), tk=2048, tn=1024)
        if i + 1 < depth:
            nsc, nsh = 1.0 + mods[i + 1][1], mods[i + 1][0]
        else:
            nsc, nsh = jnp.ones_like(sc1), jnp.zeros_like(sh1)
        xf, hb = residual_ln(xf, f, 1.0 + g2, p["ln_g"][i, 1], p["ln_b"][i, 1], nsc, nsh,
                             rows_per_batch=T, alpha=alpha)
    stk = lambda kind, k: jnp.stack([s[k] for s in st[kind]])
    return (xf.reshape(B, T, D), stk(0, 0), stk(0, 1), stk(1, 0), stk(1, 1), stk(2, 0), stk(2, 1), stk(2, 2))


def kernel(x_prompt, x_sample, cache_k, cache_v, cache_kidx, state_wkv, state_shift, state_s5_re, state_s5_im,
           page_table, c_prompt, c_sample, ada_w, ada_b, ln_g, ln_b, mlp_w1, mlp_w2,
           rwkv_mu, rwkv_w_rkv, rwkv_w_o, rwkv_w0, rwkv_w1, rwkv_w2, rwkv_a0, rwkv_a1, rwkv_a2,
           rwkv_g1, rwkv_g2, rwkv_k_k, rwkv_k_a, rwkv_r_k, rwkv_gn_g, rwkv_gn_b,
           s5_w_in, s5_lambda_re, s5_lambda_im, s5_log_dt, s5_b_re, s5_b_im, s5_c_re, s5_c_im, s5_d,
           s5_w_glu, s5_w_out, attn_w_qkv, attn_w_o, idx_w_q, idx_w_k, idx_w_w, rel_bias):
    p = dict(ln_g=ln_g, ln_b=ln_b, mlp_w1=mlp_w1, mlp_w2=mlp_w2, rwkv_mu=rwkv_mu, rwkv_w_rkv=rwkv_w_rkv,
             rwkv_w_o=rwkv_w_o, rwkv_w0=rwkv_w0, rwkv_w1=rwkv_w1, rwkv_w2=rwkv_w2, rwkv_a0=rwkv_a0,
             rwkv_a1=rwkv_a1, rwkv_a2=rwkv_a2, rwkv_g1=rwkv_g1, rwkv_g2=rwkv_g2, rwkv_k_k=rwkv_k_k,
             rwkv_k_a=rwkv_k_a, rwkv_r_k=rwkv_r_k, rwkv_gn_g=rwkv_gn_g, rwkv_gn_b=rwkv_gn_b,
             s5_w_in=s5_w_in, s5_lambda_re=s5_lambda_re, s5_lambda_im=s5_lambda_im, s5_log_dt=s5_log_dt,
             s5_b_re=s5_b_re, s5_b_im=s5_b_im, s5_c_re=s5_c_re, s5_c_im=s5_c_im, s5_d=s5_d,
             s5_w_glu=s5_w_glu, s5_w_out=s5_w_out, attn_w_qkv=attn_w_qkv, attn_w_o=attn_w_o,
             idx_w_q=idx_w_q, idx_w_k=idx_w_k, idx_w_w=idx_w_w, rel_bias=rel_bias)
    depth = ada_w.shape[0]
    Bp, Bs = c_prompt.shape[0], c_sample.shape[0]
    c_all = jax.nn.silu(jnp.concatenate([c_prompt, c_sample], axis=0).astype(F32))
    pad = -(-c_all.shape[0] // 16) * 16 - c_all.shape[0]
    c_all = jnp.pad(c_all, ((0, pad), (0, 0)))
    mods_p, mods_s = [], []
    for i in range(depth):
        mod = matmul(c_all, ada_w, (i,)) + ada_b[i]
        six = jnp.split(mod, 6, axis=-1)
        mods_p.append([z[:Bp] for z in six])
        mods_s.append([z[Bp:Bp + Bs] for z in six])
    states = dict(wkv=state_wkv, shift=state_shift, s5_re=state_s5_re, s5_im=state_s5_im)
    caches = (cache_k, cache_v, cache_kidx, page_table)
    out_p = _run(x_prompt, mods_p, False, p, caches, states)
    out_s = _run(x_sample, mods_s, True, p, caches, states)
    return (out_p[0], out_s[0]) + out_p[1:] + out_s[1:]
```

```python
import functools
import math

import jax
import jax.numpy as jnp
import numpy as np
from jax import lax
from jax.experimental import pallas as pl
from jax.experimental.pallas import tpu as pltpu

F32 = jnp.float32
BF16 = jnp.bfloat16

LANES = 128
SUBLANES = 8
VMEM_LIMIT_BYTES = 56 * 1024 * 1024

LN_EPS = 1e-5
GN_EPS = 64e-5
RWKV_HEAD_DIM = 64
S5_GROUP = 16
S5_SUB = 16
ATTN_HEAD_DIM = 128
IDX_DIM = 128
TOPK_MAX = 256
REL_BUCKETS = 32
REL_MAX_DIST = 1024
NEG_BIG = -1e30


def _cparams(sem):
    return pltpu.CompilerParams(dimension_semantics=sem, vmem_limit_bytes=VMEM_LIMIT_BYTES)


def _act(p, act):
    if act is None:
        return p
    if act == "relu2":
        r = jnp.maximum(p, 0.0)
        return r * r
    if act == "tanh":
        return jnp.tanh(p)
    if act == "sigmoid":
        return jax.nn.sigmoid(p)
    raise ValueError(act)


def _mm_kernel(a_ref, w_ref, o_ref, *scratch, nk, act):
    p = jnp.dot(a_ref[...].astype(BF16), w_ref[...].astype(BF16), preferred_element_type=F32)
    if nk == 1:
        o_ref[...] = _act(p, act).astype(o_ref.dtype)
        return
    acc_ref, = scratch
    k = pl.program_id(2)

    @pl.when(k == 0)
    def _():
        acc_ref[...] = p

    @pl.when(k > 0)
    def _():
        acc_ref[...] += p

    @pl.when(k == nk - 1)
    def _():
        o_ref[...] = _act(acc_ref[...], act).astype(o_ref.dtype)


def _pick(n, pref):
    if n <= pref:
        return n
    t = (pref // LANES) * LANES
    while t >= LANES:
        if n % t == 0:
            return t
        t -= LANES
    return n


def matmul(a, w, widx=(), *, act=None, out_dtype=F32, n_off=0, n_size=None, tm=1024, tn=512, tk=4096):
    M, K = a.shape
    Kw, N = w.shape[-2:]
    assert K == Kw and len(widx) == w.ndim - 2
    n_size = N if n_size is None else n_size
    tm = _pick(M, tm) if M % SUBLANES == 0 else M
    tn = _pick(n_size, tn)
    tk = _pick(K, tk)
    assert M % tm == 0 and n_size % tn == 0 and K % tk == 0 and n_off % tn == 0
    nk = K // tk
    joff = n_off // tn
    lead = tuple(widx)
    w_spec = pl.BlockSpec((None,) * len(lead) + (tk, tn), lambda i, j, k: lead + (k, j + joff))
    return pl.pallas_call(
        functools.partial(_mm_kernel, nk=nk, act=act),
        grid=(M // tm, n_size // tn, nk),
        in_specs=[pl.BlockSpec((tm, tk), lambda i, j, k: (i, k)), w_spec],
        out_specs=pl.BlockSpec((tm, tn), lambda i, j, k: (i, j)),
        out_shape=jax.ShapeDtypeStruct((M, n_size), out_dtype),
        scratch_shapes=[pltpu.VMEM((tm, tn), F32)] if nk > 1 else [],
        compiler_params=_cparams(("parallel", "parallel", "arbitrary")),
    )(a, w)


def _allsum_sublanes(p):
    p = p + pltpu.roll(p, 4, 0)
    p = p + pltpu.roll(p, 2, 0)
    return p + pltpu.roll(p, 1, 0)


def _wkv_kernel(r_ref, wl_ref, k_ref, v_ref, al_ref, g_ref, par_ref, s0_ref, z_ref, sT_ref,
                s_ref, op_ref, vt_ref, yt_ref, sc_ref, *, tc, n, nb, nh):
    c = pl.program_id(1)
    n8 = n // SUBLANES
    lw = nb * nh

    @pl.when(c == 0)
    def _():
        s_ref[...] = s0_ref[...]

    w0, a0, k_k, k_a, r_k, gn_g, gn_b = (par_ref[q] for q in range(7))
    cat = lambda ref, t: ref[:, t].reshape(lw, n).T
    fold = lambda x: x.reshape(n8, SUBLANES, lw)
    allsum = lambda x: _allsum_sublanes(jnp.sum(fold(x), axis=0))

    def prep(t, carry):
        r = cat(r_ref, t)
        k = cat(k_ref, t)
        w_log = -jax.nn.softplus(-(w0 + cat(wl_ref, t))) - 0.5
        w = jnp.exp(-jnp.exp(w_log))
        a = jax.nn.sigmoid(a0 + cat(al_ref, t))
        kk = k * k_k
        nrm = jnp.maximum(jnp.sqrt(allsum(kk * kk)), 1e-12)
        kk3 = fold(kk) / nrm[None]
        kh = k * (1.0 + (a - 1.0) * k_a)
        b3 = kk3 * fold(a)
        op_ref[t, 0] = -kk3
        op_ref[t, 1] = fold(w * r)
        op_ref[t, 2] = fold(w)
        op_ref[t, 3] = b3
        op_ref[t, 4] = fold(kh)
        vt_ref[t] = cat(v_ref, t)
        sc_ref[t, 0] = _allsum_sublanes(jnp.sum(b3 * fold(r), axis=0))
        sc_ref[t, 1] = allsum(kh * r)
        sc_ref[t, 2] = allsum(r * kh * r_k)
        return carry

    lax.fori_loop(0, tc, prep, 0, unroll=min(tc, 4))

    def step(t, carry):
        br = sc_ref[t, 0, 0:1]
        kr = sc_ref[t, 1, 0:1]

        def row(i, carry2):
            si = s_ref[i]
            sa = _allsum_sublanes(jnp.sum(si * op_ref[t, 0], axis=0))
            y0 = _allsum_sublanes(jnp.sum(si * op_ref[t, 1], axis=0))
            vi = vt_ref[t, pl.ds(i, 1), :]
            s_ref[i] = si * op_ref[t, 2] + sa[None] * op_ref[t, 3] + vi[None] * op_ref[t, 4]
            yt_ref[t, pl.ds(i, 1), :] = y0[0:1] + sa[0:1] * br + vi * kr
            return carry2

        lax.fori_loop(0, n, row, 0, unroll=8)
        return carry

    lax.fori_loop(0, tc, step, 0)

    def post(t, carry):
        y = yt_ref[t]
        mu = allsum(y) * (1.0 / n)
        yc = fold(y) - mu[None]
        var = _allsum_sublanes(jnp.sum(yc * yc, axis=0)) * (1.0 / n)
        yn = (yc * lax.rsqrt(var + GN_EPS)[None]).reshape(n, lw) * gn_g + gn_b
        z = (yn + (fold(vt_ref[t]) * sc_ref[t, 2][None]).reshape(n, lw)) * cat(g_ref, t)
        zt = z.T
        for q in range(nb):
            z_ref[q, t] = zt[q * nh:(q + 1) * nh].astype(z_ref.dtype)
        return carry

    lax.fori_loop(0, tc, post, 0, unroll=min(tc, 4))

    @pl.when(c == pl.num_programs(1) - 1)
    def _():
        sT_ref[...] = s_ref[...]


def wkv_scan(r, wl, k, v, al, g, par, s0, *, tc, nb):
    B, T, nh, n = r.shape
    n8 = n // SUBLANES
    lw = nb * nh
    assert T % tc == 0 and B % nb == 0
    op = pl.BlockSpec((nb, tc, nh, n), lambda l, c: (l, c, 0, 0))
    st = pl.BlockSpec((n, n8, SUBLANES, lw), lambda l, c: (0, 0, 0, l))
    return pl.pallas_call(
        functools.partial(_wkv_kernel, tc=tc, n=n, nb=nb, nh=nh),
        grid=(B // nb, T // tc),
        in_specs=[op, op, op, op, op, op, pl.BlockSpec((7, n, lw), lambda l, c: (0, 0, 0)), st],
        out_specs=[op, st],
        out_shape=[jax.ShapeDtypeStruct((B, T, nh, n), BF16), jax.ShapeDtypeStruct(s0.shape, F32)],
        scratch_shapes=[pltpu.VMEM((n, n8, SUBLANES, lw), F32), pltpu.VMEM((tc, 5, n8, SUBLANES, lw), F32),
                        pltpu.VMEM((tc, n, lw), F32), pltpu.VMEM((tc, n, lw), F32),
                        pltpu.VMEM((tc, 3, SUBLANES, lw), F32)],
        compiler_params=_cparams(("parallel", "arbitrary")),
    )(r, wl, k, v, al, g, par, s0)


_HI = lax.Precision.HIGHEST


def _s5_fold(lam_re, lam_im, log_dt, b_re, b_im, c_re, c_im, C):
    G, P, I = b_re.shape
    dt = jnp.exp(log_dt.astype(F32))[:, None]
    lr, li = lam_re.astype(F32), lam_im.astype(F32)
    mag = jnp.exp(lr * dt)
    abar_re, abar_im = mag * jnp.cos(li * dt), mag * jnp.sin(li * dt)
    den = lr * lr + li * li
    nr, ni = abar_re - 1.0, abar_im
    coef_re = (nr * lr + ni * li) / den
    coef_im = (ni * lr - nr * li) / den
    br_, bi_ = b_re.astype(F32), b_im.astype(F32)
    bb_re = coef_re[..., None] * br_ - coef_im[..., None] * bi_
    bb_im = coef_re[..., None] * bi_ + coef_im[..., None] * br_
    cr, ci = c_re.astype(F32), c_im.astype(F32)
    pw_re, pw_im = [jnp.ones_like(abar_re)], [jnp.zeros_like(abar_im)]
    for _ in range(C):
        pr, pi = pw_re[-1], pw_im[-1]
        pw_re.append(pr * abar_re - pi * abar_im)
        pw_im.append(pr * abar_im + pi * abar_re)
    a_re, a_im = jnp.stack(pw_re, 1), jnp.stack(pw_im, 1)
    ab_re = a_re[:, :C, :, None] * bb_re[:, None] - a_im[:, :C, :, None] * bb_im[:, None]
    ab_im = a_re[:, :C, :, None] * bb_im[:, None] + a_im[:, :C, :, None] * bb_re[:, None]
    kern = (jnp.einsum("gjp,gtpi->gtij", cr, ab_re, precision=_HI)
            - jnp.einsum("gjp,gtpi->gtij", ci, ab_im, precision=_HI))
    tt = np.arange(C)[None, :] - np.arange(C)[:, None]
    m = kern[:, np.clip(tt, 0, C - 1)]
    m = jnp.where((tt >= 0)[None, :, :, None, None], m, 0.0)
    m = jnp.transpose(m, (0, 1, 3, 2, 4)).reshape(G, C * I, C * I)
    rev = np.arange(C - 1, -1, -1)
    w_re = jnp.transpose(ab_re[:, rev], (0, 1, 3, 2)).reshape(G, C * I, P)
    w_im = jnp.transpose(ab_im[:, rev], (0, 1, 3, 2)).reshape(G, C * I, P)
    ar1, ai1 = a_re[:, 1:], a_im[:, 1:]
    v_re = cr[:, None] * ar1[:, :, None, :] - ci[:, None] * ai1[:, :, None, :]
    v_im = -cr[:, None] * ai1[:, :, None, :] - ci[:, None] * ar1[:, :, None, :]
    v_re = jnp.transpose(v_re, (0, 3, 1, 2)).reshape(G, P, C * I)
    v_im = jnp.transpose(v_im, (0, 3, 1, 2)).reshape(G, P, C * I)
    return m, w_re, w_im, v_re, v_im, a_re[:, C][:, None], a_im[:, C][:, None]


def _s5_kernel(u_ref, m_ref, wre_ref, wim_ref, vre_ref, vim_ref, are_ref, aim_ref, x0re_ref, x0im_ref,
               y_ref, xre_ref, xim_ref, bur_ref, bui_ref, *, gb, nc, bp):
    for g in range(gb):
        u = u_ref[g]
        bur_ref[g] = jnp.dot(u, wre_ref[g], precision=_HI, preferred_element_type=F32)
        bui_ref[g] = jnp.dot(u, wim_ref[g], precision=_HI, preferred_element_type=F32)
    ar = are_ref[...]
    ai = aim_ref[...]

    def chunk(c, carry):
        xr, xi = carry
        rows = pl.ds(pl.multiple_of(c * bp, bp), bp)
        br = bur_ref[:, rows, :]
        bi = bui_ref[:, rows, :]
        bur_ref[:, rows, :] = xr
        bui_ref[:, rows, :] = xi
        return ar * xr - ai * xi + br, ar * xi + ai * xr + bi

    xr, xi = lax.fori_loop(0, nc, chunk, (x0re_ref[...], x0im_ref[...]))
    xre_ref[...] = xr
    xim_ref[...] = xi
    for g in range(gb):
        y = jnp.dot(u_ref[g], m_ref[g], precision=_HI, preferred_element_type=F32)
        y += jnp.dot(bur_ref[g], vre_ref[g], precision=_HI, preferred_element_type=F32)
        y += jnp.dot(bui_ref[g], vim_ref[g], precision=_HI, preferred_element_type=F32)
        y_ref[g] = y


def s5_scan(u, folded, x0_re, x0_im, *, gb):
    m, w_re, w_im, v_re, v_im, a_re, a_im = folded
    G, R, CI = u.shape
    P = w_re.shape[-1]
    bp = x0_re.shape[1]
    nc = R // bp
    assert G % gb == 0 and bp % SUBLANES == 0
    blk = lambda *s: pl.BlockSpec((gb,) + s, lambda g: (g,) + (0,) * len(s))
    return pl.pallas_call(
        functools.partial(_s5_kernel, gb=gb, nc=nc, bp=bp),
        grid=(G // gb,),
        in_specs=[blk(R, CI), blk(CI, CI), blk(CI, P), blk(CI, P), blk(P, CI), blk(P, CI),
                  blk(1, P), blk(1, P), blk(bp, P), blk(bp, P)],
        out_specs=[blk(R, CI), blk(bp, P), blk(bp, P)],
        out_shape=[jax.ShapeDtypeStruct((G, R, CI), F32), jax.ShapeDtypeStruct((G, bp, P), F32),
                   jax.ShapeDtypeStruct((G, bp, P), F32)],
        scratch_shapes=[pltpu.VMEM((gb, R, P), F32), pltpu.VMEM((gb, R, P), F32)],
        compiler_params=_cparams(("parallel",)),
    )(u, m, w_re, w_im, v_re, v_im, a_re, a_im, x0_re, x0_im)


def _s5_blockdiag(folded, gl):
    m, w_re, w_im, v_re, v_im, a_re, a_im = folded
    G, CI, P = w_re.shape
    I = S5_GROUP
    C = CI // I
    NG = G // gl
    eye = jnp.eye(gl, dtype=F32)
    mf = jnp.einsum("ngsitj,gh->nsgithj", m.reshape(NG, gl, C, I, C, I), eye).reshape(NG, C * gl * I, C * gl * I)
    wf = jnp.stack([jnp.einsum("ngsip,gh->nsgihp", w.reshape(NG, gl, C, I, P), eye) for w in (w_re, w_im)], axis=4)
    wf = wf.reshape(NG, C * gl * I, 2 * gl * P)
    vf = jnp.stack([jnp.einsum("ngptj,gh->ngpthj", v.reshape(NG, gl, P, C, I), eye) for v in (v_re, v_im)], axis=1)
    vf = vf.reshape(NG, 2 * gl * P, C * gl * I)
    lanes = lambda a: a.reshape(NG, 1, gl * P)
    return mf.astype(BF16), wf.astype(BF16), vf.astype(BF16), lanes(a_re), lanes(a_im)


def _s5n_kernel(u_ref, d_ref, m_ref, w_ref, v_ref, are_ref, aim_ref, x0_ref, z_ref, xT_ref,
                xcat_ref, bu_ref, xprev_ref, y_ref, *, B, nc, C, nsplit, gp):
    hf = pl.program_id(1)
    R = B * nc
    tsub = C // nsplit
    nq = 2 * gp // LANES

    @pl.when(hf == 0)
    def _():
        for s in range(C):
            xcat_ref[:, s * LANES:(s + 1) * LANES] = u_ref[pl.ds(s, R, stride=C), :].astype(BF16)
        bu = jnp.dot(xcat_ref[...], w_ref[...], preferred_element_type=F32)
        for q in range(nq):
            bu_ref[q] = bu[:, q * LANES:(q + 1) * LANES]
        ar = are_ref[...]
        ai = aim_ref[...]

        def block(c, carry):
            xr, xi = carry
            rows = pl.ds(c, B, stride=nc)
            x = jnp.concatenate([xr, xi], axis=-1)
            for q in range(nq):
                xprev_ref.at[q][rows, :] = x[:, q * LANES:(q + 1) * LANES]
            bu = jnp.concatenate([bu_ref.at[q][rows, :] for q in range(nq)], axis=-1)
            return ar * xr - ai * xi + bu[:, :gp], ar * xi + ai * xr + bu[:, gp:]

        x0 = x0_ref[...]
        xr, xi = lax.fori_loop(0, nc, block, (x0[:, :gp], x0[:, gp:]))
        xT_ref[...] = jnp.concatenate([xr, xi], axis=-1)

    xprev = jnp.concatenate([xprev_ref[q] for q in range(nq)], axis=-1).astype(BF16)
    y = jnp.dot(xcat_ref[...], m_ref[...], preferred_element_type=F32)
    y = y + jnp.dot(xprev, v_ref[...], preferred_element_type=F32)
    for tt in range(tsub):
        y_ref[pl.ds(hf * tsub + tt, R, stride=C), :] = y[:, tt * LANES:(tt + 1) * LANES]

    @pl.when(hf == nsplit - 1)
    def _():
        val = y_ref[...] + d_ref[...] * u_ref[...]
        z_ref[...] = jax.nn.gelu(val).astype(z_ref.dtype)


def s5_natural(u, d, bd, x0, *, B, C, nsplit=2):
    mf, wf, vf, a_re, a_im = bd
    M, D = u.shape
    NG = mf.shape[0]
    K = mf.shape[1]
    gp = a_re.shape[-1]
    nc = M // (B * C)
    R = B * nc
    assert D == NG * LANES and K == C * LANES and C % nsplit == 0
    ncol = K // nsplit
    return pl.pallas_call(
        functools.partial(_s5n_kernel, B=B, nc=nc, C=C, nsplit=nsplit, gp=gp),
        grid=(NG, nsplit),
        in_specs=[pl.BlockSpec((M, LANES), lambda n, f: (0, n)),
                  pl.BlockSpec((1, LANES), lambda n, f: (0, n)),
                  pl.BlockSpec((None, K, ncol), lambda n, f: (n, 0, f)),
                  pl.BlockSpec((None, K, 2 * gp), lambda n, f: (n, 0, 0)),
                  pl.BlockSpec((None, 2 * gp, ncol), lambda n, f: (n, 0, f)),
                  pl.BlockSpec((None, 1, gp), lambda n, f: (n, 0, 0)),
                  pl.BlockSpec((None, 1, gp), lambda n, f: (n, 0, 0)),
                  pl.BlockSpec((None, B, 2 * gp), lambda n, f: (n, 0, 0))],
        out_specs=[pl.BlockSpec((M, LANES), lambda n, f: (0, n)),
                   pl.BlockSpec((None, B, 2 * gp), lambda n, f: (n, 0, 0))],
        out_shape=[jax.ShapeDtypeStruct((M, D), BF16), jax.ShapeDtypeStruct((NG, B, 2 * gp), F32)],
        scratch_shapes=[pltpu.VMEM((R, K), BF16), pltpu.VMEM((2 * gp // LANES, R, LANES), F32),
                        pltpu.VMEM((2 * gp // LANES, R, LANES), F32),
                        pltpu.VMEM((M, LANES), F32)],
        compiler_params=_cparams(("parallel", "arbitrary")),
    )(u, d, mf, wf, vf, a_re, a_im, x0)


def _ln_kernel(x_ref, f_ref, gate_ref, g_ref, b_ref, sc_ref, sh_ref, xo_ref, ho_ref, *, alpha):
    y = alpha * x_ref[...] + gate_ref[...] * f_ref[...]
    mu = jnp.mean(y, axis=-1, keepdims=True)
    yc = y - mu
    var = jnp.mean(yc * yc, axis=-1, keepdims=True)
    xn = yc * lax.rsqrt(var + LN_EPS) * g_ref[...] + b_ref[...]
    xo_ref[...] = xn
    ho_ref[...] = (xn * sc_ref[...] + sh_ref[...]).astype(ho_ref.dtype)


def residual_ln(x, f, gate, g, b, scale, shift, *, rows_per_batch, alpha, tm=256):
    M, D = x.shape
    B = gate.shape[0]
    if rows_per_batch == 1:
        tm = M
        mod = pl.BlockSpec((tm, D), lambda i: (i, 0))
        mods = (gate, scale, shift)
    else:
        tm = min(tm, rows_per_batch)
        assert rows_per_batch % tm == 0
        per = rows_per_batch // tm
        mod = pl.BlockSpec((None, 1, D), lambda i: (i // per, 0, 0))
        mods = tuple(z.reshape(B, 1, D) for z in (gate, scale, shift))
    row = pl.BlockSpec((tm, D), lambda i: (i, 0))
    vec = pl.BlockSpec((1, D), lambda i: (0, 0))
    return pl.pallas_call(
        functools.partial(_ln_kernel, alpha=alpha),
        grid=(M // tm,),
        in_specs=[row, row, mod, vec, vec, mod, mod],
        out_specs=[row, row],
        out_shape=[jax.ShapeDtypeStruct((M, D), F32), jax.ShapeDtypeStruct((M, D), BF16)],
        compiler_params=_cparams(("parallel",)),
    )(x, f, mods[0], g.reshape(1, D), b.reshape(1, D), mods[1], mods[2])


INT_MIN = -(2 ** 31)


def _order_key(x):
    bits = pltpu.bitcast(x, jnp.int32)
    return bits ^ ((bits >> 31) & 0x7FFFFFFF)


def _topk_member(key_ref, k_top):
    rows, L = key_ref.shape
    kf = jnp.float32(k_top)

    def count(pred):
        return jnp.sum(pred.astype(F32), axis=-1, keepdims=True)

    def bit_step(it, res):
        cand = res | (jnp.int32(1) << (31 - it))
        cnt = count(key_ref[...] >= (cand ^ INT_MIN))
        return jnp.where(cnt >= kf, cand, res)

    res = lax.fori_loop(0, 32, bit_step, jnp.zeros((rows, 1), jnp.int32))
    thr = res ^ INT_MIN
    key = key_ref[...]
    gt = key > thr
    eq = key == thr
    need = kf - count(gt)
    pos = lax.broadcasted_iota(jnp.int32, (rows, L), 1)
    nbits = max(1, (L - 1).bit_length())

    def pos_step(it, lim):
        cand = lim | (jnp.int32(1) << (nbits - 1 - it))
        cnt = count((key_ref[...] == thr) & (pos < cand))
        return jnp.where(cnt < need, cand, lim)

    lim = lax.fori_loop(0, nbits, pos_step, jnp.zeros((rows, 1), jnp.int32))
    return gt | (eq & (pos <= lim))


def _idx_kernel(q_ref, k_ref, w_ref, mask_ref, acc_ref, key_ref, *, tq, n_heads, k_top):
    i = pl.program_id(1)
    S = k_ref.shape[0]

    def tile(ii):
        se = (ii + 1) * tq
        kb = k_ref[:se, :].astype(BF16)
        acc = acc_ref.at[:, :se]
        keys = key_ref.at[:, :se]
        acc[...] = jnp.zeros((tq, se), F32)

        def weighted(h):
            q = q_ref[:, pl.ds(pl.multiple_of(h * IDX_DIM, IDX_DIM), IDX_DIM)]
            s = lax.dot_general(q, kb, (((1,), (1,)), ((), ())), preferred_element_type=F32)
            wcol = pltpu.roll(w_ref[...], LANES - h, 1)[:, :1]
            return wcol * jnp.maximum(s, 0.0)

        def heads(hh, carry):
            part = weighted(hh * hpp)
            for e in range(1, hpp):
                part = part + weighted(hh * hpp + e)
            acc[...] += part
            return carry

        hpp = 2 if n_heads % 2 == 0 else 1
        lax.fori_loop(0, n_heads // hpp, heads, 0)
        t_pos = ii * tq + lax.broadcasted_iota(jnp.int32, (tq, se), 0)
        s_pos = lax.broadcasted_iota(jnp.int32, (tq, se), 1)
        causal = s_pos <= t_pos
        keys[...] = _order_key(jnp.where(causal, acc[...], -jnp.inf))
        member = _topk_member(keys, k_top)
        mask_ref[:, :se] = jnp.where(member & causal, 0.0, NEG_BIG)
        if se < S:
            mask_ref[:, se:] = jnp.full((tq, S - se), NEG_BIG, F32)

    for ii in range(S // tq):
        pl.when(i == ii)(functools.partial(tile, ii))


def dsa_select_prompt(q_idx, k_idx, w_idx, *, B, T, k_top, tq=256):
    n_heads = q_idx.shape[1] // IDX_DIM
    tq = min(tq, T)
    nq = T // tq
    return pl.pallas_call(
        functools.partial(_idx_kernel, tq=tq, n_heads=n_heads, k_top=k_top),
        grid=(B, nq),
        in_specs=[pl.BlockSpec((tq, q_idx.shape[1]), lambda b, i: (b * nq + i, 0)),
                  pl.BlockSpec((T, IDX_DIM), lambda b, i: (b, 0)),
                  pl.BlockSpec((tq, LANES), lambda b, i: (b * nq + i, 0))],
        out_specs=pl.BlockSpec((tq, T), lambda b, i: (b * nq + i, 0)),
        out_shape=jax.ShapeDtypeStruct((B * T, T), F32),
        scratch_shapes=[pltpu.VMEM((tq, T), F32), pltpu.VMEM((tq, T), jnp.int32)],
        compiler_params=_cparams(("parallel", "parallel")),
    )(q_idx, k_idx, w_idx)


def _bucket_thresholds(max_dist):
    exact = REL_BUCKETS // 2
    d = np.arange(max_dist + 1)
    large = exact + np.floor(np.log(np.maximum(d, 1) / exact) / math.log(REL_MAX_DIST / exact)
                             * (REL_BUCKETS - exact) + 1e-9).astype(np.int64)
    bucket = np.where(d < exact, d, np.minimum(large, REL_BUCKETS - 1))
    return [int(np.argmax(bucket >= k)) if (bucket >= k).any() else max_dist + 1 for k in range(1, REL_BUCKETS)]


def rel_bucket_of(dist, max_dist):
    out = jnp.zeros(dist.shape, jnp.int32)
    for thr in _bucket_thresholds(max_dist):
        out = out + (dist >= thr).astype(jnp.int32)
    return out


def _attn_kernel(q_ref, k_ref, v_ref, mask_ref, bkt_ref, tab_ref, o_ref, *, scale):
    tq = q_ref.shape[0]
    S = k_ref.shape[0]
    i = pl.program_id(1)

    def tile(ii):
        se = (ii + 1) * tq
        q = q_ref[...].astype(BF16)
        kb = k_ref[:se, :].astype(BF16)
        logits = lax.dot_general(q, kb, (((1,), (1,)), ((), ())), preferred_element_type=F32) * scale
        tab = jnp.broadcast_to(tab_ref[...], (tq, LANES))
        bias = jnp.concatenate(
            [jnp.take_along_axis(tab, bkt_ref[:, c * LANES:(c + 1) * LANES], axis=1) for c in range(se // LANES)],
            axis=1)
        logits = logits + bias + mask_ref[:, :se]
        m = jnp.max(logits, axis=-1, keepdims=True)
        p = jnp.exp(logits - m)
        l = jnp.sum(p, axis=-1, keepdims=True)
        o = jnp.dot(p.astype(BF16), v_ref[:se, :].astype(BF16), preferred_element_type=F32)
        o_ref[...] = (o / l).astype(o_ref.dtype)

    for ii in range(S // tq):
        pl.when(i == ii)(functools.partial(tile, ii))


def dsa_attend_prompt(q, k, v, mask, bucket, bias_tab, *, B, T, tq=256):
    H = q.shape[1] // ATTN_HEAD_DIM
    tq = min(tq, T)
    nq = T // tq
    return pl.pallas_call(
        functools.partial(_attn_kernel, scale=ATTN_HEAD_DIM ** -0.5),
        grid=(B, nq, H),
        in_specs=[pl.BlockSpec((tq, ATTN_HEAD_DIM), lambda b, i, h: (b * nq + i, h)),
                  pl.BlockSpec((T, ATTN_HEAD_DIM), lambda b, i, h: (b, h)),
                  pl.BlockSpec((T, ATTN_HEAD_DIM), lambda b, i, h: (b, h)),
                  pl.BlockSpec((tq, T), lambda b, i, h: (b * nq + i, 0)),
                  pl.BlockSpec((tq, T), lambda b, i, h: (i, 0)),
                  pl.BlockSpec((None, 1, LANES), lambda b, i, h: (h, 0, 0))],
        out_specs=pl.BlockSpec((tq, ATTN_HEAD_DIM), lambda b, i, h: (b * nq + i, h)),
        out_shape=jax.ShapeDtypeStruct(q.shape, BF16),
        compiler_params=_cparams(("parallel", "parallel", "arbitrary")),
    )(q, k, v, mask, bucket, bias_tab)


def _page_score_kernel(pt_ref, q_ref, w_ref, kidx_ref, sc_ref):
    kb = kidx_ref[...].astype(BF16)
    s = lax.dot_general(q_ref[...], kb, (((1,), (1,)), ((), ())), preferred_element_type=F32)
    sc_ref[...] = jnp.sum(w_ref[...] * jnp.maximum(s, 0.0), axis=0, keepdims=True)


def dsa_page_scores(q_idx, w_idx, cache_kidx, page_table, j):
    B, HI, _ = q_idx.shape
    page = cache_kidx.shape[2]
    n_pages = page_table.shape[1]
    grid_spec = pltpu.PrefetchScalarGridSpec(
        num_scalar_prefetch=1,
        grid=(B, n_pages),
        in_specs=[pl.BlockSpec((None, HI, IDX_DIM), lambda b, p, pt: (b, 0, 0)),
                  pl.BlockSpec((None, HI, 1), lambda b, p, pt: (b, 0, 0)),
                  pl.BlockSpec((None, None, page, IDX_DIM), lambda b, p, pt: (j, pt[b, p], 0, 0))],
        out_specs=pl.BlockSpec((None, 1, page), lambda b, p, pt: (b, 0, p)),
    )
    return pl.pallas_call(
        _page_score_kernel,
        grid_spec=grid_spec,
        out_shape=jax.ShapeDtypeStruct((B, 1, n_pages * page), F32),
        compiler_params=_cparams(("parallel", "arbitrary")),
    )(page_table, q_idx, w_idx, cache_kidx)


def _sample_select_kernel(sc_ref, q_ref, w_ref, knew_ref, mask_ref, key_ref, *, past, k_top):
    B = sc_ref.shape[0]
    key_ref[:, :past] = _order_key(sc_ref[...])
    lane = lax.broadcasted_iota(jnp.int32, (1, LANES), 1)
    for b in range(B):
        kn = knew_ref[b].astype(BF16).astype(F32)
        s = jnp.sum(q_ref[b].astype(F32) * kn, axis=-1, keepdims=True)
        sc_new = jnp.sum(w_ref[b] * jnp.maximum(s, 0.0), axis=0, keepdims=True)
        tail = jnp.where(lane == 0, sc_new, -jnp.inf)
        key_ref[b:b + 1, past:] = _order_key(tail)
    member = _topk_member(key_ref, k_top)
    mask_ref[...] = jnp.where(member, 0.0, NEG_BIG)


def dsa_select_sample(scores, q_idx, w_idx, k_idx_new, *, k_top):
    B, past = scores.shape
    L = past + LANES
    return pl.pallas_call(
        functools.partial(_sample_select_kernel, past=past, k_top=k_top),
        out_shape=jax.ShapeDtypeStruct((B, L), F32),
        scratch_shapes=[pltpu.VMEM((B, L), jnp.int32)],
        compiler_params=pltpu.CompilerParams(vmem_limit_bytes=VMEM_LIMIT_BYTES),
    )(scores, q_idx, w_idx, k_idx_new)


def _page_attn_kernel(pt_ref, q_ref, k_ref, v_ref, bias_ref, mask_ref, knew_ref, vnew_ref, bnew_ref, mnew_ref,
                      o_ref, m_ref, l_ref, acc_ref, *, scale, H):
    p_idx = pl.program_id(1)

    @pl.when(p_idx == 0)
    def _():
        m_ref[...] = jnp.full_like(m_ref, NEG_BIG)
        l_ref[...] = jnp.zeros_like(l_ref)
        acc_ref[...] = jnp.zeros_like(acc_ref)

    q = q_ref[...].astype(BF16)
    cols = k_ref.shape[0]
    la = lax.dot_general(q, k_ref[...].astype(BF16), (((1,), (1,)), ((), ())), preferred_element_type=F32)
    la = la * scale + bias_ref[...]
    row_h = lax.broadcasted_iota(jnp.int32, (H, cols), 0)
    col_h = lax.broadcasted_iota(jnp.int32, (H, cols), 1) % H
    valid = (row_h == col_h) & (mask_ref[...] == 0.0)
    la = jnp.where(valid, la, NEG_BIG)
    m_old = m_ref[...]
    m_new = jnp.maximum(m_old, jnp.max(la, axis=-1, keepdims=True))
    alpha = jnp.exp(m_old - m_new)
    p = jnp.where(valid, jnp.exp(la - m_new), 0.0)
    l_ref[...] = alpha * l_ref[...] + jnp.sum(p, axis=-1, keepdims=True)
    acc_ref[...] = alpha * acc_ref[...] + jnp.dot(p.astype(BF16), v_ref[...].astype(BF16), preferred_element_type=F32)
    m_ref[...] = m_new

    @pl.when(p_idx == pl.num_programs(1) - 1)
    def _():
        kn = knew_ref[...].astype(BF16).astype(F32)
        s_new = jnp.sum(q.astype(F32) * kn, axis=-1, keepdims=True) * scale + bnew_ref[...]
        ok = mnew_ref[...][:, :1] == 0.0
        s_new = jnp.where(ok, s_new, NEG_BIG)
        m_o = m_ref[...]
        m_f = jnp.maximum(m_o, s_new)
        a_f = jnp.exp(m_o - m_f)
        p_new = jnp.where(ok, jnp.exp(s_new - m_f), 0.0)
        l_f = a_f * l_ref[...] + p_new
        acc = a_f * acc_ref[...] + p_new.astype(BF16).astype(F32) * vnew_ref[...].astype(BF16).astype(F32)
        o_ref[...] = acc / l_f


def dsa_attend_sample(q, cache_k, cache_v, page_table, j, bias_rows, mask_rows, k_new, v_new, bias_new, mask_new):
    B, H, Dh = q.shape
    cols = cache_k.shape[2]
    n_pages = page_table.shape[1]
    per_b = lambda *s: pl.BlockSpec((None,) + s, lambda b, p, pt: (b,) + (0,) * len(s))
    grid_spec = pltpu.PrefetchScalarGridSpec(
        num_scalar_prefetch=1,
        grid=(B, n_pages),
        in_specs=[per_b(H, Dh),
                  pl.BlockSpec((None, None, cols, Dh), lambda b, p, pt: (j, pt[b, p], 0, 0)),
                  pl.BlockSpec((None, None, cols, Dh), lambda b, p, pt: (j, pt[b, p], 0, 0)),
                  pl.BlockSpec((None, 1, cols), lambda b, p, pt: (p, 0, 0)),
                  pl.BlockSpec((None, None, 1, cols), lambda b, p, pt: (b, p, 0, 0)),
                  per_b(H, Dh), per_b(H, Dh),
                  pl.BlockSpec((H, 1), lambda b, p, pt: (0, 0)),
                  per_b(1, LANES)],
        out_specs=per_b(H, Dh),
        scratch_shapes=[pltpu.VMEM((H, 1), F32), pltpu.VMEM((H, 1), F32), pltpu.VMEM((H, Dh), F32)],
    )
    return pl.pallas_call(
        functools.partial(_page_attn_kernel, scale=Dh ** -0.5, H=H),
        grid_spec=grid_spec,
        out_shape=jax.ShapeDtypeStruct((B, H, Dh), F32),
        compiler_params=_cparams(("parallel", "arbitrary")),
    )(page_table, q, cache_k, cache_v, bias_rows, mask_rows, k_new, v_new, bias_new, mask_new)


def _rwkv_layer(h, B, T, shift0, wkv0, j, p):
    M, D = h.shape
    N = RWKV_HEAD_DIM
    H = D // N
    h3 = h.reshape(B, T, D)
    prev = jnp.concatenate([shift0[:, None].astype(F32), h3[:, :-1]], axis=1)
    d = prev - h3
    mu = p["rwkv_mu"][j]
    mix = [(h3 + d * mu[i]).astype(BF16).reshape(M, D) for i in range(6)]
    seq = lambda z: z.reshape(B, T, H, N)
    r = matmul(mix[0], p["rwkv_w_rkv"], (j, 0))
    k = matmul(mix[2], p["rwkv_w_rkv"], (j, 1))
    v = matmul(mix[3], p["rwkv_w_rkv"], (j, 2))
    wl = matmul(matmul(mix[1], p["rwkv_w1"], (j,), act="tanh", out_dtype=BF16), p["rwkv_w2"], (j,))
    al = matmul(matmul(mix[4], p["rwkv_a1"], (j,), out_dtype=BF16), p["rwkv_a2"], (j,))
    g = matmul(matmul(mix[5], p["rwkv_g1"], (j,), act="sigmoid", out_dtype=BF16), p["rwkv_g2"], (j,))
    nb = min(B, max(1, LANES // H))
    vec = lambda z: jnp.tile(z.astype(F32).reshape(H, N).T, (1, nb))
    par = jnp.stack([vec(p[q][j]) for q in ("rwkv_w0", "rwkv_a0", "rwkv_k_k", "rwkv_k_a", "rwkv_r_k",
                                            "rwkv_gn_g", "rwkv_gn_b")])
    if wkv0 is None:
        s0 = jnp.zeros((N, N // SUBLANES, SUBLANES, B * H), F32)
    else:
        s0 = jnp.transpose(wkv0.astype(F32), (2, 3, 0, 1)).reshape(N, N // SUBLANES, SUBLANES, B * H)
    z, s_last = wkv_scan(seq(r), seq(wl), seq(k), seq(v), seq(al), seq(g), par, s0, tc=math.gcd(T, 16), nb=nb)
    wkv_new = jnp.transpose(s_last.reshape(N, N, B, H), (2, 3, 0, 1))
    out = matmul(z.reshape(M, D), p["rwkv_w_o"], (j,))
    return out, h3[:, -1], wkv_new


def _s5_layer(hb, B, T, x0_re, x0_im, j, p):
    M, D = hb.shape
    I = S5_GROUP
    G = D // I
    P = p["s5_lambda_re"].shape[-1]
    C = S5_SUB if T % S5_SUB == 0 else 1
    nc = T // C
    bp = -(-B // SUBLANES) * SUBLANES
    u = matmul(hb, p["s5_w_in"], (j,))
    folded = _s5_fold(p["s5_lambda_re"][j], p["s5_lambda_im"][j], p["s5_log_dt"][j], p["s5_b_re"][j],
                      p["s5_b_im"][j], p["s5_c_re"][j], p["s5_c_im"][j], C)
    gl = LANES // I
    if C > 1 and G % gl == 0:
        NG = G // gl
        if x0_re is None:
            x0 = jnp.zeros((NG, B, 2 * gl * P), F32)
        else:
            st = lambda z: jnp.transpose(z.astype(F32).reshape(B, NG, gl * P), (1, 0, 2))
            x0 = jnp.concatenate([st(x0_re), st(x0_im)], axis=-1)
        z, xT = s5_natural(u, p["s5_d"][j].astype(F32).reshape(1, D), _s5_blockdiag(folded, gl), x0, B=B, C=C)
        fin = lambda q: jnp.transpose(q.reshape(NG, B, gl, P), (1, 0, 2, 3)).reshape(B, G, P)
        hr, hi = fin(xT[..., :gl * P]), fin(xT[..., gl * P:])
    else:
        ug = jnp.transpose(u.reshape(B, nc, C, G, I), (3, 1, 0, 2, 4))
        ug = jnp.pad(ug, ((0, 0), (0, 0), (0, bp - B), (0, 0), (0, 0))).reshape(G, nc * bp, C * I)
        if x0_re is None:
            x0r = x0i = jnp.zeros((G, bp, P), F32)
        else:
            st = lambda z: jnp.pad(jnp.transpose(z.astype(F32), (1, 0, 2)), ((0, 0), (0, bp - B), (0, 0)))
            x0r, x0i = st(x0_re), st(x0_im)
        yg, xr, xi = s5_scan(ug, folded, x0r, x0i, gb=8)
        y = jnp.transpose(yg.reshape(G, nc, bp, C, I)[:, :, :B], (2, 1, 3, 0, 4)).reshape(M, D)
        z = jax.nn.gelu(y + p["s5_d"][j].astype(F32) * u).astype(BF16)
        fin = lambda q: jnp.transpose(q[:, :B], (1, 0, 2))
        hr, hi = fin(xr), fin(xi)
    ab = matmul(z, p["s5_w_glu"], (j,))
    glu = (ab[:, :D] * jax.nn.sigmoid(ab[:, D:])).astype(BF16)
    out = matmul(glu, p["s5_w_out"], (j,))
    return out, hr, hi


def _dsa_project(hb, j, p):
    D = hb.shape[1]
    q = matmul(hb, p["attn_w_qkv"], (j,), n_off=0, n_size=D)
    k = matmul(hb, p["attn_w_qkv"], (j,), n_off=D, n_size=D)
    v = matmul(hb, p["attn_w_qkv"], (j,), n_off=2 * D, n_size=D)
    q_idx = matmul(hb, p["idx_w_q"], (j,), out_dtype=BF16)
    k_idx = matmul(hb, p["idx_w_k"], (j,))
    n_ih = p["idx_w_w"].shape[-1]
    w_idx = matmul(hb, p["idx_w_w"], (j,)) * ((n_ih * IDX_DIM) ** -0.5)
    return q, k, v, q_idx, k_idx, w_idx


def _dsa_prompt_layer(hb, B, T, j, p):
    M, D = hb.shape
    H = D // ATTN_HEAD_DIM
    q, k, v, q_idx, k_idx, w_idx = _dsa_project(hb, j, p)
    k_top = max(1, min(TOPK_MAX, T // 4))
    w_pad = jnp.pad(w_idx, ((0, 0), (0, LANES - w_idx.shape[1])))
    mask = dsa_select_prompt(q_idx, k_idx, w_pad, B=B, T=T, k_top=k_top)
    pos = jnp.arange(T, dtype=jnp.int32)
    bucket = rel_bucket_of(pos[:, None] - pos[None, :], T)
    tab = jnp.pad(p["rel_bias"].astype(F32).T, ((0, 0), (0, LANES - REL_BUCKETS))).reshape(H, 1, LANES)
    o = dsa_attend_prompt(q, k, v, mask, bucket, tab, B=B, T=T)
    out = matmul(o, p["attn_w_o"], (j,))
    return out, k.reshape(B, T, H, ATTN_HEAD_DIM), v.reshape(B, T, H, ATTN_HEAD_DIM), k_idx.reshape(B, T, IDX_DIM)


def _dsa_sample_layer(hb, B, j, p, cache_k, cache_v, cache_kidx, page_table):
    M, D = hb.shape
    H = D // ATTN_HEAD_DIM
    q, k, v, q_idx, k_idx, w_idx = _dsa_project(hb, j, p)
    page = cache_kidx.shape[2]
    n_pages = page_table.shape[1]
    past = n_pages * page
    n_ih = w_idx.shape[1]
    q_idx3 = q_idx.reshape(B, n_ih, IDX_DIM)
    w_idx3 = w_idx.reshape(B, n_ih, 1)
    scores = dsa_page_scores(q_idx3, w_idx3, cache_kidx, page_table, j).reshape(B, past)
    k_top = max(1, min(TOPK_MAX, (past + 1) // 4))
    mask = dsa_select_sample(scores, q_idx3, w_idx3, k_idx.reshape(B, 1, IDX_DIM), k_top=k_top)
    dist = past - jnp.arange(past, dtype=jnp.int32)
    bias_rows = p["rel_bias"].astype(F32)[rel_bucket_of(dist, past)]
    bias_rows = bias_rows.reshape(n_pages, 1, page * H)
    mask_rows = jnp.repeat(mask[:, :past], H, axis=1).reshape(B, n_pages, 1, page * H)
    pool = cache_k.shape[1]
    ck = cache_k.reshape(cache_k.shape[0], pool, page * H, ATTN_HEAD_DIM)
    cv = cache_v.reshape(cache_v.shape[0], pool, page * H, ATTN_HEAD_DIM)
    hd = lambda z: z.reshape(B, H, ATTN_HEAD_DIM)
    bias_new = p["rel_bias"].astype(F32)[0].reshape(H, 1)
    mask_new = jnp.broadcast_to(mask[:, past:past + 1], (B, LANES)).reshape(B, 1, LANES)
    o = dsa_attend_sample(hd(q), ck, cv, page_table, j, bias_rows, mask_rows, hd(k), hd(v), bias_new, mask_new)
    out = matmul(o.reshape(M, D).astype(BF16), p["attn_w_o"], (j,))
    return out, k.reshape(B, 1, H, ATTN_HEAD_DIM), v.reshape(B, 1, H, ATTN_HEAD_DIM), k_idx.reshape(B, 1, IDX_DIM)


def _run(x, mods, sample, p, caches, states):
    B, T, D = x.shape
    M = B * T
    depth = len(mods)
    alpha = (2.0 * depth) ** 0.25
    modulate = lambda z, sc, sh: (z.reshape(B, T, D) * (1.0 + sc)[:, None] + sh[:, None]).reshape(M, D)
    xf = x.reshape(M, D).astype(F32)
    h = modulate(xf, mods[0][1], mods[0][0])
    hb = h.astype(BF16)
    st = ([], [], [])
    for i in range(depth):
        sh1, sc1, g1, sh2, sc2, g2 = mods[i]
        kind, j = i % 3, i // 3
        if kind == 0:
            if i > 0:
                h = modulate(xf, sc1, sh1)
            s0, w0 = (states["shift"][j], states["wkv"][j]) if sample else (jnp.zeros((B, D), F32), None)
            out, shift_new, wkv_new = _rwkv_layer(h, B, T, s0, w0, j, p)
            st[0].append((wkv_new, shift_new))
        elif kind == 1:
            x0r, x0i = (states["s5_re"][j], states["s5_im"][j]) if sample else (None, None)
            out, hr, hi = _s5_layer(hb, B, T, x0r, x0i, j, p)
            st[1].append((hr, hi))
        else:
            if sample:
                out, kn, vn, kin = _dsa_sample_layer(hb, B, j, p, *caches)
            else:
                out, kn, vn, kin = _dsa_prompt_layer(hb, B, T, j, p)
            st[2].append((kn, vn, kin))
        xf, hb = residual_ln(xf, out, 1.0 + g1, p["ln_g"][i, 0], p["ln_b"][i, 0], 1.0 + sc2, sh2,
                             rows_per_batch=T, alpha=alpha)
        f = matmul(matmul(hb, p["mlp_w1"], (i,), act="relu2", out_dtype=BF16), p["mlp_w2"], (i,), tk=2048, tn=1024)
        if i + 1 < depth:
            nsc, nsh = 1.0 + mods[i + 1][1], mods[i + 1][0]
        else:
            nsc, nsh = jnp.ones_like(sc1), jnp.zeros_like(sh1)
        xf, hb = residual_ln(xf, f, 1.0 + g2, p["ln_g"][i, 1], p["ln_b"][i, 1], nsc, nsh,
                             rows_per_batch=T, alpha=alpha)
    stk = lambda kind, k: jnp.stack([s[k] for s in st[kind]])
    return (xf.reshape(B, T, D), stk(0, 0), stk(0, 1), stk(1, 0), stk(1, 1), stk(2, 0), stk(2, 1), stk(2, 2))


def kernel(x_prompt, x_sample, cache_k, cache_v, cache_kidx, state_wkv, state_shift, state_s5_re, state_s5_im,
           page_table, c_prompt, c_sample, ada_w, ada_b, ln_g, ln_b, mlp_w1, mlp_w2,
           rwkv_mu, rwkv_w_rkv, rwkv_w_o, rwkv_w0, rwkv_w1, rwkv_w2, rwkv_a0, rwkv_a1, rwkv_a2,
           rwkv_g1, rwkv_g2, rwkv_k_k, rwkv_k_a, rwkv_r_k, rwkv_gn_g, rwkv_gn_b,
           s5_w_in, s5_lambda_re, s5_lambda_im, s5_log_dt, s5_b_re, s5_b_im, s5_c_re, s5_c_im, s5_d,
           s5_w_glu, s5_w_out, attn_w_qkv, attn_w_o, idx_w_q, idx_w_k, idx_w_w, rel_bias):
    p = dict(ln_g=ln_g, ln_b=ln_b, mlp_w1=mlp_w1, mlp_w2=mlp_w2, rwkv_mu=rwkv_mu, rwkv_w_rkv=rwkv_w_rkv,
             rwkv_w_o=rwkv_w_o, rwkv_w0=rwkv_w0, rwkv_w1=rwkv_w1, rwkv_w2=rwkv_w2, rwkv_a0=rwkv_a0,
             rwkv_a1=rwkv_a1, rwkv_a2=rwkv_a2, rwkv_g1=rwkv_g1, rwkv_g2=rwkv_g2, rwkv_k_k=rwkv_k_k,
             rwkv_k_a=rwkv_k_a, rwkv_r_k=rwkv_r_k, rwkv_gn_g=rwkv_gn_g, rwkv_gn_b=rwkv_gn_b,
             s5_w_in=s5_w_in, s5_lambda_re=s5_lambda_re, s5_lambda_im=s5_lambda_im, s5_log_dt=s5_log_dt,
             s5_b_re=s5_b_re, s5_b_im=s5_b_im, s5_c_re=s5_c_re, s5_c_im=s5_c_im, s5_d=s5_d,
             s5_w_glu=s5_w_glu, s5_w_out=s5_w_out, attn_w_qkv=attn_w_qkv, attn_w_o=attn_w_o,
             idx_w_q=idx_w_q, idx_w_k=idx_w_k, idx_w_w=idx_w_w, rel_bias=rel_bias)
    depth = ada_w.shape[0]
    Bp, Bs = c_prompt.shape[0], c_sample.shape[0]
    c_all = jax.nn.silu(jnp.concatenate([c_prompt, c_sample], axis=0).astype(F32))
    pad = -(-c_all.shape[0] // 16) * 16 - c_all.shape[0]
    c_all = jnp.pad(c_all, ((0, pad), (0, 0)))
    mods_p, mods_s = [], []
    for i in range(depth):
        mod = matmul(c_all, ada_w, (i,)) + ada_b[i]
        six = jnp.split(mod, 6, axis=-1)
        mods_p.append([z[:Bp] for z in six])
        mods_s.append([z[Bp:Bp + Bs] for z in six])
    states = dict(wkv=state_wkv, shift=state_shift, s5_re=state_s5_re, s5_im=state_s5_im)
    caches = (cache_k, cache_v, cache_kidx, page_table)
    out_p = _run(x_prompt, mods_p, False, p, caches, states)
    out_s = _run(x_sample, mods_s, True, p, caches, states)
    return (out_p[0], out_s[0]) + out_p[1:] + out_s[1:]
```

```python
import functools
import math

import jax
import jax.numpy as jnp
import numpy as np
from jax import lax
from jax.experimental import pallas as pl
from jax.experimental.pallas import tpu as pltpu

F32 = jnp.float32
BF16 = jnp.bfloat16

LANES = 128
SUBLANES = 8
VMEM_LIMIT_BYTES = 56 * 1024 * 1024

LN_EPS = 1e-5
GN_EPS = 64e-5
RWKV_HEAD_DIM = 64
S5_GROUP = 16
S5_SUB = 16
ATTN_HEAD_DIM = 128
IDX_DIM = 128
TOPK_MAX = 256
REL_BUCKETS = 32
REL_MAX_DIST = 1024
NEG_BIG = -1e30


def _cparams(sem):
    return pltpu.CompilerParams(dimension_semantics=sem, vmem_limit_bytes=VMEM_LIMIT_BYTES)


def _act(p, act):
    if act is None:
        return p
    if act == "relu2":
        r = jnp.maximum(p, 0.0)
        return r * r
    if act == "tanh":
        return jnp.tanh(p)
    if act == "sigmoid":
        return jax.nn.sigmoid(p)
    raise ValueError(act)


def _mm_kernel(a_ref, w_ref, o_ref, *scratch, nk, act):
    p = jnp.dot(a_ref[...].astype(BF16), w_ref[...].astype(BF16), preferred_element_type=F32)
    if nk == 1:
        o_ref[...] = _act(p, act).astype(o_ref.dtype)
        return
    acc_ref, = scratch
    k = pl.program_id(2)

    @pl.when(k == 0)
    def _():
        acc_ref[...] = p

    @pl.when(k > 0)
    def _():
        acc_ref[...] += p

    @pl.when(k == nk - 1)
    def _():
        o_ref[...] = _act(acc_ref[...], act).astype(o_ref.dtype)


def _pick(n, pref):
    if n <= pref:
        return n
    t = (pref // LANES) * LANES
    while t >= LANES:
        if n % t == 0:
            return t
        t -= LANES
    return n


def matmul(a, w, widx=(), *, act=None, out_dtype=F32, n_off=0, n_size=None, tm=1024, tn=512, tk=4096):
    M, K = a.shape
    Kw, N = w.shape[-2:]
    assert K == Kw and len(widx) == w.ndim - 2
    n_size = N if n_size is None else n_size
    tm = _pick(M, tm) if M % SUBLANES == 0 else M
    tn = _pick(n_size, tn)
    tk = _pick(K, tk)
    assert M % tm == 0 and n_size % tn == 0 and K % tk == 0 and n_off % tn == 0
    nk = K // tk
    joff = n_off // tn
    lead = tuple(widx)
    w_spec = pl.BlockSpec((None,) * len(lead) + (tk, tn), lambda i, j, k: lead + (k, j + joff))
    return pl.pallas_call(
        functools.partial(_mm_kernel, nk=nk, act=act),
        grid=(M // tm, n_size // tn, nk),
        in_specs=[pl.BlockSpec((tm, tk), lambda i, j, k: (i, k)), w_spec],
        out_specs=pl.BlockSpec((tm, tn), lambda i, j, k: (i, j)),
        out_shape=jax.ShapeDtypeStruct((M, n_size), out_dtype),
        scratch_shapes=[pltpu.VMEM((tm, tn), F32)] if nk > 1 else [],
        compiler_params=_cparams(("parallel", "parallel", "arbitrary")),
    )(a, w)


def _allsum_sublanes(p):
    p = p + pltpu.roll(p, 4, 0)
    p = p + pltpu.roll(p, 2, 0)
    return p + pltpu.roll(p, 1, 0)


def _wkv_kernel(r_ref, wl_ref, k_ref, v_ref, al_ref, g_ref, par_ref, s0_ref, z_ref, sT_ref,
                s_ref, op_ref, vt_ref, yt_ref, sc_ref, *, tc, n, nb, nh):
    c = pl.program_id(1)
    n8 = n // SUBLANES
    lw = nb * nh

    @pl.when(c == 0)
    def _():
        s_ref[...] = s0_ref[...]

    w0, a0, k_k, k_a, r_k, gn_g, gn_b = (par_ref[q] for q in range(7))
    cat = lambda ref, t: ref[:, t].reshape(lw, n).T
    fold = lambda x: x.reshape(n8, SUBLANES, lw)
    allsum = lambda x: _allsum_sublanes(jnp.sum(fold(x), axis=0))

    def prep(t, carry):
        r = cat(r_ref, t)
        k = cat(k_ref, t)
        w_log = -jax.nn.softplus(-(w0 + cat(wl_ref, t))) - 0.5
        w = jnp.exp(-jnp.exp(w_log))
        a = jax.nn.sigmoid(a0 + cat(al_ref, t))
        kk = k * k_k
        nrm = jnp.maximum(jnp.sqrt(allsum(kk * kk)), 1e-12)
        kk3 = fold(kk) / nrm[None]
        kh = k * (1.0 + (a - 1.0) * k_a)
        b3 = kk3 * fold(a)
        op_ref[t, 0] = -kk3
        op_ref[t, 1] = fold(w * r)
        op_ref[t, 2] = fold(w)
        op_ref[t, 3] = b3
        op_ref[t, 4] = fold(kh)
        vt_ref[t] = cat(v_ref, t)
        sc_ref[t, 0] = _allsum_sublanes(jnp.sum(b3 * fold(r), axis=0))
        sc_ref[t, 1] = allsum(kh * r)
        sc_ref[t, 2] = allsum(r * kh * r_k)
        return carry

    lax.fori_loop(0, tc, prep, 0, unroll=min(tc, 4))

    def step(t, carry):
        br = sc_ref[t, 0, 0:1]
        kr = sc_ref[t, 1, 0:1]

        def row(i, carry2):
            si = s_ref[i]
            sa = _allsum_sublanes(jnp.sum(si * op_ref[t, 0], axis=0))
            y0 = _allsum_sublanes(jnp.sum(si * op_ref[t, 1], axis=0))
            vi = vt_ref[t, pl.ds(i, 1), :]
            s_ref[i] = si * op_ref[t, 2] + sa[None] * op_ref[t, 3] + vi[None] * op_ref[t, 4]
            yt_ref[t, pl.ds(i, 1), :] = y0[0:1] + sa[0:1] * br + vi * kr
            return carry2

        lax.fori_loop(0, n, row, 0, unroll=16)
        return carry

    lax.fori_loop(0, tc, step, 0)

    def post(t, carry):
        y = yt_ref[t]
        mu = allsum(y) * (1.0 / n)
        yc = fold(y) - mu[None]
        var = _allsum_sublanes(jnp.sum(yc * yc, axis=0)) * (1.0 / n)
        yn = (yc * lax.rsqrt(var + GN_EPS)[None]).reshape(n, lw) * gn_g + gn_b
        z = (yn + (fold(vt_ref[t]) * sc_ref[t, 2][None]).reshape(n, lw)) * cat(g_ref, t)
        zt = z.T
        for q in range(nb):
            z_ref[q, t] = zt[q * nh:(q + 1) * nh].astype(z_ref.dtype)
        return carry

    lax.fori_loop(0, tc, post, 0, unroll=min(tc, 4))

    @pl.when(c == pl.num_programs(1) - 1)
    def _():
        sT_ref[...] = s_ref[...]


def wkv_scan(r, wl, k, v, al, g, par, s0, *, tc, nb):
    B, T, nh, n = r.shape
    n8 = n // SUBLANES
    lw = nb * nh
    assert T % tc == 0 and B % nb == 0
    op = pl.BlockSpec((nb, tc, nh, n), lambda l, c: (l, c, 0, 0))
    st = pl.BlockSpec((n, n8, SUBLANES, lw), lambda l, c: (0, 0, 0, l))
    return pl.pallas_call(
        functools.partial(_wkv_kernel, tc=tc, n=n, nb=nb, nh=nh),
        grid=(B // nb, T // tc),
        in_specs=[op, op, op, op, op, op, pl.BlockSpec((7, n, lw), lambda l, c: (0, 0, 0)), st],
        out_specs=[op, st],
        out_shape=[jax.ShapeDtypeStruct((B, T, nh, n), BF16), jax.ShapeDtypeStruct(s0.shape, F32)],
        scratch_shapes=[pltpu.VMEM((n, n8, SUBLANES, lw), F32), pltpu.VMEM((tc, 5, n8, SUBLANES, lw), F32),
                        pltpu.VMEM((tc, n, lw), F32), pltpu.VMEM((tc, n, lw), F32),
                        pltpu.VMEM((tc, 3, SUBLANES, lw), F32)],
        compiler_params=_cparams(("parallel", "arbitrary")),
    )(r, wl, k, v, al, g, par, s0)


_HI = lax.Precision.HIGHEST


def _s5_fold(lam_re, lam_im, log_dt, b_re, b_im, c_re, c_im, C):
    G, P, I = b_re.shape
    dt = jnp.exp(log_dt.astype(F32))[:, None]
    lr, li = lam_re.astype(F32), lam_im.astype(F32)
    mag = jnp.exp(lr * dt)
    abar_re, abar_im = mag * jnp.cos(li * dt), mag * jnp.sin(li * dt)
    den = lr * lr + li * li
    nr, ni = abar_re - 1.0, abar_im
    coef_re = (nr * lr + ni * li) / den
    coef_im = (ni * lr - nr * li) / den
    br_, bi_ = b_re.astype(F32), b_im.astype(F32)
    bb_re = coef_re[..., None] * br_ - coef_im[..., None] * bi_
    bb_im = coef_re[..., None] * bi_ + coef_im[..., None] * br_
    cr, ci = c_re.astype(F32), c_im.astype(F32)
    pw_re, pw_im = [jnp.ones_like(abar_re)], [jnp.zeros_like(abar_im)]
    for _ in range(C):
        pr, pi = pw_re[-1], pw_im[-1]
        pw_re.append(pr * abar_re - pi * abar_im)
        pw_im.append(pr * abar_im + pi * abar_re)
    a_re, a_im = jnp.stack(pw_re, 1), jnp.stack(pw_im, 1)
    ab_re = a_re[:, :C, :, None] * bb_re[:, None] - a_im[:, :C, :, None] * bb_im[:, None]
    ab_im = a_re[:, :C, :, None] * bb_im[:, None] + a_im[:, :C, :, None] * bb_re[:, None]
    kern = (jnp.einsum("gjp,gtpi->gtij", cr, ab_re, precision=_HI)
            - jnp.einsum("gjp,gtpi->gtij", ci, ab_im, precision=_HI))
    ar1, ai1 = a_re[:, 1:], a_im[:, 1:]
    v_re = cr[:, None] * ar1[:, :, None, :] - ci[:, None] * ai1[:, :, None, :]
    v_im = -cr[:, None] * ai1[:, :, None, :] - ci[:, None] * ar1[:, :, None, :]
    v_re = jnp.transpose(v_re, (0, 3, 1, 2)).reshape(G, P, C * I)
    v_im = jnp.transpose(v_im, (0, 3, 1, 2)).reshape(G, P, C * I)
    return kern, ab_re, ab_im, v_re, v_im, a_re[:, C][:, None], a_im[:, C][:, None]


def _s5_pergroup(folded):
    kern, ab_re, ab_im, v_re, v_im, a_re, a_im = folded
    G, C, P, I = ab_re.shape
    tt = np.arange(C)[None, :] - np.arange(C)[:, None]
    m = kern[:, np.clip(tt, 0, C - 1)]
    m = jnp.where((tt >= 0)[None, :, :, None, None], m, 0.0)
    m = jnp.transpose(m, (0, 1, 3, 2, 4)).reshape(G, C * I, C * I)
    rev = np.arange(C - 1, -1, -1)
    w_re = jnp.transpose(ab_re[:, rev], (0, 1, 3, 2)).reshape(G, C * I, P)
    w_im = jnp.transpose(ab_im[:, rev], (0, 1, 3, 2)).reshape(G, C * I, P)
    return m, w_re, w_im, v_re, v_im, a_re, a_im


def _s5_kernel(u_ref, m_ref, wre_ref, wim_ref, vre_ref, vim_ref, are_ref, aim_ref, x0re_ref, x0im_ref,
               y_ref, xre_ref, xim_ref, bur_ref, bui_ref, *, gb, nc, bp):
    for g in range(gb):
        u = u_ref[g]
        bur_ref[g] = jnp.dot(u, wre_ref[g], precision=_HI, preferred_element_type=F32)
        bui_ref[g] = jnp.dot(u, wim_ref[g], precision=_HI, preferred_element_type=F32)
    ar = are_ref[...]
    ai = aim_ref[...]

    def chunk(c, carry):
        xr, xi = carry
        rows = pl.ds(pl.multiple_of(c * bp, bp), bp)
        br = bur_ref[:, rows, :]
        bi = bui_ref[:, rows, :]
        bur_ref[:, rows, :] = xr
        bui_ref[:, rows, :] = xi
        return ar * xr - ai * xi + br, ar * xi + ai * xr + bi

    xr, xi = lax.fori_loop(0, nc, chunk, (x0re_ref[...], x0im_ref[...]))
    xre_ref[...] = xr
    xim_ref[...] = xi
    for g in range(gb):
        y = jnp.dot(u_ref[g], m_ref[g], precision=_HI, preferred_element_type=F32)
        y += jnp.dot(bur_ref[g], vre_ref[g], precision=_HI, preferred_element_type=F32)
        y += jnp.dot(bui_ref[g], vim_ref[g], precision=_HI, preferred_element_type=F32)
        y_ref[g] = y


def s5_scan(u, folded, x0_re, x0_im, *, gb):
    m, w_re, w_im, v_re, v_im, a_re, a_im = folded
    G, R, CI = u.shape
    P = w_re.shape[-1]
    bp = x0_re.shape[1]
    nc = R // bp
    assert G % gb == 0 and bp % SUBLANES == 0
    blk = lambda *s: pl.BlockSpec((gb,) + s, lambda g: (g,) + (0,) * len(s))
    return pl.pallas_call(
        functools.partial(_s5_kernel, gb=gb, nc=nc, bp=bp),
        grid=(G // gb,),
        in_specs=[blk(R, CI), blk(CI, CI), blk(CI, P), blk(CI, P), blk(P, CI), blk(P, CI),
                  blk(1, P), blk(1, P), blk(bp, P), blk(bp, P)],
        out_specs=[blk(R, CI), blk(bp, P), blk(bp, P)],
        out_shape=[jax.ShapeDtypeStruct((G, R, CI), F32), jax.ShapeDtypeStruct((G, bp, P), F32),
                   jax.ShapeDtypeStruct((G, bp, P), F32)],
        scratch_shapes=[pltpu.VMEM((gb, R, P), F32), pltpu.VMEM((gb, R, P), F32)],
        compiler_params=_cparams(("parallel",)),
    )(u, m, w_re, w_im, v_re, v_im, a_re, a_im, x0_re, x0_im)


def _s5_blockdiag(folded, gl):
    kern, ab_re, ab_im, v_re, v_im, a_re, a_im = folded
    G, C, P, I = ab_re.shape
    NG = G // gl
    kern2 = jnp.transpose(kern.reshape(NG, gl, C, I, I), (0, 1, 3, 2, 4)).reshape(NG, gl * I, C * I)
    rev = np.arange(C - 1, -1, -1)
    ab = jnp.stack([ab_re[:, rev], ab_im[:, rev]], axis=2)
    w2 = jnp.transpose(ab.reshape(NG, gl, C, 2, P, I), (0, 1, 5, 2, 3, 4)).reshape(NG, gl * I, C * 2 * P)
    v2 = jnp.stack([v_re, v_im], axis=0).reshape(2, NG, gl * P, C * I)
    v2 = jnp.transpose(v2, (1, 0, 2, 3)).reshape(NG, 2 * gl * P, C * I)
    r_tau, r_j = np.divmod(np.arange(C * I), I)
    cu, cr = np.divmod(np.arange((2 * C - 1) * gl * I), gl * I)
    sebig = ((cu[None, :] - (C - 1) == r_tau[:, None]) & ((cr % I)[None, :] == r_j[:, None]))
    rc = np.arange(2 * P)
    cc, cp = np.divmod(np.arange(2 * gl * P), gl * P)
    e2 = (cc[None, :] == (rc // P)[:, None]) & ((cp % P)[None, :] == (rc % P)[:, None])
    lanes = lambda a: a.reshape(NG, 1, gl * P)
    return (kern2, w2, v2, jnp.asarray(sebig, BF16), jnp.asarray(e2, BF16), lanes(a_re), lanes(a_im))


def _s5n_kernel(u_ref, d_ref, k2_ref, w2_ref, v2_ref, se_ref, e2_ref, are_ref, aim_ref, x0_ref, z_ref, xT_ref,
                m_ref, w_ref, v_ref, xcat_ref, bu_ref, xprev_ref, y_ref, *, B, nc, C, nsplit, gp, I, P):
    R = B * nc
    tsub = C // nsplit
    nq = 2 * gp // LANES
    K = C * LANES

    grp = lambda shape, axis, width: lax.broadcasted_iota(jnp.int32, shape, axis) // width
    same_m = grp((LANES, K), 0, I) == (lax.broadcasted_iota(jnp.int32, (LANES, K), 1) % LANES) // I
    same_w = grp((LANES, 2 * gp), 0, I) == (lax.broadcasted_iota(jnp.int32, (LANES, 2 * gp), 1) % gp) // P
    k2 = k2_ref[...].astype(BF16)
    for s in range(C):
        sel = se_ref[:, (C - 1 - s) * LANES:(C - 1 - s) * LANES + K]
        ms = jnp.dot(k2, sel, preferred_element_type=F32)
        m_ref[s * LANES:(s + 1) * LANES, :] = jnp.where(same_m, ms, 0.0).astype(BF16)
        ws = jnp.dot(w2_ref[:, s * 2 * P:(s + 1) * 2 * P].astype(BF16), e2_ref[...], preferred_element_type=F32)
        w_ref[s * LANES:(s + 1) * LANES, :] = jnp.where(same_w, ws, 0.0).astype(BF16)
    same_v = ((lax.broadcasted_iota(jnp.int32, (2 * gp, K), 0) % gp) // P
              == (lax.broadcasted_iota(jnp.int32, (2 * gp, K), 1) % LANES) // I)
    vs = jnp.dot(v2_ref[...].astype(BF16), se_ref[:, (C - 1) * LANES:(C - 1) * LANES + K], preferred_element_type=F32)
    v_ref[...] = jnp.where(same_v, vs, 0.0).astype(BF16)

    for s in range(C):
        xcat_ref[:, s * LANES:(s + 1) * LANES] = u_ref[pl.ds(s, R, stride=C), :].astype(BF16)
    bu = jnp.dot(xcat_ref[...], w_ref[...], preferred_element_type=F32)
    for q in range(nq):
        bu_ref[q] = bu[:, q * LANES:(q + 1) * LANES]
    ar = are_ref[...]
    ai = aim_ref[...]

    def block(c, carry):
        xr, xi = carry
        rows = pl.ds(c, B, stride=nc)
        x = jnp.concatenate([xr, xi], axis=-1)
        for q in range(nq):
            xprev_ref.at[q][rows, :] = x[:, q * LANES:(q + 1) * LANES]
        bu_c = jnp.concatenate([bu_ref.at[q][rows, :] for q in range(nq)], axis=-1)
        return ar * xr - ai * xi + bu_c[:, :gp], ar * xi + ai * xr + bu_c[:, gp:]

    x0 = x0_ref[...]
    xr, xi = lax.fori_loop(0, nc, block, (x0[:, :gp], x0[:, gp:]))
    xT_ref[...] = jnp.concatenate([xr, xi], axis=-1)

    xprev = jnp.concatenate([xprev_ref[q] for q in range(nq)], axis=-1).astype(BF16)
    for hf in range(nsplit):
        cols = slice(hf * tsub * LANES, (hf + 1) * tsub * LANES)
        y = jnp.dot(xcat_ref[...], m_ref[:, cols], preferred_element_type=F32)
        y = y + jnp.dot(xprev, v_ref[:, cols], preferred_element_type=F32)
        for tt in range(tsub):
            y_ref[pl.ds(hf * tsub + tt, R, stride=C), :] = y[:, tt * LANES:(tt + 1) * LANES]
    val = y_ref[...] + d_ref[...] * u_ref[...]
    z_ref[...] = jax.nn.gelu(val).astype(z_ref.dtype)


def s5_natural(u, d, bd, x0, *, B, C, nsplit=2):
    kern2, w2, v2, sebig, e2, a_re, a_im = bd
    M, D = u.shape
    NG, gi, ci = kern2.shape
    I = ci // C
    P = e2.shape[0] // 2
    K = C * LANES
    gp = a_re.shape[-1]
    nc = M // (B * C)
    R = B * nc
    assert D == NG * LANES and gi == LANES and 2 * P == LANES and C % nsplit == 0
    per_g = lambda *s: pl.BlockSpec((None,) + s, lambda n: (n,) + (0,) * len(s))
    const = lambda a: pl.BlockSpec(a.shape, lambda n: (0,) * a.ndim)
    return pl.pallas_call(
        functools.partial(_s5n_kernel, B=B, nc=nc, C=C, nsplit=nsplit, gp=gp, I=I, P=P),
        grid=(NG,),
        in_specs=[pl.BlockSpec((M, LANES), lambda n: (0, n)),
                  pl.BlockSpec((1, LANES), lambda n: (0, n)),
                  per_g(gi, ci), per_g(gi, C * 2 * P), per_g(2 * gp, ci), const(sebig), const(e2),
                  per_g(1, gp), per_g(1, gp), per_g(B, 2 * gp)],
        out_specs=[pl.BlockSpec((M, LANES), lambda n: (0, n)), per_g(B, 2 * gp)],
        out_shape=[jax.ShapeDtypeStruct((M, D), BF16), jax.ShapeDtypeStruct((NG, B, 2 * gp), F32)],
        scratch_shapes=[pltpu.VMEM((K, K), BF16), pltpu.VMEM((K, 2 * gp), BF16), pltpu.VMEM((2 * gp, K), BF16),
                        pltpu.VMEM((R, K), BF16), pltpu.VMEM((2 * gp // LANES, R, LANES), F32),
                        pltpu.VMEM((2 * gp // LANES, R, LANES), F32),
                        pltpu.VMEM((M, LANES), F32)],
        compiler_params=_cparams(("parallel",)),
    )(u, d, kern2, w2, v2, sebig, e2, a_re, a_im, x0)


def _ln_kernel(x_ref, f_ref, gate_ref, g_ref, b_ref, sc_ref, sh_ref, xo_ref, ho_ref, *, alpha):
    y = alpha * x_ref[...] + gate_ref[...] * f_ref[...]
    mu = jnp.mean(y, axis=-1, keepdims=True)
    yc = y - mu
    var = jnp.mean(yc * yc, axis=-1, keepdims=True)
    xn = yc * lax.rsqrt(var + LN_EPS) * g_ref[...] + b_ref[...]
    xo_ref[...] = xn
    ho_ref[...] = (xn * sc_ref[...] + sh_ref[...]).astype(ho_ref.dtype)


def residual_ln(x, f, gate, g, b, scale, shift, *, rows_per_batch, alpha, tm=256):
    M, D = x.shape
    B = gate.shape[0]
    if rows_per_batch == 1:
        tm = M
        mod = pl.BlockSpec((tm, D), lambda i: (i, 0))
        mods = (gate, scale, shift)
    else:
        tm = min(tm, rows_per_batch)
        assert rows_per_batch % tm == 0
        per = rows_per_batch // tm
        mod = pl.BlockSpec((None, 1, D), lambda i: (i // per, 0, 0))
        mods = tuple(z.reshape(B, 1, D) for z in (gate, scale, shift))
    row = pl.BlockSpec((tm, D), lambda i: (i, 0))
    vec = pl.BlockSpec((1, D), lambda i: (0, 0))
    return pl.pallas_call(
        functools.partial(_ln_kernel, alpha=alpha),
        grid=(M // tm,),
        in_specs=[row, row, mod, vec, vec, mod, mod],
        out_specs=[row, row],
        out_shape=[jax.ShapeDtypeStruct((M, D), F32), jax.ShapeDtypeStruct((M, D), BF16)],
        compiler_params=_cparams(("parallel",)),
    )(x, f, mods[0], g.reshape(1, D), b.reshape(1, D), mods[1], mods[2])


INT_MIN = -(2 ** 31)


def _order_key(x):
    bits = pltpu.bitcast(x, jnp.int32)
    return bits ^ ((bits >> 31) & 0x7FFFFFFF)


def _topk_member(key_ref, k_top):
    rows, L = key_ref.shape
    kf = jnp.float32(k_top)

    def count(pred):
        return jnp.sum(pred.astype(F32), axis=-1, keepdims=True)

    def bit_step(it, res):
        cand = res | (jnp.int32(1) << (31 - it))
        cnt = count(key_ref[...] >= (cand ^ INT_MIN))
        return jnp.where(cnt >= kf, cand, res)

    res = lax.fori_loop(0, 32, bit_step, jnp.zeros((rows, 1), jnp.int32))
    thr = res ^ INT_MIN
    key = key_ref[...]
    gt = key > thr
    eq = key == thr
    need = kf - count(gt)
    pos = lax.broadcasted_iota(jnp.int32, (rows, L), 1)
    nbits = max(1, (L - 1).bit_length())

    def pos_step(it, lim):
        cand = lim | (jnp.int32(1) << (nbits - 1 - it))
        cnt = count((key_ref[...] == thr) & (pos < cand))
        return jnp.where(cnt < need, cand, lim)

    lim = lax.fori_loop(0, nbits, pos_step, jnp.zeros((rows, 1), jnp.int32))
    return gt | (eq & (pos <= lim))


def _idx_kernel(q_ref, k_ref, w_ref, mask_ref, acc_ref, key_ref, *, tq, n_heads, k_top):
    i = pl.program_id(1)
    S = k_ref.shape[0]

    def tile(ii):
        se = (ii + 1) * tq
        kb = k_ref[:se, :].astype(BF16)
        acc = acc_ref.at[:, :se]
        keys = key_ref.at[:, :se]
        acc[...] = jnp.zeros((tq, se), F32)

        def weighted(h):
            q = q_ref[:, pl.ds(pl.multiple_of(h * IDX_DIM, IDX_DIM), IDX_DIM)]
            s = lax.dot_general(q, kb, (((1,), (1,)), ((), ())), preferred_element_type=F32)
            wcol = pltpu.roll(w_ref[...], LANES - h, 1)[:, :1]
            return wcol * jnp.maximum(s, 0.0)

        def heads(hh, carry):
            part = weighted(hh * hpp)
            for e in range(1, hpp):
                part = part + weighted(hh * hpp + e)
            acc[...] += part
            return carry

        hpp = 2 if n_heads % 2 == 0 else 1
        lax.fori_loop(0, n_heads // hpp, heads, 0)
        t_pos = ii * tq + lax.broadcasted_iota(jnp.int32, (tq, se), 0)
        s_pos = lax.broadcasted_iota(jnp.int32, (tq, se), 1)
        causal = s_pos <= t_pos
        keys[...] = _order_key(jnp.where(causal, acc[...], -jnp.inf))
        member = _topk_member(keys, k_top)
        mask_ref[:, :se] = jnp.where(member & causal, 0.0, NEG_BIG)
        if se < S:
            mask_ref[:, se:] = jnp.full((tq, S - se), NEG_BIG, F32)

    for ii in range(S // tq):
        pl.when(i == ii)(functools.partial(tile, ii))


def dsa_select_prompt(q_idx, k_idx, w_idx, *, B, T, k_top, tq=256):
    n_heads = q_idx.shape[1] // IDX_DIM
    tq = min(tq, T)
    nq = T // tq
    return pl.pallas_call(
        functools.partial(_idx_kernel, tq=tq, n_heads=n_heads, k_top=k_top),
        grid=(B, nq),
        in_specs=[pl.BlockSpec((tq, q_idx.shape[1]), lambda b, i: (b * nq + i, 0)),
                  pl.BlockSpec((T, IDX_DIM), lambda b, i: (b, 0)),
                  pl.BlockSpec((tq, LANES), lambda b, i: (b * nq + i, 0))],
        out_specs=pl.BlockSpec((tq, T), lambda b, i: (b * nq + i, 0)),
        out_shape=jax.ShapeDtypeStruct((B * T, T), F32),
        scratch_shapes=[pltpu.VMEM((tq, T), F32), pltpu.VMEM((tq, T), jnp.int32)],
        compiler_params=_cparams(("parallel", "parallel")),
    )(q_idx, k_idx, w_idx)


def _bucket_thresholds(max_dist):
    exact = REL_BUCKETS // 2
    d = np.arange(max_dist + 1)
    large = exact + np.floor(np.log(np.maximum(d, 1) / exact) / math.log(REL_MAX_DIST / exact)
                             * (REL_BUCKETS - exact) + 1e-9).astype(np.int64)
    bucket = np.where(d < exact, d, np.minimum(large, REL_BUCKETS - 1))
    return [int(np.argmax(bucket >= k)) if (bucket >= k).any() else max_dist + 1 for k in range(1, REL_BUCKETS)]


def rel_bucket_of(dist, max_dist):
    out = jnp.zeros(dist.shape, jnp.int32)
    for thr in _bucket_thresholds(max_dist):
        out = out + (dist >= thr).astype(jnp.int32)
    return out


def _attn_kernel(q_ref, k_ref, v_ref, mask_ref, bkt_ref, tab_ref, o_ref, *, scale):
    tq = q_ref.shape[0]
    S = k_ref.shape[0]
    i = pl.program_id(1)

    def tile(ii):
        se = (ii + 1) * tq
        q = q_ref[...].astype(BF16)
        kb = k_ref[:se, :].astype(BF16)
        logits = lax.dot_general(q, kb, (((1,), (1,)), ((), ())), preferred_element_type=F32) * scale
        tab = jnp.broadcast_to(tab_ref[...], (tq, LANES))
        bias = jnp.concatenate(
            [jnp.take_along_axis(tab, bkt_ref[:, c * LANES:(c + 1) * LANES], axis=1) for c in range(se // LANES)],
            axis=1)
        logits = logits + bias + mask_ref[:, :se]
        m = jnp.max(logits, axis=-1, keepdims=True)
        p = jnp.exp(logits - m)
        l = jnp.sum(p, axis=-1, keepdims=True)
        o = jnp.dot(p.astype(BF16), v_ref[:se, :].astype(BF16), preferred_element_type=F32)
        o_ref[...] = (o / l).astype(o_ref.dtype)

    for ii in range(S // tq):
        pl.when(i == ii)(functools.partial(tile, ii))


def dsa_attend_prompt(q, k, v, mask, bucket, bias_tab, *, B, T, tq=256):
    H = q.shape[1] // ATTN_HEAD_DIM
    tq = min(tq, T)
    nq = T // tq
    return pl.pallas_call(
        functools.partial(_attn_kernel, scale=ATTN_HEAD_DIM ** -0.5),
        grid=(B, nq, H),
        in_specs=[pl.BlockSpec((tq, ATTN_HEAD_DIM), lambda b, i, h: (b * nq + i, h)),
                  pl.BlockSpec((T, ATTN_HEAD_DIM), lambda b, i, h: (b, h)),
                  pl.BlockSpec((T, ATTN_HEAD_DIM), lambda b, i, h: (b, h)),
                  pl.BlockSpec((tq, T), lambda b, i, h: (b * nq + i, 0)),
                  pl.BlockSpec((tq, T), lambda b, i, h: (i, 0)),
                  pl.BlockSpec((None, 1, LANES), lambda b, i, h: (h, 0, 0))],
        out_specs=pl.BlockSpec((tq, ATTN_HEAD_DIM), lambda b, i, h: (b * nq + i, h)),
        out_shape=jax.ShapeDtypeStruct(q.shape, BF16),
        compiler_params=_cparams(("parallel", "parallel", "arbitrary")),
    )(q, k, v, mask, bucket, bias_tab)


def _page_score_kernel(pt_ref, q_ref, w_ref, kidx_ref, sc_ref):
    kb = kidx_ref[...].astype(BF16)
    s = lax.dot_general(q_ref[...], kb, (((1,), (1,)), ((), ())), preferred_element_type=F32)
    sc_ref[...] = jnp.sum(w_ref[...] * jnp.maximum(s, 0.0), axis=0, keepdims=True)


def dsa_page_scores(q_idx, w_idx, cache_kidx, page_table, j):
    B, HI, _ = q_idx.shape
    page = cache_kidx.shape[2]
    n_pages = page_table.shape[1]
    grid_spec = pltpu.PrefetchScalarGridSpec(
        num_scalar_prefetch=1,
        grid=(B, n_pages),
        in_specs=[pl.BlockSpec((None, HI, IDX_DIM), lambda b, p, pt: (b, 0, 0)),
                  pl.BlockSpec((None, HI, 1), lambda b, p, pt: (b, 0, 0)),
                  pl.BlockSpec((None, None, page, IDX_DIM), lambda b, p, pt: (j, pt[b, p], 0, 0))],
        out_specs=pl.BlockSpec((None, 1, page), lambda b, p, pt: (b, 0, p)),
    )
    return pl.pallas_call(
        _page_score_kernel,
        grid_spec=grid_spec,
        out_shape=jax.ShapeDtypeStruct((B, 1, n_pages * page), F32),
        compiler_params=_cparams(("parallel", "arbitrary")),
    )(page_table, q_idx, w_idx, cache_kidx)


def _sample_select_kernel(sc_ref, q_ref, w_ref, knew_ref, mask_ref, key_ref, *, past, k_top):
    B = sc_ref.shape[0]
    key_ref[:, :past] = _order_key(sc_ref[...])
    lane = lax.broadcasted_iota(jnp.int32, (1, LANES), 1)
    for b in range(B):
        kn = knew_ref[b].astype(BF16).astype(F32)
        s = jnp.sum(q_ref[b].astype(F32) * kn, axis=-1, keepdims=True)
        sc_new = jnp.sum(w_ref[b] * jnp.maximum(s, 0.0), axis=0, keepdims=True)
        tail = jnp.where(lane == 0, sc_new, -jnp.inf)
        key_ref[b:b + 1, past:] = _order_key(tail)
    member = _topk_member(key_ref, k_top)
    mask_ref[...] = jnp.where(member, 0.0, NEG_BIG)


def dsa_select_sample(scores, q_idx, w_idx, k_idx_new, *, k_top):
    B, past = scores.shape
    L = past + LANES
    return pl.pallas_call(
        functools.partial(_sample_select_kernel, past=past, k_top=k_top),
        out_shape=jax.ShapeDtypeStruct((B, L), F32),
        scratch_shapes=[pltpu.VMEM((B, L), jnp.int32)],
        compiler_params=pltpu.CompilerParams(vmem_limit_bytes=VMEM_LIMIT_BYTES),
    )(scores, q_idx, w_idx, k_idx_new)


def _page_attn_kernel(pt_ref, q_ref, k_ref, v_ref, bias_ref, mask_ref, knew_ref, vnew_ref, bnew_ref, mnew_ref,
                      o_ref, m_ref, l_ref, acc_ref, *, scale, H):
    p_idx = pl.program_id(1)

    @pl.when(p_idx == 0)
    def _():
        m_ref[...] = jnp.full_like(m_ref, NEG_BIG)
        l_ref[...] = jnp.zeros_like(l_ref)
        acc_ref[...] = jnp.zeros_like(acc_ref)

    q = q_ref[...].astype(BF16)
    cols = k_ref.shape[0]
    la = lax.dot_general(q, k_ref[...].astype(BF16), (((1,), (1,)), ((), ())), preferred_element_type=F32)
    la = la * scale + bias_ref[...]
    row_h = lax.broadcasted_iota(jnp.int32, (H, cols), 0)
    col_h = lax.broadcasted_iota(jnp.int32, (H, cols), 1) % H
    valid = (row_h == col_h) & (mask_ref[...] == 0.0)
    la = jnp.where(valid, la, NEG_BIG)
    m_old = m_ref[...]
    m_new = jnp.maximum(m_old, jnp.max(la, axis=-1, keepdims=True))
    alpha = jnp.exp(m_old - m_new)
    p = jnp.where(valid, jnp.exp(la - m_new), 0.0)
    l_ref[...] = alpha * l_ref[...] + jnp.sum(p, axis=-1, keepdims=True)
    acc_ref[...] = alpha * acc_ref[...] + jnp.dot(p.astype(BF16), v_ref[...].astype(BF16), preferred_element_type=F32)
    m_ref[...] = m_new

    @pl.when(p_idx == pl.num_programs(1) - 1)
    def _():
        kn = knew_ref[...].astype(BF16).astype(F32)
        s_new = jnp.sum(q.astype(F32) * kn, axis=-1, keepdims=True) * scale + bnew_ref[...]
        ok = mnew_ref[...][:, :1] == 0.0
        s_new = jnp.where(ok, s_new, NEG_BIG)
        m_o = m_ref[...]
        m_f = jnp.maximum(m_o, s_new)
        a_f = jnp.exp(m_o - m_f)
        p_new = jnp.where(ok, jnp.exp(s_new - m_f), 0.0)
        l_f = a_f * l_ref[...] + p_new
        acc = a_f * acc_ref[...] + p_new.astype(BF16).astype(F32) * vnew_ref[...].astype(BF16).astype(F32)
        o_ref[...] = acc / l_f


def dsa_attend_sample(q, cache_k, cache_v, page_table, j, bias_rows, mask_rows, k_new, v_new, bias_new, mask_new):
    B, H, Dh = q.shape
    cols = cache_k.shape[2]
    n_pages = page_table.shape[1]
    per_b = lambda *s: pl.BlockSpec((None,) + s, lambda b, p, pt: (b,) + (0,) * len(s))
    grid_spec = pltpu.PrefetchScalarGridSpec(
        num_scalar_prefetch=1,
        grid=(B, n_pages),
        in_specs=[per_b(H, Dh),
                  pl.BlockSpec((None, None, cols, Dh), lambda b, p, pt: (j, pt[b, p], 0, 0)),
                  pl.BlockSpec((None, None, cols, Dh), lambda b, p, pt: (j, pt[b, p], 0, 0)),
                  pl.BlockSpec((None, 1, cols), lambda b, p, pt: (p, 0, 0)),
                  pl.BlockSpec((None, None, 1, cols), lambda b, p, pt: (b, p, 0, 0)),
                  per_b(H, Dh), per_b(H, Dh),
                  pl.BlockSpec((H, 1), lambda b, p, pt: (0, 0)),
                  per_b(1, LANES)],
        out_specs=per_b(H, Dh),
        scratch_shapes=[pltpu.VMEM((H, 1), F32), pltpu.VMEM((H, 1), F32), pltpu.VMEM((H, Dh), F32)],
    )
    return pl.pallas_call(
        functools.partial(_page_attn_kernel, scale=Dh ** -0.5, H=H),
        grid_spec=grid_spec,
        out_shape=jax.ShapeDtypeStruct((B, H, Dh), F32),
        compiler_params=_cparams(("parallel", "arbitrary")),
    )(page_table, q, cache_k, cache_v, bias_rows, mask_rows, k_new, v_new, bias_new, mask_new)


def _rwkv_layer(h, B, T, shift0, wkv0, j, p):
    M, D = h.shape
    N = RWKV_HEAD_DIM
    H = D // N
    h3 = h.reshape(B, T, D)
    prev = jnp.concatenate([shift0[:, None].astype(F32), h3[:, :-1]], axis=1)
    d = prev - h3
    mu = p["rwkv_mu"][j]
    mix = [(h3 + d * mu[i]).astype(BF16).reshape(M, D) for i in range(6)]
    seq = lambda z: z.reshape(B, T, H, N)
    r = matmul(mix[0], p["rwkv_w_rkv"], (j, 0))
    k = matmul(mix[2], p["rwkv_w_rkv"], (j, 1))
    v = matmul(mix[3], p["rwkv_w_rkv"], (j, 2))
    wl = matmul(matmul(mix[1], p["rwkv_w1"], (j,), act="tanh", out_dtype=BF16), p["rwkv_w2"], (j,))
    al = matmul(matmul(mix[4], p["rwkv_a1"], (j,), out_dtype=BF16), p["rwkv_a2"], (j,))
    g = matmul(matmul(mix[5], p["rwkv_g1"], (j,), act="sigmoid", out_dtype=BF16), p["rwkv_g2"], (j,))
    nb = min(B, max(1, LANES // H))
    vec = lambda z: jnp.tile(z.astype(F32).reshape(H, N).T, (1, nb))
    par = jnp.stack([vec(p[q][j]) for q in ("rwkv_w0", "rwkv_a0", "rwkv_k_k", "rwkv_k_a", "rwkv_r_k",
                                            "rwkv_gn_g", "rwkv_gn_b")])
    if wkv0 is None:
        s0 = jnp.zeros((N, N // SUBLANES, SUBLANES, B * H), F32)
    else:
        s0 = jnp.transpose(wkv0.astype(F32), (2, 3, 0, 1)).reshape(N, N // SUBLANES, SUBLANES, B * H)
    z, s_last = wkv_scan(seq(r), seq(wl), seq(k), seq(v), seq(al), seq(g), par, s0, tc=math.gcd(T, 16), nb=nb)
    wkv_new = jnp.transpose(s_last.reshape(N, N, B, H), (2, 3, 0, 1))
    out = matmul(z.reshape(M, D), p["rwkv_w_o"], (j,))
    return out, h3[:, -1], wkv_new


def _s5_layer(hb, B, T, x0_re, x0_im, j, p):
    M, D = hb.shape
    I = S5_GROUP
    G = D // I
    P = p["s5_lambda_re"].shape[-1]
    C = S5_SUB if T % S5_SUB == 0 else 1
    nc = T // C
    bp = -(-B // SUBLANES) * SUBLANES
    u = matmul(hb, p["s5_w_in"], (j,))
    folded = _s5_fold(p["s5_lambda_re"][j], p["s5_lambda_im"][j], p["s5_log_dt"][j], p["s5_b_re"][j],
                      p["s5_b_im"][j], p["s5_c_re"][j], p["s5_c_im"][j], C)
    gl = LANES // I
    if C > 1 and G % gl == 0:
        NG = G // gl
        if x0_re is None:
            x0 = jnp.zeros((NG, B, 2 * gl * P), F32)
        else:
            st = lambda z: jnp.transpose(z.astype(F32).reshape(B, NG, gl * P), (1, 0, 2))
            x0 = jnp.concatenate([st(x0_re), st(x0_im)], axis=-1)
        z, xT = s5_natural(u, p["s5_d"][j].astype(F32).reshape(1, D), _s5_blockdiag(folded, gl), x0, B=B, C=C)
        fin = lambda q: jnp.transpose(q.reshape(NG, B, gl, P), (1, 0, 2, 3)).reshape(B, G, P)
        hr, hi = fin(xT[..., :gl * P]), fin(xT[..., gl * P:])
    else:
        ug = jnp.transpose(u.reshape(B, nc, C, G, I), (3, 1, 0, 2, 4))
        ug = jnp.pad(ug, ((0, 0), (0, 0), (0, bp - B), (0, 0), (0, 0))).reshape(G, nc * bp, C * I)
        if x0_re is None:
            x0r = x0i = jnp.zeros((G, bp, P), F32)
        else:
            st = lambda z: jnp.pad(jnp.transpose(z.astype(F32), (1, 0, 2)), ((0, 0), (0, bp - B), (0, 0)))
            x0r, x0i = st(x0_re), st(x0_im)
        yg, xr, xi = s5_scan(ug, _s5_pergroup(folded), x0r, x0i, gb=8)
        y = jnp.transpose(yg.reshape(G, nc, bp, C, I)[:, :, :B], (2, 1, 3, 0, 4)).reshape(M, D)
        z = jax.nn.gelu(y + p["s5_d"][j].astype(F32) * u).astype(BF16)
        fin = lambda q: jnp.transpose(q[:, :B], (1, 0, 2))
        hr, hi = fin(xr), fin(xi)
    ab = matmul(z, p["s5_w_glu"], (j,))
    glu = (ab[:, :D] * jax.nn.sigmoid(ab[:, D:])).astype(BF16)
    out = matmul(glu, p["s5_w_out"], (j,))
    return out, hr, hi


def _dsa_project(hb, j, p):
    D = hb.shape[1]
    q = matmul(hb, p["attn_w_qkv"], (j,), n_off=0, n_size=D)
    k = matmul(hb, p["attn_w_qkv"], (j,), n_off=D, n_size=D)
    v = matmul(hb, p["attn_w_qkv"], (j,), n_off=2 * D, n_size=D)
    q_idx = matmul(hb, p["idx_w_q"], (j,), out_dtype=BF16)
    k_idx = matmul(hb, p["idx_w_k"], (j,))
    n_ih = p["idx_w_w"].shape[-1]
    w_idx = matmul(hb, p["idx_w_w"], (j,)) * ((n_ih * IDX_DIM) ** -0.5)
    return q, k, v, q_idx, k_idx, w_idx


def _dsa_prompt_layer(hb, B, T, j, p):
    M, D = hb.shape
    H = D // ATTN_HEAD_DIM
    q, k, v, q_idx, k_idx, w_idx = _dsa_project(hb, j, p)
    k_top = max(1, min(TOPK_MAX, T // 4))
    w_pad = jnp.pad(w_idx, ((0, 0), (0, LANES - w_idx.shape[1])))
    mask = dsa_select_prompt(q_idx, k_idx, w_pad, B=B, T=T, k_top=k_top)
    pos = jnp.arange(T, dtype=jnp.int32)
    bucket = rel_bucket_of(pos[:, None] - pos[None, :], T)
    tab = jnp.pad(p["rel_bias"].astype(F32).T, ((0, 0), (0, LANES - REL_BUCKETS))).reshape(H, 1, LANES)
    o = dsa_attend_prompt(q, k, v, mask, bucket, tab, B=B, T=T)
    out = matmul(o, p["attn_w_o"], (j,))
    return out, k.reshape(B, T, H, ATTN_HEAD_DIM), v.reshape(B, T, H, ATTN_HEAD_DIM), k_idx.reshape(B, T, IDX_DIM)


def _dsa_sample_layer(hb, B, j, p, cache_k, cache_v, cache_kidx, page_table):
    M, D = hb.shape
    H = D // ATTN_HEAD_DIM
    q, k, v, q_idx, k_idx, w_idx = _dsa_project(hb, j, p)
    page = cache_kidx.shape[2]
    n_pages = page_table.shape[1]
    past = n_pages * page
    n_ih = w_idx.shape[1]
    q_idx3 = q_idx.reshape(B, n_ih, IDX_DIM)
    w_idx3 = w_idx.reshape(B, n_ih, 1)
    scores = dsa_page_scores(q_idx3, w_idx3, cache_kidx, page_table, j).reshape(B, past)
    k_top = max(1, min(TOPK_MAX, (past + 1) // 4))
    mask = dsa_select_sample(scores, q_idx3, w_idx3, k_idx.reshape(B, 1, IDX_DIM), k_top=k_top)
    dist = past - jnp.arange(past, dtype=jnp.int32)
    bias_rows = p["rel_bias"].astype(F32)[rel_bucket_of(dist, past)]
    bias_rows = bias_rows.reshape(n_pages, 1, page * H)
    mask_rows = jnp.repeat(mask[:, :past], H, axis=1).reshape(B, n_pages, 1, page * H)
    pool = cache_k.shape[1]
    ck = cache_k.reshape(cache_k.shape[0], pool, page * H, ATTN_HEAD_DIM)
    cv = cache_v.reshape(cache_v.shape[0], pool, page * H, ATTN_HEAD_DIM)
    hd = lambda z: z.reshape(B, H, ATTN_HEAD_DIM)
    bias_new = p["rel_bias"].astype(F32)[0].reshape(H, 1)
    mask_new = jnp.broadcast_to(mask[:, past:past + 1], (B, LANES)).reshape(B, 1, LANES)
    o = dsa_attend_sample(hd(q), ck, cv, page_table, j, bias_rows, mask_rows, hd(k), hd(v), bias_new, mask_new)
    out = matmul(o.reshape(M, D).astype(BF16), p["attn_w_o"], (j,))
    return out, k.reshape(B, 1, H, ATTN_HEAD_DIM), v.reshape(B, 1, H, ATTN_HEAD_DIM), k_idx.reshape(B, 1, IDX_DIM)


def _run(x, mods, sample, p, caches, states):
    B, T, D = x.shape
    M = B * T
    depth = len(mods)
    alpha = (2.0 * depth) ** 0.25
    modulate = lambda z, sc, sh: (z.reshape(B, T, D) * (1.0 + sc)[:, None] + sh[:, None]).reshape(M, D)
    xf = x.reshape(M, D).astype(F32)
    h = modulate(xf, mods[0][1], mods[0][0])
    hb = h.astype(BF16)
    st = ([], [], [])
    for i in range(depth):
        sh1, sc1, g1, sh2, sc2, g2 = mods[i]
        kind, j = i % 3, i // 3
        if kind == 0:
            if i > 0:
                h = modulate(xf, sc1, sh1)
            s0, w0 = (states["shift"][j], states["wkv"][j]) if sample else (jnp.zeros((B, D), F32), None)
            out, shift_new, wkv_new = _rwkv_layer(h, B, T, s0, w0, j, p)
            st[0].append((wkv_new, shift_new))
        elif kind == 1:
            x0r, x0i = (states["s5_re"][j], states["s5_im"][j]) if sample else (None, None)
            out, hr, hi = _s5_layer(hb, B, T, x0r, x0i, j, p)
            st[1].append((hr, hi))
        else:
            if sample:
                out, kn, vn, kin = _dsa_sample_layer(hb, B, j, p, *caches)
            else:
                out, kn, vn, kin = _dsa_prompt_layer(hb, B, T, j, p)
            st[2].append((kn, vn, kin))
        xf, hb = residual_ln(xf, out, 1.0 + g1, p["ln_g"][i, 0], p["ln_b"][i, 0], 1.0 + sc2, sh2,
                             rows_per_batch=T, alpha=alpha)
        f = matmul(matmul(hb, p["mlp_w1"], (i,), act="relu2", out_dtype=BF16), p["mlp_w2"], (i,), tk=2048, tn=1024)
        if i + 1 < depth:
            nsc, nsh = 1.0 + mods[i + 1][1], mods[i + 1][0]
        else:
            nsc, nsh = jnp.ones_like(sc1), jnp.zeros_like(sh1)
        xf, hb = residual_ln(xf, f, 1.0 + g2, p["ln_g"][i, 1], p["ln_b"][i, 1], nsc, nsh,
                             rows_per_batch=T, alpha=alpha)
    stk = lambda kind, k: jnp.stack([s[k] for s in st[kind]])
    return (xf.reshape(B, T, D), stk(0, 0), stk(0, 1), stk(1, 0), stk(1, 1), stk(2, 0), stk(2, 1), stk(2, 2))


def kernel(x_prompt, x_sample, cache_k, cache_v, cache_kidx, state_wkv, state_shift, state_s5_re, state_s5_im,
           page_table, c_prompt, c_sample, ada_w, ada_b, ln_g, ln_b, mlp_w1, mlp_w2,
           rwkv_mu, rwkv_w_rkv, rwkv_w_o, rwkv_w0, rwkv_w1, rwkv_w2, rwkv_a0, rwkv_a1, rwkv_a2,
           rwkv_g1, rwkv_g2, rwkv_k_k, rwkv_k_a, rwkv_r_k, rwkv_gn_g, rwkv_gn_b,
           s5_w_in, s5_lambda_re, s5_lambda_im, s5_log_dt, s5_b_re, s5_b_im, s5_c_re, s5_c_im, s5_d,
           s5_w_glu, s5_w_out, attn_w_qkv, attn_w_o, idx_w_q, idx_w_k, idx_w_w, rel_bias):
    p = dict(ln_g=ln_g, ln_b=ln_b, mlp_w1=mlp_w1, mlp_w2=mlp_w2, rwkv_mu=rwkv_mu, rwkv_w_rkv=rwkv_w_rkv,
             rwkv_w_o=rwkv_w_o, rwkv_w0=rwkv_w0, rwkv_w1=rwkv_w1, rwkv_w2=rwkv_w2, rwkv_a0=rwkv_a0,
             rwkv_a1=rwkv_a1, rwkv_a2=rwkv_a2, rwkv_g1=rwkv_g1, rwkv_g2=rwkv_g2, rwkv_k_k=rwkv_k_k,
             rwkv_k_a=rwkv_k_a, rwkv_r_k=rwkv_r_k, rwkv_gn_g=rwkv_gn_g, rwkv_gn_b=rwkv_gn_b,
             s5_w_in=s5_w_in, s5_lambda_re=s5_lambda_re, s5_lambda_im=s5_lambda_im, s5_log_dt=s5_log_dt,
             s5_b_re=s5_b_re, s5_b_im=s5_b_im, s5_c_re=s5_c_re, s5_c_im=s5_c_im, s5_d=s5_d,
             s5_w_glu=s5_w_glu, s5_w_out=s5_w_out, attn_w_qkv=attn_w_qkv, attn_w_o=attn_w_o,
             idx_w_q=idx_w_q, idx_w_k=idx_w_k, idx_w_w=idx_w_w, rel_bias=rel_bias)
    depth = ada_w.shape[0]
    Bp, Bs = c_prompt.shape[0], c_sample.shape[0]
    c_all = jax.nn.silu(jnp.concatenate([c_prompt, c_sample], axis=0).astype(F32))
    pad = -(-c_all.shape[0] // 16) * 16 - c_all.shape[0]
    c_all = jnp.pad(c_all, ((0, pad), (0, 0)))
    mods_p, mods_s = [], []
    for i in range(depth):
        mod = matmul(c_all, ada_w, (i,)) + ada_b[i]
        six = jnp.split(mod, 6, axis=-1)
        mods_p.append([z[:Bp] for z in six])
        mods_s.append([z[Bp:Bp + Bs] for z in six])
    states = dict(wkv=state_wkv, shift=state_shift, s5_re=state_s5_re, s5_im=state_s5_im)
    caches = (cache_k, cache_v, cache_kidx, page_table)
    out_p = _run(x_prompt, mods_p, False, p, caches, states)
    out_s = _run(x_sample, mods_s, True, p, caches, states)
    return (out_p[0], out_s[0]) + out_p[1:] + out_s[1:]
```

```python
import functools
import math

import jax
import jax.numpy as jnp
import numpy as np
from jax import lax
from jax.experimental import pallas as pl
from jax.experimental.pallas import tpu as pltpu

F32 = jnp.float32
BF16 = jnp.bfloat16

LANES = 128
SUBLANES = 8
VMEM_LIMIT_BYTES = 56 * 1024 * 1024

LN_EPS = 1e-5
GN_EPS = 64e-5
RWKV_HEAD_DIM = 64
S5_GROUP = 16
S5_SUB = 16
ATTN_HEAD_DIM = 128
IDX_DIM = 128
TOPK_MAX = 256
REL_BUCKETS = 32
REL_MAX_DIST = 1024
NEG_BIG = -1e30


def _cparams(sem):
    return pltpu.CompilerParams(dimension_semantics=sem, vmem_limit_bytes=VMEM_LIMIT_BYTES)


def _act(p, act):
    if act is None:
        return p
    if act == "relu2":
        r = jnp.maximum(p, 0.0)
        return r * r
    if act == "tanh":
        return jnp.tanh(p)
    if act == "sigmoid":
        return jax.nn.sigmoid(p)
    raise ValueError(act)


def _mm_kernel(a_ref, w_ref, o_ref, *scratch, nk, act):
    p = jnp.dot(a_ref[...].astype(BF16), w_ref[...].astype(BF16), preferred_element_type=F32)
    if nk == 1:
        o_ref[...] = _act(p, act).astype(o_ref.dtype)
        return
    acc_ref = scratch[0] if scratch else o_ref
    k = pl.program_id(2)

    @pl.when(k == 0)
    def _():
        acc_ref[...] = p

    @pl.when(k > 0)
    def _():
        acc_ref[...] += p

    if scratch or act is not None:
        @pl.when(k == nk - 1)
        def _():
            o_ref[...] = _act(acc_ref[...], act).astype(o_ref.dtype)


def _pick(n, pref):
    if n <= pref:
        return n
    t = (pref // LANES) * LANES
    while t >= LANES:
        if n % t == 0:
            return t
        t -= LANES
    return n


def matmul(a, w, widx=(), *, act=None, out_dtype=F32, n_off=0, n_size=None, tm=1024, tn=512, tk=4096):
    M, K = a.shape
    Kw, N = w.shape[-2:]
    assert K == Kw and len(widx) == w.ndim - 2
    n_size = N if n_size is None else n_size
    tm = _pick(M, tm) if M % SUBLANES == 0 else M
    tn = _pick(n_size, tn)
    tk = _pick(K, tk)
    assert M % tm == 0 and n_size % tn == 0 and K % tk == 0 and n_off % tn == 0
    nk = K // tk
    joff = n_off // tn
    lead = tuple(widx)
    w_spec = pl.BlockSpec((None,) * len(lead) + (tk, tn), lambda i, j, k: lead + (k, j + joff))
    return pl.pallas_call(
        functools.partial(_mm_kernel, nk=nk, act=act),
        grid=(M // tm, n_size // tn, nk),
        in_specs=[pl.BlockSpec((tm, tk), lambda i, j, k: (i, k)), w_spec],
        out_specs=pl.BlockSpec((tm, tn), lambda i, j, k: (i, j)),
        out_shape=jax.ShapeDtypeStruct((M, n_size), out_dtype),
        scratch_shapes=[pltpu.VMEM((tm, tn), F32)] if nk > 1 and out_dtype != F32 else [],
        compiler_params=_cparams(("parallel", "parallel", "arbitrary")),
    )(a, w)


def _allsum_sublanes(p):
    p = p + pltpu.roll(p, 4, 0)
    p = p + pltpu.roll(p, 2, 0)
    return p + pltpu.roll(p, 1, 0)


def _wkv_kernel(r_ref, wl_ref, k_ref, v_ref, al_ref, g_ref, par_ref, s0_ref, z_ref, sT_ref,
                s_ref, op_ref, vt_ref, yt_ref, sc_ref, in_ref, zs_ref, *, tc, n, nb, nh):
    c = pl.program_id(1)
    n8 = n // SUBLANES
    lw = nb * nh

    @pl.when(c == 0)
    def _():
        s_ref[...] = s0_ref[...]

    for o, ref in enumerate((r_ref, wl_ref, k_ref, v_ref, al_ref, g_ref)):
        for q in range(nb):
            in_ref[o, q] = ref[q].reshape(tc, nh, n)
    R_, WL_, K_, V_, AL_, G_ = range(6)

    w0, a0, k_k, k_a, r_k, gn_g, gn_b = (par_ref[q] for q in range(7))
    cat = lambda o, t: in_ref[o, :, t].reshape(lw, n).T
    fold = lambda x: x.reshape(n8, SUBLANES, lw)
    allsum = lambda x: _allsum_sublanes(jnp.sum(fold(x), axis=0))

    def prep(t, carry):
        r = cat(R_, t)
        k = cat(K_, t)
        w_log = -jax.nn.softplus(-(w0 + cat(WL_, t))) - 0.5
        w = jnp.exp(-jnp.exp(w_log))
        a = jax.nn.sigmoid(a0 + cat(AL_, t))
        kk = k * k_k
        nrm = jnp.maximum(jnp.sqrt(allsum(kk * kk)), 1e-12)
        kk3 = fold(kk) / nrm[None]
        kh = k * (1.0 + (a - 1.0) * k_a)
        b3 = kk3 * fold(a)
        op_ref[t, 0] = -kk3
        op_ref[t, 1] = fold(w * r)
        op_ref[t, 2] = fold(w)
        op_ref[t, 3] = b3
        op_ref[t, 4] = fold(kh)
        vt_ref[t] = cat(V_, t)
        sc_ref[t, 0] = _allsum_sublanes(jnp.sum(b3 * fold(r), axis=0))
        sc_ref[t, 1] = allsum(kh * r)
        sc_ref[t, 2] = allsum(r * kh * r_k)
        return carry

    lax.fori_loop(0, tc, prep, 0, unroll=min(tc, 4))

    def step(t, carry):
        br = sc_ref[t, 0, 0:1]
        kr = sc_ref[t, 1, 0:1]

        def row(i, carry2):
            si = s_ref[i]
            sa = _allsum_sublanes(jnp.sum(si * op_ref[t, 0], axis=0))
            y0 = _allsum_sublanes(jnp.sum(si * op_ref[t, 1], axis=0))
            vi = vt_ref[t, pl.ds(i, 1), :]
            s_ref[i] = si * op_ref[t, 2] + sa[None] * op_ref[t, 3] + vi[None] * op_ref[t, 4]
            yt_ref[t, pl.ds(i, 1), :] = y0[0:1] + sa[0:1] * br + vi * kr
            return carry2

        lax.fori_loop(0, n, row, 0, unroll=16)
        return carry

    lax.fori_loop(0, tc, step, 0)

    def post(t, carry):
        y = yt_ref[t]
        mu = allsum(y) * (1.0 / n)
        yc = fold(y) - mu[None]
        var = _allsum_sublanes(jnp.sum(yc * yc, axis=0)) * (1.0 / n)
        yn = (yc * lax.rsqrt(var + GN_EPS)[None]).reshape(n, lw) * gn_g + gn_b
        z = (yn + (fold(vt_ref[t]) * sc_ref[t, 2][None]).reshape(n, lw)) * cat(G_, t)
        zt = z.T
        for q in range(nb):
            zs_ref[q, t] = zt[q * nh:(q + 1) * nh]
        return carry

    lax.fori_loop(0, tc, post, 0, unroll=min(tc, 4))
    for q in range(nb):
        z_ref[q] = zs_ref[q].reshape(tc, nh * n).astype(z_ref.dtype)

    @pl.when(c == pl.num_programs(1) - 1)
    def _():
        sT_ref[...] = s_ref[...]


def wkv_scan(r, wl, k, v, al, g, par, s0, *, tc, nb):
    B, T, D = r.shape
    n = par.shape[1]
    nh = D // n
    n8 = n // SUBLANES
    lw = nb * nh
    assert T % tc == 0 and B % nb == 0
    op = pl.BlockSpec((nb, tc, D), lambda l, c: (l, c, 0))
    st = pl.BlockSpec((n, n8, SUBLANES, lw), lambda l, c: (0, 0, 0, l))
    return pl.pallas_call(
        functools.partial(_wkv_kernel, tc=tc, n=n, nb=nb, nh=nh),
        grid=(B // nb, T // tc),
        in_specs=[op, op, op, op, op, op, pl.BlockSpec((7, n, lw), lambda l, c: (0, 0, 0)), st],
        out_specs=[op, st],
        out_shape=[jax.ShapeDtypeStruct((B, T, D), BF16), jax.ShapeDtypeStruct(s0.shape, F32)],
        scratch_shapes=[pltpu.VMEM((n, n8, SUBLANES, lw), F32), pltpu.VMEM((tc, 5, n8, SUBLANES, lw), F32),
                        pltpu.VMEM((tc, n, lw), F32), pltpu.VMEM((tc, n, lw), F32),
                        pltpu.VMEM((tc, 3, SUBLANES, lw), F32),
                        pltpu.VMEM((6, nb, tc, nh, n), F32), pltpu.VMEM((nb, tc, nh, n), F32)],
        compiler_params=_cparams(("parallel", "arbitrary")),
    )(r, wl, k, v, al, g, par, s0)


_HI = lax.Precision.HIGHEST


def _s5_fold(lam_re, lam_im, log_dt, b_re, b_im, c_re, c_im, C):
    G, P, I = b_re.shape
    dt = jnp.exp(log_dt.astype(F32))[:, None]
    lr, li = lam_re.astype(F32), lam_im.astype(F32)
    mag = jnp.exp(lr * dt)
    abar_re, abar_im = mag * jnp.cos(li * dt), mag * jnp.sin(li * dt)
    den = lr * lr + li * li
    nr, ni = abar_re - 1.0, abar_im
    coef_re = (nr * lr + ni * li) / den
    coef_im = (ni * lr - nr * li) / den
    br_, bi_ = b_re.astype(F32), b_im.astype(F32)
    bb_re = coef_re[..., None] * br_ - coef_im[..., None] * bi_
    bb_im = coef_re[..., None] * bi_ + coef_im[..., None] * br_
    cr, ci = c_re.astype(F32), c_im.astype(F32)
    pw_re, pw_im = [jnp.ones_like(abar_re)], [jnp.zeros_like(abar_im)]
    for _ in range(C):
        pr, pi = pw_re[-1], pw_im[-1]
        pw_re.append(pr * abar_re - pi * abar_im)
        pw_im.append(pr * abar_im + pi * abar_re)
    a_re, a_im = jnp.stack(pw_re, 1), jnp.stack(pw_im, 1)
    ab_re = a_re[:, :C, :, None] * bb_re[:, None] - a_im[:, :C, :, None] * bb_im[:, None]
    ab_im = a_re[:, :C, :, None] * bb_im[:, None] + a_im[:, :C, :, None] * bb_re[:, None]
    kern = (jnp.einsum("gjp,gtpi->gtij", cr, ab_re, precision=_HI)
            - jnp.einsum("gjp,gtpi->gtij", ci, ab_im, precision=_HI))
    ar1, ai1 = a_re[:, 1:], a_im[:, 1:]
    v_re = cr[:, None] * ar1[:, :, None, :] - ci[:, None] * ai1[:, :, None, :]
    v_im = -cr[:, None] * ai1[:, :, None, :] - ci[:, None] * ar1[:, :, None, :]
    v_re = jnp.transpose(v_re, (0, 3, 1, 2)).reshape(G, P, C * I)
    v_im = jnp.transpose(v_im, (0, 3, 1, 2)).reshape(G, P, C * I)
    return kern, ab_re, ab_im, v_re, v_im, a_re[:, C][:, None], a_im[:, C][:, None]


def _s5_pergroup(folded):
    kern, ab_re, ab_im, v_re, v_im, a_re, a_im = folded
    G, C, P, I = ab_re.shape
    tt = np.arange(C)[None, :] - np.arange(C)[:, None]
    m = kern[:, np.clip(tt, 0, C - 1)]
    m = jnp.where((tt >= 0)[None, :, :, None, None], m, 0.0)
    m = jnp.transpose(m, (0, 1, 3, 2, 4)).reshape(G, C * I, C * I)
    rev = np.arange(C - 1, -1, -1)
    w_re = jnp.transpose(ab_re[:, rev], (0, 1, 3, 2)).reshape(G, C * I, P)
    w_im = jnp.transpose(ab_im[:, rev], (0, 1, 3, 2)).reshape(G, C * I, P)
    return m, w_re, w_im, v_re, v_im, a_re, a_im


def _s5_kernel(u_ref, m_ref, wre_ref, wim_ref, vre_ref, vim_ref, are_ref, aim_ref, x0re_ref, x0im_ref,
               y_ref, xre_ref, xim_ref, bur_ref, bui_ref, *, gb, nc, bp):
    for g in range(gb):
        u = u_ref[g]
        bur_ref[g] = jnp.dot(u, wre_ref[g], precision=_HI, preferred_element_type=F32)
        bui_ref[g] = jnp.dot(u, wim_ref[g], precision=_HI, preferred_element_type=F32)
    ar = are_ref[...]
    ai = aim_ref[...]

    def chunk(c, carry):
        xr, xi = carry
        rows = pl.ds(pl.multiple_of(c * bp, bp), bp)
        br = bur_ref[:, rows, :]
        bi = bui_ref[:, rows, :]
        bur_ref[:, rows, :] = xr
        bui_ref[:, rows, :] = xi
        return ar * xr - ai * xi + br, ar * xi + ai * xr + bi

    xr, xi = lax.fori_loop(0, nc, chunk, (x0re_ref[...], x0im_ref[...]))
    xre_ref[...] = xr
    xim_ref[...] = xi
    for g in range(gb):
        y = jnp.dot(u_ref[g], m_ref[g], precision=_HI, preferred_element_type=F32)
        y += jnp.dot(bur_ref[g], vre_ref[g], precision=_HI, preferred_element_type=F32)
        y += jnp.dot(bui_ref[g], vim_ref[g], precision=_HI, preferred_element_type=F32)
        y_ref[g] = y


def s5_scan(u, folded, x0_re, x0_im, *, gb):
    m, w_re, w_im, v_re, v_im, a_re, a_im = folded
    G, R, CI = u.shape
    P = w_re.shape[-1]
    bp = x0_re.shape[1]
    nc = R // bp
    assert G % gb == 0 and bp % SUBLANES == 0
    blk = lambda *s: pl.BlockSpec((gb,) + s, lambda g: (g,) + (0,) * len(s))
    return pl.pallas_call(
        functools.partial(_s5_kernel, gb=gb, nc=nc, bp=bp),
        grid=(G // gb,),
        in_specs=[blk(R, CI), blk(CI, CI), blk(CI, P), blk(CI, P), blk(P, CI), blk(P, CI),
                  blk(1, P), blk(1, P), blk(bp, P), blk(bp, P)],
        out_specs=[blk(R, CI), blk(bp, P), blk(bp, P)],
        out_shape=[jax.ShapeDtypeStruct((G, R, CI), F32), jax.ShapeDtypeStruct((G, bp, P), F32),
                   jax.ShapeDtypeStruct((G, bp, P), F32)],
        scratch_shapes=[pltpu.VMEM((gb, R, P), F32), pltpu.VMEM((gb, R, P), F32)],
        compiler_params=_cparams(("parallel",)),
    )(u, m, w_re, w_im, v_re, v_im, a_re, a_im, x0_re, x0_im)


def _s5_blockdiag(folded, gl):
    kern, ab_re, ab_im, v_re, v_im, a_re, a_im = folded
    G, C, P, I = ab_re.shape
    NG = G // gl
    kern2 = jnp.transpose(kern.reshape(NG, gl, C, I, I), (0, 1, 3, 2, 4)).reshape(NG, gl * I, C * I)
    rev = np.arange(C - 1, -1, -1)
    ab = jnp.stack([ab_re[:, rev], ab_im[:, rev]], axis=2)
    w2 = jnp.transpose(ab.reshape(NG, gl, C, 2, P, I), (0, 1, 5, 2, 3, 4)).reshape(NG, gl * I, C * 2 * P)
    v2 = jnp.stack([v_re, v_im], axis=0).reshape(2, NG, gl * P, C * I)
    v2 = jnp.transpose(v2, (1, 0, 2, 3)).reshape(NG, 2 * gl * P, C * I)
    r_tau, r_j = np.divmod(np.arange(C * I), I)
    cu, cr = np.divmod(np.arange((2 * C - 1) * gl * I), gl * I)
    sebig = ((cu[None, :] - (C - 1) == r_tau[:, None]) & ((cr % I)[None, :] == r_j[:, None]))
    rc = np.arange(2 * P)
    cc, cp = np.divmod(np.arange(2 * gl * P), gl * P)
    e2 = (cc[None, :] == (rc // P)[:, None]) & ((cp % P)[None, :] == (rc % P)[:, None])
    lanes = lambda a: a.reshape(NG, 1, gl * P)
    return (kern2, w2, v2, jnp.asarray(sebig, BF16), jnp.asarray(e2, BF16), lanes(a_re), lanes(a_im))


def _s5n_kernel(u_ref, d_ref, k2_ref, w2_ref, v2_ref, se_ref, e2_ref, are_ref, aim_ref, x0_ref, z_ref, xT_ref,
                m_ref, w_ref, v_ref, xcat_ref, bu_ref, xprev_ref, y_ref, *, B, nc, C, nsplit, gp, I, P):
    R = B * nc
    tsub = C // nsplit
    nq = 2 * gp // LANES
    K = C * LANES

    grp = lambda shape, axis, width: lax.broadcasted_iota(jnp.int32, shape, axis) // width
    same_m = grp((LANES, K), 0, I) == (lax.broadcasted_iota(jnp.int32, (LANES, K), 1) % LANES) // I
    same_w = grp((LANES, 2 * gp), 0, I) == (lax.broadcasted_iota(jnp.int32, (LANES, 2 * gp), 1) % gp) // P
    k2 = k2_ref[...].astype(BF16)
    for s in range(C):
        sel = se_ref[:, (C - 1 - s) * LANES:(C - 1 - s) * LANES + K]
        ms = jnp.dot(k2, sel, preferred_element_type=F32)
        m_ref[s * LANES:(s + 1) * LANES, :] = jnp.where(same_m, ms, 0.0).astype(BF16)
        ws = jnp.dot(w2_ref[:, s * 2 * P:(s + 1) * 2 * P].astype(BF16), e2_ref[...], preferred_element_type=F32)
        w_ref[s * LANES:(s + 1) * LANES, :] = jnp.where(same_w, ws, 0.0).astype(BF16)
    same_v = ((lax.broadcasted_iota(jnp.int32, (2 * gp, K), 0) % gp) // P
              == (lax.broadcasted_iota(jnp.int32, (2 * gp, K), 1) % LANES) // I)
    vs = jnp.dot(v2_ref[...].astype(BF16), se_ref[:, (C - 1) * LANES:(C - 1) * LANES + K], preferred_element_type=F32)
    v_ref[...] = jnp.where(same_v, vs, 0.0).astype(BF16)

    for s in range(C):
        xcat_ref[:, s * LANES:(s + 1) * LANES] = u_ref[pl.ds(s, R, stride=C), :].astype(BF16)
    bu = jnp.dot(xcat_ref[...], w_ref[...], preferred_element_type=F32)
    for q in range(nq):
        bu_ref[q] = bu[:, q * LANES:(q + 1) * LANES]
    ar = are_ref[...]
    ai = aim_ref[...]

    def block(c, carry):
        xr, xi = carry
        rows = pl.ds(c, B, stride=nc)
        x = jnp.concatenate([xr, xi], axis=-1)
        for q in range(nq):
            xprev_ref.at[q][rows, :] = x[:, q * LANES:(q + 1) * LANES]
        bu_c = jnp.concatenate([bu_ref.at[q][rows, :] for q in range(nq)], axis=-1)
        return ar * xr - ai * xi + bu_c[:, :gp], ar * xi + ai * xr + bu_c[:, gp:]

    x0 = x0_ref[...]
    xr, xi = lax.fori_loop(0, nc, block, (x0[:, :gp], x0[:, gp:]))
    xT_ref[...] = jnp.concatenate([xr, xi], axis=-1)

    xprev = jnp.concatenate([xprev_ref[q] for q in range(nq)], axis=-1).astype(BF16)
    for hf in range(nsplit):
        cols = slice(hf * tsub * LANES, (hf + 1) * tsub * LANES)
        y = jnp.dot(xcat_ref[...], m_ref[:, cols], preferred_element_type=F32)
        y = y + jnp.dot(xprev, v_ref[:, cols], preferred_element_type=F32)
        for tt in range(tsub):
            y_ref[pl.ds(hf * tsub + tt, R, stride=C), :] = y[:, tt * LANES:(tt + 1) * LANES]
    val = y_ref[...] + d_ref[...] * u_ref[...]
    z_ref[...] = jax.nn.gelu(val).astype(z_ref.dtype)


def s5_natural(u, d, bd, x0, *, B, C, nsplit=2):
    kern2, w2, v2, sebig, e2, a_re, a_im = bd
    M, D = u.shape
    NG, gi, ci = kern2.shape
    I = ci // C
    P = e2.shape[0] // 2
    K = C * LANES
    gp = a_re.shape[-1]
    nc = M // (B * C)
    R = B * nc
    assert D == NG * LANES and gi == LANES and 2 * P == LANES and C % nsplit == 0
    per_g = lambda *s: pl.BlockSpec((None,) + s, lambda n: (n,) + (0,) * len(s))
    const = lambda a: pl.BlockSpec(a.shape, lambda n: (0,) * a.ndim)
    return pl.pallas_call(
        functools.partial(_s5n_kernel, B=B, nc=nc, C=C, nsplit=nsplit, gp=gp, I=I, P=P),
        grid=(NG,),
        in_specs=[pl.BlockSpec((M, LANES), lambda n: (0, n)),
                  pl.BlockSpec((1, LANES), lambda n: (0, n)),
                  per_g(gi, ci), per_g(gi, C * 2 * P), per_g(2 * gp, ci), const(sebig), const(e2),
                  per_g(1, gp), per_g(1, gp), per_g(B, 2 * gp)],
        out_specs=[pl.BlockSpec((M, LANES), lambda n: (0, n)), per_g(B, 2 * gp)],
        out_shape=[jax.ShapeDtypeStruct((M, D), BF16), jax.ShapeDtypeStruct((NG, B, 2 * gp), F32)],
        scratch_shapes=[pltpu.VMEM((K, K), BF16), pltpu.VMEM((K, 2 * gp), BF16), pltpu.VMEM((2 * gp, K), BF16),
                        pltpu.VMEM((R, K), BF16), pltpu.VMEM((2 * gp // LANES, R, LANES), F32),
                        pltpu.VMEM((2 * gp // LANES, R, LANES), F32),
                        pltpu.VMEM((M, LANES), F32)],
        compiler_params=_cparams(("parallel",)),
    )(u, d, kern2, w2, v2, sebig, e2, a_re, a_im, x0)


def _ln_kernel(x_ref, f_ref, gate_ref, g_ref, b_ref, sc_ref, sh_ref, xo_ref, ho_ref, *, alpha):
    y = alpha * x_ref[...] + gate_ref[...] * f_ref[...]
    mu = jnp.mean(y, axis=-1, keepdims=True)
    yc = y - mu
    var = jnp.mean(yc * yc, axis=-1, keepdims=True)
    xn = yc * lax.rsqrt(var + LN_EPS) * g_ref[...] + b_ref[...]
    xo_ref[...] = xn
    ho_ref[...] = (xn * sc_ref[...] + sh_ref[...]).astype(ho_ref.dtype)


def residual_ln(x, f, gate, g, b, scale, shift, *, rows_per_batch, alpha, tm=256):
    M, D = x.shape
    B = gate.shape[0]
    if rows_per_batch == 1:
        tm = M
        mod = pl.BlockSpec((tm, D), lambda i: (i, 0))
        mods = (gate, scale, shift)
    else:
        tm = min(tm, rows_per_batch)
        assert rows_per_batch % tm == 0
        per = rows_per_batch // tm
        mod = pl.BlockSpec((None, 1, D), lambda i: (i // per, 0, 0))
        mods = tuple(z.reshape(B, 1, D) for z in (gate, scale, shift))
    row = pl.BlockSpec((tm, D), lambda i: (i, 0))
    vec = pl.BlockSpec((1, D), lambda i: (0, 0))
    return pl.pallas_call(
        functools.partial(_ln_kernel, alpha=alpha),
        grid=(M // tm,),
        in_specs=[row, row, mod, vec, vec, mod, mod],
        out_specs=[row, row],
        out_shape=[jax.ShapeDtypeStruct((M, D), F32), jax.ShapeDtypeStruct((M, D), BF16)],
        compiler_params=_cparams(("parallel",)),
    )(x, f, mods[0], g.reshape(1, D), b.reshape(1, D), mods[1], mods[2])


INT_MIN = -(2 ** 31)


def _order_key(x):
    bits = pltpu.bitcast(x, jnp.int32)
    return bits ^ ((bits >> 31) & 0x7FFFFFFF)


def _topk_member(key_ref, k_top):
    rows, L = key_ref.shape
    kf = jnp.float32(k_top)

    def count(pred):
        return jnp.sum(pred.astype(F32), axis=-1, keepdims=True)

    def bit_step(it, res):
        cand = res | (jnp.int32(1) << (31 - it))
        cnt = count(key_ref[...] >= (cand ^ INT_MIN))
        return jnp.where(cnt >= kf, cand, res)

    res = lax.fori_loop(0, 32, bit_step, jnp.zeros((rows, 1), jnp.int32))
    thr = res ^ INT_MIN
    key = key_ref[...]
    gt = key > thr
    eq = key == thr
    need = kf - count(gt)
    pos = lax.broadcasted_iota(jnp.int32, (rows, L), 1)
    nbits = max(1, (L - 1).bit_length())

    def pos_step(it, lim):
        cand = lim | (jnp.int32(1) << (nbits - 1 - it))
        cnt = count((key_ref[...] == thr) & (pos < cand))
        return jnp.where(cnt < need, cand, lim)

    lim = lax.fori_loop(0, nbits, pos_step, jnp.zeros((rows, 1), jnp.int32))
    return gt | (eq & (pos <= lim))


def _idx_kernel(q_ref, k_ref, w_ref, mask_ref, acc_ref, key_ref, *, tq, n_heads, k_top):
    i = pl.program_id(1)
    S = k_ref.shape[0]

    def tile(ii):
        se = (ii + 1) * tq
        kb = k_ref[:se, :].astype(BF16)
        acc = acc_ref.at[:, :se]
        keys = key_ref.at[:, :se]
        acc[...] = jnp.zeros((tq, se), F32)

        def weighted(h):
            q = q_ref[:, pl.ds(pl.multiple_of(h * IDX_DIM, IDX_DIM), IDX_DIM)]
            s = lax.dot_general(q, kb, (((1,), (1,)), ((), ())), preferred_element_type=F32)
            wcol = pltpu.roll(w_ref[...], LANES - h, 1)[:, :1]
            return wcol * jnp.maximum(s, 0.0)

        def heads(hh, carry):
            part = weighted(hh * hpp)
            for e in range(1, hpp):
                part = part + weighted(hh * hpp + e)
            acc[...] += part
            return carry

        hpp = 2 if n_heads % 2 == 0 else 1
        lax.fori_loop(0, n_heads // hpp, heads, 0)
        t_pos = ii * tq + lax.broadcasted_iota(jnp.int32, (tq, se), 0)
        s_pos = lax.broadcasted_iota(jnp.int32, (tq, se), 1)
        causal = s_pos <= t_pos
        keys[...] = _order_key(jnp.where(causal, acc[...], -jnp.inf))
        member = _topk_member(keys, k_top)
        mask_ref[:, :se] = jnp.where(member & causal, 0.0, NEG_BIG)
        if se < S:
            mask_ref[:, se:] = jnp.full((tq, S - se), NEG_BIG, F32)

    for ii in range(S // tq):
        pl.when(i == ii)(functools.partial(tile, ii))


def dsa_select_prompt(q_idx, k_idx, w_idx, *, B, T, k_top, tq=256):
    n_heads = q_idx.shape[1] // IDX_DIM
    tq = min(tq, T)
    nq = T // tq
    return pl.pallas_call(
        functools.partial(_idx_kernel, tq=tq, n_heads=n_heads, k_top=k_top),
        grid=(B, nq),
        in_specs=[pl.BlockSpec((tq, q_idx.shape[1]), lambda b, i: (b * nq + i, 0)),
                  pl.BlockSpec((T, IDX_DIM), lambda b, i: (b, 0)),
                  pl.BlockSpec((tq, LANES), lambda b, i: (b * nq + i, 0))],
        out_specs=pl.BlockSpec((tq, T), lambda b, i: (b * nq + i, 0)),
        out_shape=jax.ShapeDtypeStruct((B * T, T), F32),
        scratch_shapes=[pltpu.VMEM((tq, T), F32), pltpu.VMEM((tq, T), jnp.int32)],
        compiler_params=_cparams(("parallel", "parallel")),
    )(q_idx, k_idx, w_idx)


def _bucket_thresholds(max_dist):
    exact = REL_BUCKETS // 2
    d = np.arange(max_dist + 1)
    large = exact + np.floor(np.log(np.maximum(d, 1) / exact) / math.log(REL_MAX_DIST / exact)
                             * (REL_BUCKETS - exact) + 1e-9).astype(np.int64)
    bucket = np.where(d < exact, d, np.minimum(large, REL_BUCKETS - 1))
    return [int(np.argmax(bucket >= k)) if (bucket >= k).any() else max_dist + 1 for k in range(1, REL_BUCKETS)]


def rel_bucket_of(dist, max_dist):
    out = jnp.zeros(dist.shape, jnp.int32)
    for thr in _bucket_thresholds(max_dist):
        out = out + (dist >= thr).astype(jnp.int32)
    return out


def _attn_kernel(q_ref, k_ref, v_ref, mask_ref, bkt_ref, tab_ref, o_ref, *, scale, hb):
    tq = q_ref.shape[0]
    S = k_ref.shape[0]
    Dh = ATTN_HEAD_DIM
    i = pl.program_id(1)

    def tile(ii):
        se = (ii + 1) * tq
        for e in range(hb):
            cols = slice(e * Dh, (e + 1) * Dh)
            q = q_ref[:, cols].astype(BF16)
            kb = k_ref[:se, cols].astype(BF16)
            logits = lax.dot_general(q, kb, (((1,), (1,)), ((), ())), preferred_element_type=F32) * scale
            tab = jnp.broadcast_to(tab_ref[e], (tq, LANES))
            bias = jnp.concatenate(
                [jnp.take_along_axis(tab, bkt_ref[:, c * LANES:(c + 1) * LANES], axis=1)
                 for c in range(se // LANES)], axis=1)
            logits = logits + bias + mask_ref[:, :se]
            m = jnp.max(logits, axis=-1, keepdims=True)
            p = jnp.exp(logits - m)
            l = jnp.sum(p, axis=-1, keepdims=True)
            o = jnp.dot(p.astype(BF16), v_ref[:se, cols].astype(BF16), preferred_element_type=F32)
            o_ref[:, cols] = (o / l).astype(o_ref.dtype)

    for ii in range(S // tq):
        pl.when(i == ii)(functools.partial(tile, ii))


def dsa_attend_prompt(q, k, v, mask, bucket, bias_tab, *, B, T, tq=256, hb=4):
    H = q.shape[1] // ATTN_HEAD_DIM
    tq = min(tq, T)
    nq = T // tq
    hb = math.gcd(H, hb)
    w = hb * ATTN_HEAD_DIM
    return pl.pallas_call(
        functools.partial(_attn_kernel, scale=ATTN_HEAD_DIM ** -0.5, hb=hb),
        grid=(B, nq, H // hb),
        in_specs=[pl.BlockSpec((tq, w), lambda b, i, h: (b * nq + i, h)),
                  pl.BlockSpec((T, w), lambda b, i, h: (b, h)),
                  pl.BlockSpec((T, w), lambda b, i, h: (b, h)),
                  pl.BlockSpec((tq, T), lambda b, i, h: (b * nq + i, 0)),
                  pl.BlockSpec((tq, T), lambda b, i, h: (i, 0)),
                  pl.BlockSpec((hb, 1, LANES), lambda b, i, h: (h, 0, 0))],
        out_specs=pl.BlockSpec((tq, w), lambda b, i, h: (b * nq + i, h)),
        out_shape=jax.ShapeDtypeStruct(q.shape, BF16),
        compiler_params=_cparams(("parallel", "parallel", "arbitrary")),
    )(q, k, v, mask, bucket, bias_tab)


def _page_score_kernel(pt_ref, q_ref, w_ref, *refs):
    *kidx_refs, sc_ref = refs
    page = kidx_refs[0].shape[0]
    for e, kidx_ref in enumerate(kidx_refs):
        kb = kidx_ref[...].astype(BF16)
        s = lax.dot_general(q_ref[...], kb, (((1,), (1,)), ((), ())), preferred_element_type=F32)
        sc_ref[:, e * page:(e + 1) * page] = jnp.sum(w_ref[...] * jnp.maximum(s, 0.0), axis=0, keepdims=True)


def dsa_page_scores(q_idx, w_idx, cache_kidx, page_table, j, pps=8):
    B, HI, _ = q_idx.shape
    page = cache_kidx.shape[2]
    n_pages = page_table.shape[1]
    pps = math.gcd(n_pages, pps)
    page_spec = lambda e: pl.BlockSpec((None, None, page, IDX_DIM), lambda b, p, pt: (j, pt[b, p * pps + e], 0, 0))
    grid_spec = pltpu.PrefetchScalarGridSpec(
        num_scalar_prefetch=1,
        grid=(B, n_pages // pps),
        in_specs=[pl.BlockSpec((None, HI, IDX_DIM), lambda b, p, pt: (b, 0, 0)),
                  pl.BlockSpec((None, HI, 1), lambda b, p, pt: (b, 0, 0))] + [page_spec(e) for e in range(pps)],
        out_specs=pl.BlockSpec((None, 1, pps * page), lambda b, p, pt: (b, 0, p)),
    )
    return pl.pallas_call(
        _page_score_kernel,
        grid_spec=grid_spec,
        out_shape=jax.ShapeDtypeStruct((B, 1, n_pages * page), F32),
        compiler_params=_cparams(("parallel", "arbitrary")),
    )(page_table, q_idx, w_idx, *([cache_kidx] * pps))


def _sample_select_kernel(sc_ref, q_ref, w_ref, knew_ref, mask_ref, key_ref, *, past, k_top):
    B = sc_ref.shape[0]
    key_ref[:, :past] = _order_key(sc_ref[...])
    lane = lax.broadcasted_iota(jnp.int32, (1, LANES), 1)
    for b in range(B):
        kn = knew_ref[b].astype(BF16).astype(F32)
        s = jnp.sum(q_ref[b].astype(F32) * kn, axis=-1, keepdims=True)
        sc_new = jnp.sum(w_ref[b] * jnp.maximum(s, 0.0), axis=0, keepdims=True)
        tail = jnp.where(lane == 0, sc_new, -jnp.inf)
        key_ref[b:b + 1, past:] = _order_key(tail)
    member = _topk_member(key_ref, k_top)
    mask_ref[...] = jnp.where(member, 0.0, NEG_BIG)


def dsa_select_sample(scores, q_idx, w_idx, k_idx_new, *, k_top):
    B, past = scores.shape
    L = past + LANES
    return pl.pallas_call(
        functools.partial(_sample_select_kernel, past=past, k_top=k_top),
        out_shape=jax.ShapeDtypeStruct((B, L), F32),
        scratch_shapes=[pltpu.VMEM((B, L), jnp.int32)],
        compiler_params=pltpu.CompilerParams(vmem_limit_bytes=VMEM_LIMIT_BYTES),
    )(scores, q_idx, w_idx, k_idx_new)


def _page_attn_kernel(pt_ref, q_ref, k_ref, v_ref, bias_ref, mask_ref, knew_ref, vnew_ref, bnew_ref, mnew_ref,
                      o_ref, m_ref, l_ref, acc_ref, *, scale, H):
    p_idx = pl.program_id(1)

    @pl.when(p_idx == 0)
    def _():
        m_ref[...] = jnp.full_like(m_ref, NEG_BIG)
        l_ref[...] = jnp.zeros_like(l_ref)
        acc_ref[...] = jnp.zeros_like(acc_ref)

    q = q_ref[...].astype(BF16)
    cols = k_ref.shape[0]
    la = lax.dot_general(q, k_ref[...].astype(BF16), (((1,), (1,)), ((), ())), preferred_element_type=F32)
    la = la * scale + bias_ref[...]
    row_h = lax.broadcasted_iota(jnp.int32, (H, cols), 0)
    col_h = lax.broadcasted_iota(jnp.int32, (H, cols), 1) % H
    valid = (row_h == col_h) & (mask_ref[...] == 0.0)
    la = jnp.where(valid, la, NEG_BIG)
    m_old = m_ref[...]
    m_new = jnp.maximum(m_old, jnp.max(la, axis=-1, keepdims=True))
    alpha = jnp.exp(m_old - m_new)
    p = jnp.where(valid, jnp.exp(la - m_new), 0.0)
    l_ref[...] = alpha * l_ref[...] + jnp.sum(p, axis=-1, keepdims=True)
    acc_ref[...] = alpha * acc_ref[...] + jnp.dot(p.astype(BF16), v_ref[...].astype(BF16), preferred_element_type=F32)
    m_ref[...] = m_new

    @pl.when(p_idx == pl.num_programs(1) - 1)
    def _():
        kn = knew_ref[...].astype(BF16).astype(F32)
        s_new = jnp.sum(q.astype(F32) * kn, axis=-1, keepdims=True) * scale + bnew_ref[...]
        ok = mnew_ref[...][:, :1] == 0.0
        s_new = jnp.where(ok, s_new, NEG_BIG)
        m_o = m_ref[...]
        m_f = jnp.maximum(m_o, s_new)
        a_f = jnp.exp(m_o - m_f)
        p_new = jnp.where(ok, jnp.exp(s_new - m_f), 0.0)
        l_f = a_f * l_ref[...] + p_new
        acc = a_f * acc_ref[...] + p_new.astype(BF16).astype(F32) * vnew_ref[...].astype(BF16).astype(F32)
        o_ref[...] = acc / l_f


def dsa_attend_sample(q, cache_k, cache_v, page_table, j, bias_rows, mask_rows, k_new, v_new, bias_new, mask_new):
    B, H, Dh = q.shape
    cols = cache_k.shape[2]
    n_pages = page_table.shape[1]
    per_b = lambda *s: pl.BlockSpec((None,) + s, lambda b, p, pt: (b,) + (0,) * len(s))
    grid_spec = pltpu.PrefetchScalarGridSpec(
        num_scalar_prefetch=1,
        grid=(B, n_pages),
        in_specs=[per_b(H, Dh),
                  pl.BlockSpec((None, None, cols, Dh), lambda b, p, pt: (j, pt[b, p], 0, 0)),
                  pl.BlockSpec((None, None, cols, Dh), lambda b, p, pt: (j, pt[b, p], 0, 0)),
                  pl.BlockSpec((None, 1, cols), lambda b, p, pt: (p, 0, 0)),
                  pl.BlockSpec((None, None, 1, cols), lambda b, p, pt: (b, p, 0, 0)),
                  per_b(H, Dh), per_b(H, Dh),
                  pl.BlockSpec((H, 1), lambda b, p, pt: (0, 0)),
                  per_b(1, LANES)],
        out_specs=per_b(H, Dh),
        scratch_shapes=[pltpu.VMEM((H, 1), F32), pltpu.VMEM((H, 1), F32), pltpu.VMEM((H, Dh), F32)],
    )
    return pl.pallas_call(
        functools.partial(_page_attn_kernel, scale=Dh ** -0.5, H=H),
        grid_spec=grid_spec,
        out_shape=jax.ShapeDtypeStruct((B, H, Dh), F32),
        compiler_params=_cparams(("parallel", "arbitrary")),
    )(page_table, q, cache_k, cache_v, bias_rows, mask_rows, k_new, v_new, bias_new, mask_new)


def _rwkv_layer(h, B, T, shift0, wkv0, j, p):
    M, D = h.shape
    N = RWKV_HEAD_DIM
    H = D // N
    h3 = h.reshape(B, T, D)
    prev = jnp.concatenate([shift0[:, None].astype(F32), h3[:, :-1]], axis=1)
    d = prev - h3
    mu = p["rwkv_mu"][j]
    mix = [(h3 + d * mu[i]).astype(BF16).reshape(M, D) for i in range(6)]
    seq = lambda z: z.reshape(B, T, D)
    r = matmul(mix[0], p["rwkv_w_rkv"], (j, 0))
    k = matmul(mix[2], p["rwkv_w_rkv"], (j, 1))
    v = matmul(mix[3], p["rwkv_w_rkv"], (j, 2))
    wl = matmul(matmul(mix[1], p["rwkv_w1"], (j,), act="tanh", out_dtype=BF16), p["rwkv_w2"], (j,))
    al = matmul(matmul(mix[4], p["rwkv_a1"], (j,), out_dtype=BF16), p["rwkv_a2"], (j,))
    g = matmul(matmul(mix[5], p["rwkv_g1"], (j,), act="sigmoid", out_dtype=BF16), p["rwkv_g2"], (j,))
    nb = min(B, max(1, LANES // H))
    vec = lambda z: jnp.tile(z.astype(F32).reshape(H, N).T, (1, nb))
    par = jnp.stack([vec(p[q][j]) for q in ("rwkv_w0", "rwkv_a0", "rwkv_k_k", "rwkv_k_a", "rwkv_r_k",
                                            "rwkv_gn_g", "rwkv_gn_b")])
    if wkv0 is None:
        s0 = jnp.zeros((N, N // SUBLANES, SUBLANES, B * H), F32)
    else:
        s0 = jnp.transpose(wkv0.astype(F32), (2, 3, 0, 1)).reshape(N, N // SUBLANES, SUBLANES, B * H)
    z, s_last = wkv_scan(seq(r), seq(wl), seq(k), seq(v), seq(al), seq(g), par, s0, tc=math.gcd(T, 16), nb=nb)
    wkv_new = jnp.transpose(s_last.reshape(N, N, B, H), (2, 3, 0, 1))
    out = matmul(z.reshape(M, D), p["rwkv_w_o"], (j,))
    return out, h3[:, -1], wkv_new


def _s5_layer(hb, B, T, x0_re, x0_im, j, p):
    M, D = hb.shape
    I = S5_GROUP
    G = D // I
    P = p["s5_lambda_re"].shape[-1]
    C = S5_SUB if T % S5_SUB == 0 else 1
    nc = T // C
    bp = -(-B // SUBLANES) * SUBLANES
    u = matmul(hb, p["s5_w_in"], (j,))
    folded = _s5_fold(p["s5_lambda_re"][j], p["s5_lambda_im"][j], p["s5_log_dt"][j], p["s5_b_re"][j],
                      p["s5_b_im"][j], p["s5_c_re"][j], p["s5_c_im"][j], C)
    gl = LANES // I
    if C > 1 and G % gl == 0:
        NG = G // gl
        if x0_re is None:
            x0 = jnp.zeros((NG, B, 2 * gl * P), F32)
        else:
            st = lambda z: jnp.transpose(z.astype(F32).reshape(B, NG, gl * P), (1, 0, 2))
            x0 = jnp.concatenate([st(x0_re), st(x0_im)], axis=-1)
        z, xT = s5_natural(u, p["s5_d"][j].astype(F32).reshape(1, D), _s5_blockdiag(folded, gl), x0, B=B, C=C)
        fin = lambda q: jnp.transpose(q.reshape(NG, B, gl, P), (1, 0, 2, 3)).reshape(B, G, P)
        hr, hi = fin(xT[..., :gl * P]), fin(xT[..., gl * P:])
    else:
        ug = jnp.transpose(u.reshape(B, nc, C, G, I), (3, 1, 0, 2, 4))
        ug = jnp.pad(ug, ((0, 0), (0, 0), (0, bp - B), (0, 0), (0, 0))).reshape(G, nc * bp, C * I)
        if x0_re is None:
            x0r = x0i = jnp.zeros((G, bp, P), F32)
        else:
            st = lambda z: jnp.pad(jnp.transpose(z.astype(F32), (1, 0, 2)), ((0, 0), (0, bp - B), (0, 0)))
            x0r, x0i = st(x0_re), st(x0_im)
        yg, xr, xi = s5_scan(ug, _s5_pergroup(folded), x0r, x0i, gb=8)
        y = jnp.transpose(yg.reshape(G, nc, bp, C, I)[:, :, :B], (2, 1, 3, 0, 4)).reshape(M, D)
        z = jax.nn.gelu(y + p["s5_d"][j].astype(F32) * u).astype(BF16)
        fin = lambda q: jnp.transpose(q[:, :B], (1, 0, 2))
        hr, hi = fin(xr), fin(xi)
    ab = matmul(z, p["s5_w_glu"], (j,))
    glu = (ab[:, :D] * jax.nn.sigmoid(ab[:, D:])).astype(BF16)
    out = matmul(glu, p["s5_w_out"], (j,))
    return out, hr, hi


def _dsa_project(hb, j, p):
    D = hb.shape[1]
    q = matmul(hb, p["attn_w_qkv"], (j,), n_off=0, n_size=D)
    k = matmul(hb, p["attn_w_qkv"], (j,), n_off=D, n_size=D)
    v = matmul(hb, p["attn_w_qkv"], (j,), n_off=2 * D, n_size=D)
    q_idx = matmul(hb, p["idx_w_q"], (j,), out_dtype=BF16)
    k_idx = matmul(hb, p["idx_w_k"], (j,))
    n_ih = p["idx_w_w"].shape[-1]
    w_idx = matmul(hb, p["idx_w_w"], (j,)) * ((n_ih * IDX_DIM) ** -0.5)
    return q, k, v, q_idx, k_idx, w_idx


def _dsa_prompt_layer(hb, B, T, j, p):
    M, D = hb.shape
    H = D // ATTN_HEAD_DIM
    q, k, v, q_idx, k_idx, w_idx = _dsa_project(hb, j, p)
    k_top = max(1, min(TOPK_MAX, T // 4))
    w_pad = jnp.pad(w_idx, ((0, 0), (0, LANES - w_idx.shape[1])))
    mask = dsa_select_prompt(q_idx, k_idx, w_pad, B=B, T=T, k_top=k_top)
    pos = jnp.arange(T, dtype=jnp.int32)
    bucket = rel_bucket_of(pos[:, None] - pos[None, :], T)
    tab = jnp.pad(p["rel_bias"].astype(F32).T, ((0, 0), (0, LANES - REL_BUCKETS))).reshape(H, 1, LANES)
    o = dsa_attend_prompt(q, k, v, mask, bucket, tab, B=B, T=T)
    out = matmul(o, p["attn_w_o"], (j,))
    return out, k.reshape(B, T, H, ATTN_HEAD_DIM), v.reshape(B, T, H, ATTN_HEAD_DIM), k_idx.reshape(B, T, IDX_DIM)


def _dsa_sample_layer(hb, B, j, p, cache_k, cache_v, cache_kidx, page_table):
    M, D = hb.shape
    H = D // ATTN_HEAD_DIM
    q, k, v, q_idx, k_idx, w_idx = _dsa_project(hb, j, p)
    page = cache_kidx.shape[2]
    n_pages = page_table.shape[1]
    past = n_pages * page
    n_ih = w_idx.shape[1]
    q_idx3 = q_idx.reshape(B, n_ih, IDX_DIM)
    w_idx3 = w_idx.reshape(B, n_ih, 1)
    scores = dsa_page_scores(q_idx3, w_idx3, cache_kidx, page_table, j).reshape(B, past)
    k_top = max(1, min(TOPK_MAX, (past + 1) // 4))
    mask = dsa_select_sample(scores, q_idx3, w_idx3, k_idx.reshape(B, 1, IDX_DIM), k_top=k_top)
    dist = past - jnp.arange(past, dtype=jnp.int32)
    bias_rows = p["rel_bias"].astype(F32)[rel_bucket_of(dist, past)]
    bias_rows = bias_rows.reshape(n_pages, 1, page * H)
    mask_rows = jnp.repeat(mask[:, :past], H, axis=1).reshape(B, n_pages, 1, page * H)
    pool = cache_k.shape[1]
    ck = cache_k.reshape(cache_k.shape[0], pool, page * H, ATTN_HEAD_DIM)
    cv = cache_v.reshape(cache_v.shape[0], pool, page * H, ATTN_HEAD_DIM)
    hd = lambda z: z.reshape(B, H, ATTN_HEAD_DIM)
    bias_new = p["rel_bias"].astype(F32)[0].reshape(H, 1)
    mask_new = jnp.broadcast_to(mask[:, past:past + 1], (B, LANES)).reshape(B, 1, LANES)
    o = dsa_attend_sample(hd(q), ck, cv, page_table, j, bias_rows, mask_rows, hd(k), hd(v), bias_new, mask_new)
    out = matmul(o.reshape(M, D).astype(BF16), p["attn_w_o"], (j,))
    return out, k.reshape(B, 1, H, ATTN_HEAD_DIM), v.reshape(B, 1, H, ATTN_HEAD_DIM), k_idx.reshape(B, 1, IDX_DIM)


def _run(x, mods, sample, p, caches, states):
    B, T, D = x.shape
    M = B * T
    depth = len(mods)
    alpha = (2.0 * depth) ** 0.25
    modulate = lambda z, sc, sh: (z.reshape(B, T, D) * (1.0 + sc)[:, None] + sh[:, None]).reshape(M, D)
    xf = x.reshape(M, D).astype(F32)
    h = modulate(xf, mods[0][1], mods[0][0])
    hb = h.astype(BF16)
    st = ([], [], [])
    for i in range(depth):
        sh1, sc1, g1, sh2, sc2, g2 = mods[i]
        kind, j = i % 3, i // 3
        if kind == 0:
            if i > 0:
                h = modulate(xf, sc1, sh1)
            s0, w0 = (states["shift"][j], states["wkv"][j]) if sample else (jnp.zeros((B, D), F32), None)
            out, shift_new, wkv_new = _rwkv_layer(h, B, T, s0, w0, j, p)
            st[0].append((wkv_new, shift_new))
        elif kind == 1:
            x0r, x0i = (states["s5_re"][j], states["s5_im"][j]) if sample else (None, None)
            out, hr, hi = _s5_layer(hb, B, T, x0r, x0i, j, p)
            st[1].append((hr, hi))
        else:
            if sample:
                out, kn, vn, kin = _dsa_sample_layer(hb, B, j, p, *caches)
            else:
                out, kn, vn, kin = _dsa_prompt_layer(hb, B, T, j, p)
            st[2].append((kn, vn, kin))
        xf, hb = residual_ln(xf, out, 1.0 + g1, p["ln_g"][i, 0], p["ln_b"][i, 0], 1.0 + sc2, sh2,
                             rows_per_batch=T, alpha=alpha)
        f = matmul(matmul(hb, p["mlp_w1"], (i,), act="relu2", out_dtype=BF16), p["mlp_w2"], (i,),
                   tm=2048, tk=1024, tn=1024)
        if i + 1 < depth:
            nsc, nsh = 1.0 + mods[i + 1][1], mods[i + 1][0]
        else:
            nsc, nsh = jnp.ones_like(sc1), jnp.zeros_like(sh1)
        xf, hb = residual_ln(xf, f, 1.0 + g2, p["ln_g"][i, 1], p["ln_b"][i, 1], nsc, nsh,
                             rows_per_batch=T, alpha=alpha)
    stk = lambda kind, k: jnp.stack([s[k] for s in st[kind]])
    return (xf.reshape(B, T, D), stk(0, 0), stk(0, 1), stk(1, 0), stk(1, 1), stk(2, 0), stk(2, 1), stk(2, 2))


def kernel(x_prompt, x_sample, cache_k, cache_v, cache_kidx, state_wkv, state_shift, state_s5_re, state_s5_im,
           page_table, c_prompt, c_sample, ada_w, ada_b, ln_g, ln_b, mlp_w1, mlp_w2,
           rwkv_mu, rwkv_w_rkv, rwkv_w_o, rwkv_w0, rwkv_w1, rwkv_w2, rwkv_a0, rwkv_a1, rwkv_a2,
           rwkv_g1, rwkv_g2, rwkv_k_k, rwkv_k_a, rwkv_r_k, rwkv_gn_g, rwkv_gn_b,
           s5_w_in, s5_lambda_re, s5_lambda_im, s5_log_dt, s5_b_re, s5_b_im, s5_c_re, s5_c_im, s5_d,
           s5_w_glu, s5_w_out, attn_w_qkv, attn_w_o, idx_w_q, idx_w_k, idx_w_w, rel_bias):
    p = dict(ln_g=ln_g, ln_b=ln_b, mlp_w1=mlp_w1, mlp_w2=mlp_w2, rwkv_mu=rwkv_mu, rwkv_w_rkv=rwkv_w_rkv,
             rwkv_w_o=rwkv_w_o, rwkv_w0=rwkv_w0, rwkv_w1=rwkv_w1, rwkv_w2=rwkv_w2, rwkv_a0=rwkv_a0,
             rwkv_a1=rwkv_a1, rwkv_a2=rwkv_a2, rwkv_g1=rwkv_g1, rwkv_g2=rwkv_g2, rwkv_k_k=rwkv_k_k,
             rwkv_k_a=rwkv_k_a, rwkv_r_k=rwkv_r_k, rwkv_gn_g=rwkv_gn_g, rwkv_gn_b=rwkv_gn_b,
             s5_w_in=s5_w_in, s5_lambda_re=s5_lambda_re, s5_lambda_im=s5_lambda_im, s5_log_dt=s5_log_dt,
             s5_b_re=s5_b_re, s5_b_im=s5_b_im, s5_c_re=s5_c_re, s5_c_im=s5_c_im, s5_d=s5_d,
             s5_w_glu=s5_w_glu, s5_w_out=s5_w_out, attn_w_qkv=attn_w_qkv, attn_w_o=attn_w_o,
             idx_w_q=idx_w_q, idx_w_k=idx_w_k, idx_w_w=idx_w_w, rel_bias=rel_bias)
    depth = ada_w.shape[0]
    Bp, Bs = c_prompt.shape[0], c_sample.shape[0]
    c_all = jax.nn.silu(jnp.concatenate([c_prompt, c_sample], axis=0).astype(F32))
    pad = -(-c_all.shape[0] // 16) * 16 - c_all.shape[0]
    c_all = jnp.pad(c_all, ((0, pad), (0, 0)))
    mods_p, mods_s = [], []
    for i in range(depth):
        mod = matmul(c_all, ada_w, (i,)) + ada_b[i]
        six = jnp.split(mod, 6, axis=-1)
        mods_p.append([z[:Bp] for z in six])
        mods_s.append([z[Bp:Bp + Bs] for z in six])
    states = dict(wkv=state_wkv, shift=state_shift, s5_re=state_s5_re, s5_im=state_s5_im)
    caches = (cache_k, cache_v, cache_kidx, page_table)
    out_p = _run(x_prompt, mods_p, False, p, caches, states)
    out_s = _run(x_sample, mods_s, True, p, caches, states)
    return (out_p[0], out_s[0]) + out_p[1:] + out_s[1:]
```

```python
import functools
import math

import jax
import jax.numpy as jnp
import numpy as np
from jax import lax
from jax.experimental import pallas as pl
from jax.experimental.pallas import tpu as pltpu

F32 = jnp.float32
BF16 = jnp.bfloat16

LANES = 128
SUBLANES = 8
VMEM_LIMIT_BYTES = 56 * 1024 * 1024

LN_EPS = 1e-5
GN_EPS = 64e-5
RWKV_HEAD_DIM = 64
S5_GROUP = 16
S5_SUB = 16
ATTN_HEAD_DIM = 128
IDX_DIM = 128
TOPK_MAX = 256
REL_BUCKETS = 32
REL_MAX_DIST = 1024
NEG_BIG = -1e30


def _cparams(sem):
    return pltpu.CompilerParams(dimension_semantics=sem, vmem_limit_bytes=VMEM_LIMIT_BYTES)


def _act(p, act):
    if act is None:
        return p
    if act == "relu2":
        r = jnp.maximum(p, 0.0)
        return r * r
    if act == "tanh":
        return jnp.tanh(p)
    if act == "sigmoid":
        return jax.nn.sigmoid(p)
    raise ValueError(act)


def _mm_kernel(a_ref, w_ref, o_ref, *scratch, nk, act):
    p = jnp.dot(a_ref[...].astype(BF16), w_ref[...].astype(BF16), preferred_element_type=F32)
    if nk == 1:
        o_ref[...] = _act(p, act).astype(o_ref.dtype)
        return
    acc_ref, = scratch
    k = pl.program_id(2)

    @pl.when(k == 0)
    def _():
        acc_ref[...] = p

    @pl.when(k > 0)
    def _():
        acc_ref[...] += p

    @pl.when(k == nk - 1)
    def _():
        o_ref[...] = _act(acc_ref[...], act).astype(o_ref.dtype)


def _pick(n, pref):
    if n <= pref:
        return n
    t = (pref // LANES) * LANES
    while t >= LANES:
        if n % t == 0:
            return t
        t -= LANES
    return n


def matmul(a, w, widx=(), *, act=None, out_dtype=F32, n_off=0, n_size=None, tm=1024, tn=512, tk=4096):
    M, K = a.shape
    Kw, N = w.shape[-2:]
    assert K == Kw and len(widx) == w.ndim - 2
    n_size = N if n_size is None else n_size
    tm = _pick(M, tm) if M % SUBLANES == 0 else M
    tn = _pick(n_size, tn)
    tk = _pick(K, tk)
    assert M % tm == 0 and n_size % tn == 0 and K % tk == 0 and n_off % tn == 0
    nk = K // tk
    joff = n_off // tn
    lead = tuple(widx)
    w_spec = pl.BlockSpec((None,) * len(lead) + (tk, tn), lambda i, j, k: lead + (k, j + joff))
    return pl.pallas_call(
        functools.partial(_mm_kernel, nk=nk, act=act),
        grid=(M // tm, n_size // tn, nk),
        in_specs=[pl.BlockSpec((tm, tk), lambda i, j, k: (i, k)), w_spec],
        out_specs=pl.BlockSpec((tm, tn), lambda i, j, k: (i, j)),
        out_shape=jax.ShapeDtypeStruct((M, n_size), out_dtype),
        scratch_shapes=[pltpu.VMEM((tm, tn), F32)] if nk > 1 else [],
        compiler_params=_cparams(("parallel", "parallel", "arbitrary")),
    )(a, w)


def _allsum_sublanes(p):
    p = p + pltpu.roll(p, 4, 0)
    p = p + pltpu.roll(p, 2, 0)
    return p + pltpu.roll(p, 1, 0)


def _wkv_kernel(r_ref, wl_ref, k_ref, v_ref, al_ref, g_ref, par_ref, s0_ref, z_ref, sT_ref,
                s_ref, op_ref, vt_ref, yt_ref, sc_ref, in_ref, zs_ref, *, tc, n, nb, nh):
    c = pl.program_id(1)
    n8 = n // SUBLANES
    lw = nb * nh

    @pl.when(c == 0)
    def _():
        s_ref[...] = s0_ref[...]

    for o, ref in enumerate((r_ref, wl_ref, k_ref, v_ref, al_ref, g_ref)):
        for q in range(nb):
            in_ref[o, q] = ref[q].reshape(tc, nh, n)
    R_, WL_, K_, V_, AL_, G_ = range(6)

    w0, a0, k_k, k_a, r_k, gn_g, gn_b = (par_ref[q] for q in range(7))
    cat = lambda o, t: in_ref[o, :, t].reshape(lw, n).T
    fold = lambda x: x.reshape(n8, SUBLANES, lw)
    allsum = lambda x: _allsum_sublanes(jnp.sum(fold(x), axis=0))

    def prep(t, carry):
        r = cat(R_, t)
        k = cat(K_, t)
        w_log = -jax.nn.softplus(-(w0 + cat(WL_, t))) - 0.5
        w = jnp.exp(-jnp.exp(w_log))
        a = jax.nn.sigmoid(a0 + cat(AL_, t))
        kk = k * k_k
        nrm = jnp.maximum(jnp.sqrt(allsum(kk * kk)), 1e-12)
        kk3 = fold(kk) / nrm[None]
        kh = k * (1.0 + (a - 1.0) * k_a)
        b3 = kk3 * fold(a)
        op_ref[t, 0] = -kk3
        op_ref[t, 1] = fold(w * r)
        op_ref[t, 2] = fold(w)
        op_ref[t, 3] = b3
        op_ref[t, 4] = fold(kh)
        vt_ref[t] = cat(V_, t)
        sc_ref[t, 0] = _allsum_sublanes(jnp.sum(b3 * fold(r), axis=0))
        sc_ref[t, 1] = allsum(kh * r)
        sc_ref[t, 2] = allsum(r * kh * r_k)
        return carry

    lax.fori_loop(0, tc, prep, 0, unroll=min(tc, 4))

    def step(t, carry):
        br = sc_ref[t, 0, 0:1]
        kr = sc_ref[t, 1, 0:1]

        def row(i, carry2):
            si = s_ref[i]
            sa = _allsum_sublanes(jnp.sum(si * op_ref[t, 0], axis=0))
            y0 = _allsum_sublanes(jnp.sum(si * op_ref[t, 1], axis=0))
            vi = vt_ref[t, pl.ds(i, 1), :]
            s_ref[i] = si * op_ref[t, 2] + sa[None] * op_ref[t, 3] + vi[None] * op_ref[t, 4]
            yt_ref[t, pl.ds(i, 1), :] = y0[0:1] + sa[0:1] * br + vi * kr
            return carry2

        lax.fori_loop(0, n, row, 0, unroll=16)
        return carry

    lax.fori_loop(0, tc, step, 0)

    def post(t, carry):
        y = yt_ref[t]
        mu = allsum(y) * (1.0 / n)
        yc = fold(y) - mu[None]
        var = _allsum_sublanes(jnp.sum(yc * yc, axis=0)) * (1.0 / n)
        yn = (yc * lax.rsqrt(var + GN_EPS)[None]).reshape(n, lw) * gn_g + gn_b
        z = (yn + (fold(vt_ref[t]) * sc_ref[t, 2][None]).reshape(n, lw)) * cat(G_, t)
        zt = z.T
        for q in range(nb):
            zs_ref[q, t] = zt[q * nh:(q + 1) * nh]
        return carry

    lax.fori_loop(0, tc, post, 0, unroll=min(tc, 4))
    for q in range(nb):
        z_ref[q] = zs_ref[q].reshape(tc, nh * n).astype(z_ref.dtype)

    @pl.when(c == pl.num_programs(1) - 1)
    def _():
        sT_ref[...] = s_ref[...]


def wkv_scan(r, wl, k, v, al, g, par, s0, *, tc, nb):
    B, T, D = r.shape
    n = par.shape[1]
    nh = D // n
    n8 = n // SUBLANES
    lw = nb * nh
    assert T % tc == 0 and B % nb == 0
    op = pl.BlockSpec((nb, tc, D), lambda l, c: (l, c, 0))
    st = pl.BlockSpec((n, n8, SUBLANES, lw), lambda l, c: (0, 0, 0, l))
    return pl.pallas_call(
        functools.partial(_wkv_kernel, tc=tc, n=n, nb=nb, nh=nh),
        grid=(B // nb, T // tc),
        in_specs=[op, op, op, op, op, op, pl.BlockSpec((7, n, lw), lambda l, c: (0, 0, 0)), st],
        out_specs=[op, st],
        out_shape=[jax.ShapeDtypeStruct((B, T, D), BF16), jax.ShapeDtypeStruct(s0.shape, F32)],
        scratch_shapes=[pltpu.VMEM((n, n8, SUBLANES, lw), F32), pltpu.VMEM((tc, 5, n8, SUBLANES, lw), F32),
                        pltpu.VMEM((tc, n, lw), F32), pltpu.VMEM((tc, n, lw), F32),
                        pltpu.VMEM((tc, 3, SUBLANES, lw), F32),
                        pltpu.VMEM((6, nb, tc, nh, n), F32), pltpu.VMEM((nb, tc, nh, n), F32)],
        compiler_params=_cparams(("parallel", "arbitrary")),
    )(r, wl, k, v, al, g, par, s0)


_HI = lax.Precision.HIGHEST


def _s5_fold(lam_re, lam_im, log_dt, b_re, b_im, c_re, c_im, C):
    G, P, I = b_re.shape
    dt = jnp.exp(log_dt.astype(F32))[:, None]
    lr, li = lam_re.astype(F32), lam_im.astype(F32)
    mag = jnp.exp(lr * dt)
    abar_re, abar_im = mag * jnp.cos(li * dt), mag * jnp.sin(li * dt)
    den = lr * lr + li * li
    nr, ni = abar_re - 1.0, abar_im
    coef_re = (nr * lr + ni * li) / den
    coef_im = (ni * lr - nr * li) / den
    br_, bi_ = b_re.astype(F32), b_im.astype(F32)
    bb_re = coef_re[..., None] * br_ - coef_im[..., None] * bi_
    bb_im = coef_re[..., None] * bi_ + coef_im[..., None] * br_
    cr, ci = c_re.astype(F32), c_im.astype(F32)
    pw_re, pw_im = [jnp.ones_like(abar_re)], [jnp.zeros_like(abar_im)]
    for _ in range(C):
        pr, pi = pw_re[-1], pw_im[-1]
        pw_re.append(pr * abar_re - pi * abar_im)
        pw_im.append(pr * abar_im + pi * abar_re)
    a_re, a_im = jnp.stack(pw_re, 1), jnp.stack(pw_im, 1)
    ab_re = a_re[:, :C, :, None] * bb_re[:, None] - a_im[:, :C, :, None] * bb_im[:, None]
    ab_im = a_re[:, :C, :, None] * bb_im[:, None] + a_im[:, :C, :, None] * bb_re[:, None]
    kern = (jnp.einsum("gjp,gtpi->gtij", cr, ab_re, precision=_HI)
            - jnp.einsum("gjp,gtpi->gtij", ci, ab_im, precision=_HI))
    ar1, ai1 = a_re[:, 1:], a_im[:, 1:]
    v_re = cr[:, None] * ar1[:, :, None, :] - ci[:, None] * ai1[:, :, None, :]
    v_im = -cr[:, None] * ai1[:, :, None, :] - ci[:, None] * ar1[:, :, None, :]
    v_re = jnp.transpose(v_re, (0, 3, 1, 2)).reshape(G, P, C * I)
    v_im = jnp.transpose(v_im, (0, 3, 1, 2)).reshape(G, P, C * I)
    return kern, ab_re, ab_im, v_re, v_im, a_re[:, C][:, None], a_im[:, C][:, None]


def _s5_pergroup(folded):
    kern, ab_re, ab_im, v_re, v_im, a_re, a_im = folded
    G, C, P, I = ab_re.shape
    tt = np.arange(C)[None, :] - np.arange(C)[:, None]
    m = kern[:, np.clip(tt, 0, C - 1)]
    m = jnp.where((tt >= 0)[None, :, :, None, None], m, 0.0)
    m = jnp.transpose(m, (0, 1, 3, 2, 4)).reshape(G, C * I, C * I)
    rev = np.arange(C - 1, -1, -1)
    w_re = jnp.transpose(ab_re[:, rev], (0, 1, 3, 2)).reshape(G, C * I, P)
    w_im = jnp.transpose(ab_im[:, rev], (0, 1, 3, 2)).reshape(G, C * I, P)
    return m, w_re, w_im, v_re, v_im, a_re, a_im


def _s5_kernel(u_ref, m_ref, wre_ref, wim_ref, vre_ref, vim_ref, are_ref, aim_ref, x0re_ref, x0im_ref,
               y_ref, xre_ref, xim_ref, bur_ref, bui_ref, *, gb, nc, bp):
    for g in range(gb):
        u = u_ref[g]
        bur_ref[g] = jnp.dot(u, wre_ref[g], precision=_HI, preferred_element_type=F32)
        bui_ref[g] = jnp.dot(u, wim_ref[g], precision=_HI, preferred_element_type=F32)
    ar = are_ref[...]
    ai = aim_ref[...]

    def chunk(c, carry):
        xr, xi = carry
        rows = pl.ds(pl.multiple_of(c * bp, bp), bp)
        br = bur_ref[:, rows, :]
        bi = bui_ref[:, rows, :]
        bur_ref[:, rows, :] = xr
        bui_ref[:, rows, :] = xi
        return ar * xr - ai * xi + br, ar * xi + ai * xr + bi

    xr, xi = lax.fori_loop(0, nc, chunk, (x0re_ref[...], x0im_ref[...]))
    xre_ref[...] = xr
    xim_ref[...] = xi
    for g in range(gb):
        y = jnp.dot(u_ref[g], m_ref[g], precision=_HI, preferred_element_type=F32)
        y += jnp.dot(bur_ref[g], vre_ref[g], precision=_HI, preferred_element_type=F32)
        y += jnp.dot(bui_ref[g], vim_ref[g], precision=_HI, preferred_element_type=F32)
        y_ref[g] = y


def s5_scan(u, folded, x0_re, x0_im, *, gb):
    m, w_re, w_im, v_re, v_im, a_re, a_im = folded
    G, R, CI = u.shape
    P = w_re.shape[-1]
    bp = x0_re.shape[1]
    nc = R // bp
    assert G % gb == 0 and bp % SUBLANES == 0
    blk = lambda *s: pl.BlockSpec((gb,) + s, lambda g: (g,) + (0,) * len(s))
    return pl.pallas_call(
        functools.partial(_s5_kernel, gb=gb, nc=nc, bp=bp),
        grid=(G // gb,),
        in_specs=[blk(R, CI), blk(CI, CI), blk(CI, P), blk(CI, P), blk(P, CI), blk(P, CI),
                  blk(1, P), blk(1, P), blk(bp, P), blk(bp, P)],
        out_specs=[blk(R, CI), blk(bp, P), blk(bp, P)],
        out_shape=[jax.ShapeDtypeStruct((G, R, CI), F32), jax.ShapeDtypeStruct((G, bp, P), F32),
                   jax.ShapeDtypeStruct((G, bp, P), F32)],
        scratch_shapes=[pltpu.VMEM((gb, R, P), F32), pltpu.VMEM((gb, R, P), F32)],
        compiler_params=_cparams(("parallel",)),
    )(u, m, w_re, w_im, v_re, v_im, a_re, a_im, x0_re, x0_im)


def _s5_blockdiag(folded, gl):
    kern, ab_re, ab_im, v_re, v_im, a_re, a_im = folded
    G, C, P, I = ab_re.shape
    NG = G // gl
    kern2 = jnp.transpose(kern.reshape(NG, gl, C, I, I), (0, 1, 3, 2, 4)).reshape(NG, gl * I, C * I)
    rev = np.arange(C - 1, -1, -1)
    ab = jnp.stack([ab_re[:, rev], ab_im[:, rev]], axis=2)
    w2 = jnp.transpose(ab.reshape(NG, gl, C, 2, P, I), (0, 1, 5, 2, 3, 4)).reshape(NG, gl * I, C * 2 * P)
    v2 = jnp.stack([v_re, v_im], axis=0).reshape(2, NG, gl * P, C * I)
    v2 = jnp.transpose(v2, (1, 0, 2, 3)).reshape(NG, 2 * gl * P, C * I)
    r_tau, r_j = np.divmod(np.arange(C * I), I)
    cu, cr = np.divmod(np.arange((2 * C - 1) * gl * I), gl * I)
    sebig = ((cu[None, :] - (C - 1) == r_tau[:, None]) & ((cr % I)[None, :] == r_j[:, None]))
    rc = np.arange(2 * P)
    cc, cp = np.divmod(np.arange(2 * gl * P), gl * P)
    e2 = (cc[None, :] == (rc // P)[:, None]) & ((cp % P)[None, :] == (rc % P)[:, None])
    lanes = lambda a: a.reshape(NG, 1, gl * P)
    return (kern2, w2, v2, jnp.asarray(sebig, BF16), jnp.asarray(e2, BF16), lanes(a_re), lanes(a_im))


def _s5n_kernel(u_ref, d_ref, k2_ref, w2_ref, v2_ref, se_ref, e2_ref, are_ref, aim_ref, x0_ref, z_ref, xT_ref,
                m_ref, w_ref, v_ref, xcat_ref, bu_ref, xprev_ref, y_ref, *, B, nc, C, nsplit, gp, I, P):
    R = B * nc
    tsub = C // nsplit
    nq = 2 * gp // LANES
    K = C * LANES

    grp = lambda shape, axis, width: lax.broadcasted_iota(jnp.int32, shape, axis) // width
    same_m = grp((LANES, K), 0, I) == (lax.broadcasted_iota(jnp.int32, (LANES, K), 1) % LANES) // I
    same_w = grp((LANES, 2 * gp), 0, I) == (lax.broadcasted_iota(jnp.int32, (LANES, 2 * gp), 1) % gp) // P
    k2 = k2_ref[...].astype(BF16)
    for s in range(C):
        sel = se_ref[:, (C - 1 - s) * LANES:(C - 1 - s) * LANES + K]
        ms = jnp.dot(k2, sel, preferred_element_type=F32)
        m_ref[s * LANES:(s + 1) * LANES, :] = jnp.where(same_m, ms, 0.0).astype(BF16)
        ws = jnp.dot(w2_ref[:, s * 2 * P:(s + 1) * 2 * P].astype(BF16), e2_ref[...], preferred_element_type=F32)
        w_ref[s * LANES:(s + 1) * LANES, :] = jnp.where(same_w, ws, 0.0).astype(BF16)
    same_v = ((lax.broadcasted_iota(jnp.int32, (2 * gp, K), 0) % gp) // P
              == (lax.broadcasted_iota(jnp.int32, (2 * gp, K), 1) % LANES) // I)
    vs = jnp.dot(v2_ref[...].astype(BF16), se_ref[:, (C - 1) * LANES:(C - 1) * LANES + K], preferred_element_type=F32)
    v_ref[...] = jnp.where(same_v, vs, 0.0).astype(BF16)

    for s in range(C):
        xcat_ref[:, s * LANES:(s + 1) * LANES] = u_ref[pl.ds(s, R, stride=C), :].astype(BF16)
    bu = jnp.dot(xcat_ref[...], w_ref[...], preferred_element_type=F32)
    for q in range(nq):
        bu_ref[q] = bu[:, q * LANES:(q + 1) * LANES]
    ar = are_ref[...]
    ai = aim_ref[...]

    def block(c, carry):
        xr, xi = carry
        rows = pl.ds(c, B, stride=nc)
        x = jnp.concatenate([xr, xi], axis=-1)
        for q in range(nq):
            xprev_ref.at[q][rows, :] = x[:, q * LANES:(q + 1) * LANES]
        bu_c = jnp.concatenate([bu_ref.at[q][rows, :] for q in range(nq)], axis=-1)
        return ar * xr - ai * xi + bu_c[:, :gp], ar * xi + ai * xr + bu_c[:, gp:]

    x0 = x0_ref[...]
    xr, xi = lax.fori_loop(0, nc, block, (x0[:, :gp], x0[:, gp:]))
    xT_ref[...] = jnp.concatenate([xr, xi], axis=-1)

    xprev = jnp.concatenate([xprev_ref[q] for q in range(nq)], axis=-1).astype(BF16)
    for hf in range(nsplit):
        cols = slice(hf * tsub * LANES, (hf + 1) * tsub * LANES)
        y = jnp.dot(xcat_ref[...], m_ref[:, cols], preferred_element_type=F32)
        y = y + jnp.dot(xprev, v_ref[:, cols], preferred_element_type=F32)
        for tt in range(tsub):
            y_ref[pl.ds(hf * tsub + tt, R, stride=C), :] = y[:, tt * LANES:(tt + 1) * LANES]
    val = y_ref[...] + d_ref[...] * u_ref[...]
    z_ref[...] = jax.nn.gelu(val).astype(z_ref.dtype)


def s5_natural(u, d, bd, x0, *, B, C, nsplit=2):
    kern2, w2, v2, sebig, e2, a_re, a_im = bd
    M, D = u.shape
    NG, gi, ci = kern2.shape
    I = ci // C
    P = e2.shape[0] // 2
    K = C * LANES
    gp = a_re.shape[-1]
    nc = M // (B * C)
    R = B * nc
    assert D == NG * LANES and gi == LANES and 2 * P == LANES and C % nsplit == 0
    per_g = lambda *s: pl.BlockSpec((None,) + s, lambda n: (n,) + (0,) * len(s))
    const = lambda a: pl.BlockSpec(a.shape, lambda n: (0,) * a.ndim)
    return pl.pallas_call(
        functools.partial(_s5n_kernel, B=B, nc=nc, C=C, nsplit=nsplit, gp=gp, I=I, P=P),
        grid=(NG,),
        in_specs=[pl.BlockSpec((M, LANES), lambda n: (0, n)),
                  pl.BlockSpec((1, LANES), lambda n: (0, n)),
                  per_g(gi, ci), per_g(gi, C * 2 * P), per_g(2 * gp, ci), const(sebig), const(e2),
                  per_g(1, gp), per_g(1, gp), per_g(B, 2 * gp)],
        out_specs=[pl.BlockSpec((M, LANES), lambda n: (0, n)), per_g(B, 2 * gp)],
        out_shape=[jax.ShapeDtypeStruct((M, D), BF16), jax.ShapeDtypeStruct((NG, B, 2 * gp), F32)],
        scratch_shapes=[pltpu.VMEM((K, K), BF16), pltpu.VMEM((K, 2 * gp), BF16), pltpu.VMEM((2 * gp, K), BF16),
                        pltpu.VMEM((R, K), BF16), pltpu.VMEM((2 * gp // LANES, R, LANES), F32),
                        pltpu.VMEM((2 * gp // LANES, R, LANES), F32),
                        pltpu.VMEM((M, LANES), F32)],
        compiler_params=_cparams(("parallel",)),
    )(u, d, kern2, w2, v2, sebig, e2, a_re, a_im, x0)


def _ln_kernel(x_ref, f_ref, gate_ref, g_ref, b_ref, sc_ref, sh_ref, xo_ref, ho_ref, *, alpha):
    y = alpha * x_ref[...] + gate_ref[...] * f_ref[...]
    mu = jnp.mean(y, axis=-1, keepdims=True)
    yc = y - mu
    var = jnp.mean(yc * yc, axis=-1, keepdims=True)
    xn = yc * lax.rsqrt(var + LN_EPS) * g_ref[...] + b_ref[...]
    xo_ref[...] = xn
    ho_ref[...] = (xn * sc_ref[...] + sh_ref[...]).astype(ho_ref.dtype)


def residual_ln(x, f, gate, g, b, scale, shift, *, rows_per_batch, alpha, tm=256):
    M, D = x.shape
    B = gate.shape[0]
    if rows_per_batch == 1:
        tm = M
        mod = pl.BlockSpec((tm, D), lambda i: (i, 0))
        mods = (gate, scale, shift)
    else:
        tm = min(tm, rows_per_batch)
        assert rows_per_batch % tm == 0
        per = rows_per_batch // tm
        mod = pl.BlockSpec((None, 1, D), lambda i: (i // per, 0, 0))
        mods = tuple(z.reshape(B, 1, D) for z in (gate, scale, shift))
    row = pl.BlockSpec((tm, D), lambda i: (i, 0))
    vec = pl.BlockSpec((1, D), lambda i: (0, 0))
    return pl.pallas_call(
        functools.partial(_ln_kernel, alpha=alpha),
        grid=(M // tm,),
        in_specs=[row, row, mod, vec, vec, mod, mod],
        out_specs=[row, row],
        out_shape=[jax.ShapeDtypeStruct((M, D), F32), jax.ShapeDtypeStruct((M, D), BF16)],
        compiler_params=_cparams(("parallel",)),
    )(x, f, mods[0], g.reshape(1, D), b.reshape(1, D), mods[1], mods[2])


INT_MIN = -(2 ** 31)


def _order_key(x):
    bits = pltpu.bitcast(x, jnp.int32)
    return bits ^ ((bits >> 31) & 0x7FFFFFFF)


def _topk_member(key_ref, k_top):
    rows, L = key_ref.shape
    kf = jnp.float32(k_top)

    def count(pred):
        return jnp.sum(pred.astype(F32), axis=-1, keepdims=True)

    def bit_step(it, res):
        cand = res | (jnp.int32(1) << (31 - it))
        cnt = count(key_ref[...] >= (cand ^ INT_MIN))
        return jnp.where(cnt >= kf, cand, res)

    res = lax.fori_loop(0, 32, bit_step, jnp.zeros((rows, 1), jnp.int32))
    thr = res ^ INT_MIN
    key = key_ref[...]
    gt = key > thr
    eq = key == thr
    need = kf - count(gt)
    pos = lax.broadcasted_iota(jnp.int32, (rows, L), 1)
    nbits = max(1, (L - 1).bit_length())

    def pos_step(it, lim):
        cand = lim | (jnp.int32(1) << (nbits - 1 - it))
        cnt = count((key_ref[...] == thr) & (pos < cand))
        return jnp.where(cnt < need, cand, lim)

    lim = lax.fori_loop(0, nbits, pos_step, jnp.zeros((rows, 1), jnp.int32))
    return gt | (eq & (pos <= lim))


def _idx_kernel(q_ref, k_ref, w_ref, mask_ref, acc_ref, key_ref, *, tq, n_heads, k_top):
    i = pl.program_id(1)
    S = k_ref.shape[0]

    def tile(ii):
        se = (ii + 1) * tq
        kb = k_ref[:se, :].astype(BF16)
        acc = acc_ref.at[:, :se]
        keys = key_ref.at[:, :se]
        acc[...] = jnp.zeros((tq, se), F32)

        def weighted(h):
            q = q_ref[:, pl.ds(pl.multiple_of(h * IDX_DIM, IDX_DIM), IDX_DIM)]
            s = lax.dot_general(q, kb, (((1,), (1,)), ((), ())), preferred_element_type=F32)
            wcol = pltpu.roll(w_ref[...], LANES - h, 1)[:, :1]
            return wcol * jnp.maximum(s, 0.0)

        def heads(hh, carry):
            part = weighted(hh * hpp)
            for e in range(1, hpp):
                part = part + weighted(hh * hpp + e)
            acc[...] += part
            return carry

        hpp = 2 if n_heads % 2 == 0 else 1
        lax.fori_loop(0, n_heads // hpp, heads, 0)
        t_pos = ii * tq + lax.broadcasted_iota(jnp.int32, (tq, se), 0)
        s_pos = lax.broadcasted_iota(jnp.int32, (tq, se), 1)
        causal = s_pos <= t_pos
        keys[...] = _order_key(jnp.where(causal, acc[...], -jnp.inf))
        member = _topk_member(keys, k_top)
        mask_ref[:, :se] = jnp.where(member & causal, 0.0, NEG_BIG)
        if se < S:
            mask_ref[:, se:] = jnp.full((tq, S - se), NEG_BIG, F32)

    for ii in range(S // tq):
        pl.when(i == ii)(functools.partial(tile, ii))


def dsa_select_prompt(q_idx, k_idx, w_idx, *, B, T, k_top, tq=256):
    n_heads = q_idx.shape[1] // IDX_DIM
    tq = min(tq, T)
    nq = T // tq
    return pl.pallas_call(
        functools.partial(_idx_kernel, tq=tq, n_heads=n_heads, k_top=k_top),
        grid=(B, nq),
        in_specs=[pl.BlockSpec((tq, q_idx.shape[1]), lambda b, i: (b * nq + i, 0)),
                  pl.BlockSpec((T, IDX_DIM), lambda b, i: (b, 0)),
                  pl.BlockSpec((tq, LANES), lambda b, i: (b * nq + i, 0))],
        out_specs=pl.BlockSpec((tq, T), lambda b, i: (b * nq + i, 0)),
        out_shape=jax.ShapeDtypeStruct((B * T, T), F32),
        scratch_shapes=[pltpu.VMEM((tq, T), F32), pltpu.VMEM((tq, T), jnp.int32)],
        compiler_params=_cparams(("parallel", "parallel")),
    )(q_idx, k_idx, w_idx)


def _bucket_thresholds(max_dist):
    exact = REL_BUCKETS // 2
    d = np.arange(max_dist + 1)
    large = exact + np.floor(np.log(np.maximum(d, 1) / exact) / math.log(REL_MAX_DIST / exact)
                             * (REL_BUCKETS - exact) + 1e-9).astype(np.int64)
    bucket = np.where(d < exact, d, np.minimum(large, REL_BUCKETS - 1))
    return [int(np.argmax(bucket >= k)) if (bucket >= k).any() else max_dist + 1 for k in range(1, REL_BUCKETS)]


def rel_bucket_of(dist, max_dist):
    out = jnp.zeros(dist.shape, jnp.int32)
    for thr in _bucket_thresholds(max_dist):
        out = out + (dist >= thr).astype(jnp.int32)
    return out


def _attn_kernel(q_ref, k_ref, v_ref, mask_ref, bkt_ref, tab_ref, o_ref, *, scale, hb):
    tq = q_ref.shape[0]
    S = k_ref.shape[0]
    Dh = ATTN_HEAD_DIM
    i = pl.program_id(1)

    def tile(ii):
        se = (ii + 1) * tq
        for e in range(hb):
            cols = slice(e * Dh, (e + 1) * Dh)
            q = q_ref[:, cols].astype(BF16)
            kb = k_ref[:se, cols].astype(BF16)
            logits = lax.dot_general(q, kb, (((1,), (1,)), ((), ())), preferred_element_type=F32) * scale
            tab = jnp.broadcast_to(tab_ref[e], (tq, LANES))
            bias = jnp.concatenate(
                [jnp.take_along_axis(tab, bkt_ref[:, c * LANES:(c + 1) * LANES], axis=1)
                 for c in range(se // LANES)], axis=1)
            logits = logits + bias + mask_ref[:, :se]
            m = jnp.max(logits, axis=-1, keepdims=True)
            p = jnp.exp(logits - m)
            l = jnp.sum(p, axis=-1, keepdims=True)
            o = jnp.dot(p.astype(BF16), v_ref[:se, cols].astype(BF16), preferred_element_type=F32)
            o_ref[:, cols] = (o / l).astype(o_ref.dtype)

    for ii in range(S // tq):
        pl.when(i == ii)(functools.partial(tile, ii))


def dsa_attend_prompt(q, k, v, mask, bucket, bias_tab, *, B, T, tq=256, hb=1):
    H = q.shape[1] // ATTN_HEAD_DIM
    tq = min(tq, T)
    nq = T // tq
    hb = math.gcd(H, hb)
    w = hb * ATTN_HEAD_DIM
    return pl.pallas_call(
        functools.partial(_attn_kernel, scale=ATTN_HEAD_DIM ** -0.5, hb=hb),
        grid=(B, nq, H // hb),
        in_specs=[pl.BlockSpec((tq, w), lambda b, i, h: (b * nq + i, h)),
                  pl.BlockSpec((T, w), lambda b, i, h: (b, h)),
                  pl.BlockSpec((T, w), lambda b, i, h: (b, h)),
                  pl.BlockSpec((tq, T), lambda b, i, h: (b * nq + i, 0)),
                  pl.BlockSpec((tq, T), lambda b, i, h: (i, 0)),
                  pl.BlockSpec((hb, 1, LANES), lambda b, i, h: (h, 0, 0))],
        out_specs=pl.BlockSpec((tq, w), lambda b, i, h: (b * nq + i, h)),
        out_shape=jax.ShapeDtypeStruct(q.shape, BF16),
        compiler_params=_cparams(("parallel", "parallel", "arbitrary")),
    )(q, k, v, mask, bucket, bias_tab)


def _page_score_kernel(pt_ref, q_ref, w_ref, *refs):
    *kidx_refs, sc_ref = refs
    page = kidx_refs[0].shape[0]
    for e, kidx_ref in enumerate(kidx_refs):
        kb = kidx_ref[...].astype(BF16)
        s = lax.dot_general(q_ref[...], kb, (((1,), (1,)), ((), ())), preferred_element_type=F32)
        sc_ref[:, e * page:(e + 1) * page] = jnp.sum(w_ref[...] * jnp.maximum(s, 0.0), axis=0, keepdims=True)


def dsa_page_scores(q_idx, w_idx, cache_kidx, page_table, j, pps=8):
    B, HI, _ = q_idx.shape
    page = cache_kidx.shape[2]
    n_pages = page_table.shape[1]
    pps = math.gcd(n_pages, pps)
    page_spec = lambda e: pl.BlockSpec((None, None, page, IDX_DIM), lambda b, p, pt: (j, pt[b, p * pps + e], 0, 0))
    grid_spec = pltpu.PrefetchScalarGridSpec(
        num_scalar_prefetch=1,
        grid=(B, n_pages // pps),
        in_specs=[pl.BlockSpec((None, HI, IDX_DIM), lambda b, p, pt: (b, 0, 0)),
                  pl.BlockSpec((None, HI, 1), lambda b, p, pt: (b, 0, 0))] + [page_spec(e) for e in range(pps)],
        out_specs=pl.BlockSpec((None, 1, pps * page), lambda b, p, pt: (b, 0, p)),
    )
    return pl.pallas_call(
        _page_score_kernel,
        grid_spec=grid_spec,
        out_shape=jax.ShapeDtypeStruct((B, 1, n_pages * page), F32),
        compiler_params=_cparams(("parallel", "arbitrary")),
    )(page_table, q_idx, w_idx, *([cache_kidx] * pps))


def _sample_select_kernel(sc_ref, q_ref, w_ref, knew_ref, mask_ref, key_ref, *, past, k_top):
    B = sc_ref.shape[0]
    key_ref[:, :past] = _order_key(sc_ref[...])
    lane = lax.broadcasted_iota(jnp.int32, (1, LANES), 1)
    for b in range(B):
        kn = knew_ref[b].astype(BF16).astype(F32)
        s = jnp.sum(q_ref[b].astype(F32) * kn, axis=-1, keepdims=True)
        sc_new = jnp.sum(w_ref[b] * jnp.maximum(s, 0.0), axis=0, keepdims=True)
        tail = jnp.where(lane == 0, sc_new, -jnp.inf)
        key_ref[b:b + 1, past:] = _order_key(tail)
    member = _topk_member(key_ref, k_top)
    mask_ref[...] = jnp.where(member, 0.0, NEG_BIG)


def dsa_select_sample(scores, q_idx, w_idx, k_idx_new, *, k_top):
    B, past = scores.shape
    L = past + LANES
    return pl.pallas_call(
        functools.partial(_sample_select_kernel, past=past, k_top=k_top),
        out_shape=jax.ShapeDtypeStruct((B, L), F32),
        scratch_shapes=[pltpu.VMEM((B, L), jnp.int32)],
        compiler_params=pltpu.CompilerParams(vmem_limit_bytes=VMEM_LIMIT_BYTES),
    )(scores, q_idx, w_idx, k_idx_new)


def _page_attn_kernel(pt_ref, q_ref, k_ref, v_ref, bias_ref, mask_ref, knew_ref, vnew_ref, bnew_ref, mnew_ref,
                      o_ref, m_ref, l_ref, acc_ref, *, scale, H):
    p_idx = pl.program_id(1)

    @pl.when(p_idx == 0)
    def _():
        m_ref[...] = jnp.full_like(m_ref, NEG_BIG)
        l_ref[...] = jnp.zeros_like(l_ref)
        acc_ref[...] = jnp.zeros_like(acc_ref)

    q = q_ref[...].astype(BF16)
    cols = k_ref.shape[0]
    la = lax.dot_general(q, k_ref[...].astype(BF16), (((1,), (1,)), ((), ())), preferred_element_type=F32)
    la = la * scale + bias_ref[...]
    row_h = lax.broadcasted_iota(jnp.int32, (H, cols), 0)
    col_h = lax.broadcasted_iota(jnp.int32, (H, cols), 1) % H
    valid = (row_h == col_h) & (mask_ref[...] == 0.0)
    la = jnp.where(valid, la, NEG_BIG)
    m_old = m_ref[...]
    m_new = jnp.maximum(m_old, jnp.max(la, axis=-1, keepdims=True))
    alpha = jnp.exp(m_old - m_new)
    p = jnp.where(valid, jnp.exp(la - m_new), 0.0)
    l_ref[...] = alpha * l_ref[...] + jnp.sum(p, axis=-1, keepdims=True)
    acc_ref[...] = alpha * acc_ref[...] + jnp.dot(p.astype(BF16), v_ref[...].astype(BF16), preferred_element_type=F32)
    m_ref[...] = m_new

    @pl.when(p_idx == pl.num_programs(1) - 1)
    def _():
        kn = knew_ref[...].astype(BF16).astype(F32)
        s_new = jnp.sum(q.astype(F32) * kn, axis=-1, keepdims=True) * scale + bnew_ref[...]
        ok = mnew_ref[...][:, :1] == 0.0
        s_new = jnp.where(ok, s_new, NEG_BIG)
        m_o = m_ref[...]
        m_f = jnp.maximum(m_o, s_new)
        a_f = jnp.exp(m_o - m_f)
        p_new = jnp.where(ok, jnp.exp(s_new - m_f), 0.0)
        l_f = a_f * l_ref[...] + p_new
        acc = a_f * acc_ref[...] + p_new.astype(BF16).astype(F32) * vnew_ref[...].astype(BF16).astype(F32)
        o_ref[...] = acc / l_f


def dsa_attend_sample(q, cache_k, cache_v, page_table, j, bias_rows, mask_rows, k_new, v_new, bias_new, mask_new):
    B, H, Dh = q.shape
    cols = cache_k.shape[2]
    n_pages = page_table.shape[1]
    per_b = lambda *s: pl.BlockSpec((None,) + s, lambda b, p, pt: (b,) + (0,) * len(s))
    grid_spec = pltpu.PrefetchScalarGridSpec(
        num_scalar_prefetch=1,
        grid=(B, n_pages),
        in_specs=[per_b(H, Dh),
                  pl.BlockSpec((None, None, cols, Dh), lambda b, p, pt: (j, pt[b, p], 0, 0)),
                  pl.BlockSpec((None, None, cols, Dh), lambda b, p, pt: (j, pt[b, p], 0, 0)),
                  pl.BlockSpec((None, 1, cols), lambda b, p, pt: (p, 0, 0)),
                  pl.BlockSpec((None, None, 1, cols), lambda b, p, pt: (b, p, 0, 0)),
                  per_b(H, Dh), per_b(H, Dh),
                  pl.BlockSpec((H, 1), lambda b, p, pt: (0, 0)),
                  per_b(1, LANES)],
        out_specs=per_b(H, Dh),
        scratch_shapes=[pltpu.VMEM((H, 1), F32), pltpu.VMEM((H, 1), F32), pltpu.VMEM((H, Dh), F32)],
    )
    return pl.pallas_call(
        functools.partial(_page_attn_kernel, scale=Dh ** -0.5, H=H),
        grid_spec=grid_spec,
        out_shape=jax.ShapeDtypeStruct((B, H, Dh), F32),
        compiler_params=_cparams(("parallel", "arbitrary")),
    )(page_table, q, cache_k, cache_v, bias_rows, mask_rows, k_new, v_new, bias_new, mask_new)


def _rwkv_layer(h, B, T, shift0, wkv0, j, p):
    M, D = h.shape
    N = RWKV_HEAD_DIM
    H = D // N
    h3 = h.reshape(B, T, D)
    prev = jnp.concatenate([shift0[:, None].astype(F32), h3[:, :-1]], axis=1)
    d = prev - h3
    mu = p["rwkv_mu"][j]
    mix = [(h3 + d * mu[i]).astype(BF16).reshape(M, D) for i in range(6)]
    seq = lambda z: z.reshape(B, T, D)
    r = matmul(mix[0], p["rwkv_w_rkv"], (j, 0))
    k = matmul(mix[2], p["rwkv_w_rkv"], (j, 1))
    v = matmul(mix[3], p["rwkv_w_rkv"], (j, 2))
    wl = matmul(matmul(mix[1], p["rwkv_w1"], (j,), act="tanh", out_dtype=BF16), p["rwkv_w2"], (j,))
    al = matmul(matmul(mix[4], p["rwkv_a1"], (j,), out_dtype=BF16), p["rwkv_a2"], (j,))
    g = matmul(matmul(mix[5], p["rwkv_g1"], (j,), act="sigmoid", out_dtype=BF16), p["rwkv_g2"], (j,))
    nb = min(B, max(1, LANES // H))
    vec = lambda z: jnp.tile(z.astype(F32).reshape(H, N).T, (1, nb))
    par = jnp.stack([vec(p[q][j]) for q in ("rwkv_w0", "rwkv_a0", "rwkv_k_k", "rwkv_k_a", "rwkv_r_k",
                                            "rwkv_gn_g", "rwkv_gn_b")])
    if wkv0 is None:
        s0 = jnp.zeros((N, N // SUBLANES, SUBLANES, B * H), F32)
    else:
        s0 = jnp.transpose(wkv0.astype(F32), (2, 3, 0, 1)).reshape(N, N // SUBLANES, SUBLANES, B * H)
    z, s_last = wkv_scan(seq(r), seq(wl), seq(k), seq(v), seq(al), seq(g), par, s0, tc=math.gcd(T, 16), nb=nb)
    wkv_new = jnp.transpose(s_last.reshape(N, N, B, H), (2, 3, 0, 1))
    out = matmul(z.reshape(M, D), p["rwkv_w_o"], (j,))
    return out, h3[:, -1], wkv_new


def _s5_layer(hb, B, T, x0_re, x0_im, j, p):
    M, D = hb.shape
    I = S5_GROUP
    G = D // I
    P = p["s5_lambda_re"].shape[-1]
    C = S5_SUB if T % S5_SUB == 0 else 1
    nc = T // C
    bp = -(-B // SUBLANES) * SUBLANES
    u = matmul(hb, p["s5_w_in"], (j,))
    folded = _s5_fold(p["s5_lambda_re"][j], p["s5_lambda_im"][j], p["s5_log_dt"][j], p["s5_b_re"][j],
                      p["s5_b_im"][j], p["s5_c_re"][j], p["s5_c_im"][j], C)
    gl = LANES // I
    if C > 1 and G % gl == 0:
        NG = G // gl
        if x0_re is None:
            x0 = jnp.zeros((NG, B, 2 * gl * P), F32)
        else:
            st = lambda z: jnp.transpose(z.astype(F32).reshape(B, NG, gl * P), (1, 0, 2))
            x0 = jnp.concatenate([st(x0_re), st(x0_im)], axis=-1)
        z, xT = s5_natural(u, p["s5_d"][j].astype(F32).reshape(1, D), _s5_blockdiag(folded, gl), x0, B=B, C=C)
        fin = lambda q: jnp.transpose(q.reshape(NG, B, gl, P), (1, 0, 2, 3)).reshape(B, G, P)
        hr, hi = fin(xT[..., :gl * P]), fin(xT[..., gl * P:])
    else:
        ug = jnp.transpose(u.reshape(B, nc, C, G, I), (3, 1, 0, 2, 4))
        ug = jnp.pad(ug, ((0, 0), (0, 0), (0, bp - B), (0, 0), (0, 0))).reshape(G, nc * bp, C * I)
        if x0_re is None:
            x0r = x0i = jnp.zeros((G, bp, P), F32)
        else:
            st = lambda z: jnp.pad(jnp.transpose(z.astype(F32), (1, 0, 2)), ((0, 0), (0, bp - B), (0, 0)))
            x0r, x0i = st(x0_re), st(x0_im)
        yg, xr, xi = s5_scan(ug, _s5_pergroup(folded), x0r, x0i, gb=8)
        y = jnp.transpose(yg.reshape(G, nc, bp, C, I)[:, :, :B], (2, 1, 3, 0, 4)).reshape(M, D)
        z = jax.nn.gelu(y + p["s5_d"][j].astype(F32) * u).astype(BF16)
        fin = lambda q: jnp.transpose(q[:, :B], (1, 0, 2))
        hr, hi = fin(xr), fin(xi)
    ab = matmul(z, p["s5_w_glu"], (j,))
    glu = (ab[:, :D] * jax.nn.sigmoid(ab[:, D:])).astype(BF16)
    out = matmul(glu, p["s5_w_out"], (j,))
    return out, hr, hi


def _dsa_project(hb, j, p):
    D = hb.shape[1]
    q = matmul(hb, p["attn_w_qkv"], (j,), n_off=0, n_size=D)
    k = matmul(hb, p["attn_w_qkv"], (j,), n_off=D, n_size=D)
    v = matmul(hb, p["attn_w_qkv"], (j,), n_off=2 * D, n_size=D)
    q_idx = matmul(hb, p["idx_w_q"], (j,), out_dtype=BF16)
    k_idx = matmul(hb, p["idx_w_k"], (j,))
    n_ih = p["idx_w_w"].shape[-1]
    w_idx = matmul(hb, p["idx_w_w"], (j,)) * ((n_ih * IDX_DIM) ** -0.5)
    return q, k, v, q_idx, k_idx, w_idx


def _dsa_prompt_layer(hb, B, T, j, p):
    M, D = hb.shape
    H = D // ATTN_HEAD_DIM
    q, k, v, q_idx, k_idx, w_idx = _dsa_project(hb, j, p)
    k_top = max(1, min(TOPK_MAX, T // 4))
    w_pad = jnp.pad(w_idx, ((0, 0), (0, LANES - w_idx.shape[1])))
    mask = dsa_select_prompt(q_idx, k_idx, w_pad, B=B, T=T, k_top=k_top)
    pos = jnp.arange(T, dtype=jnp.int32)
    bucket = rel_bucket_of(pos[:, None] - pos[None, :], T)
    tab = jnp.pad(p["rel_bias"].astype(F32).T, ((0, 0), (0, LANES - REL_BUCKETS))).reshape(H, 1, LANES)
    o = dsa_attend_prompt(q, k, v, mask, bucket, tab, B=B, T=T)
    out = matmul(o, p["attn_w_o"], (j,))
    return out, k.reshape(B, T, H, ATTN_HEAD_DIM), v.reshape(B, T, H, ATTN_HEAD_DIM), k_idx.reshape(B, T, IDX_DIM)


def _dsa_sample_layer(hb, B, j, p, cache_k, cache_v, cache_kidx, page_table):
    M, D = hb.shape
    H = D // ATTN_HEAD_DIM
    q, k, v, q_idx, k_idx, w_idx = _dsa_project(hb, j, p)
    page = cache_kidx.shape[2]
    n_pages = page_table.shape[1]
    past = n_pages * page
    n_ih = w_idx.shape[1]
    q_idx3 = q_idx.reshape(B, n_ih, IDX_DIM)
    w_idx3 = w_idx.reshape(B, n_ih, 1)
    scores = dsa_page_scores(q_idx3, w_idx3, cache_kidx, page_table, j).reshape(B, past)
    k_top = max(1, min(TOPK_MAX, (past + 1) // 4))
    mask = dsa_select_sample(scores, q_idx3, w_idx3, k_idx.reshape(B, 1, IDX_DIM), k_top=k_top)
    dist = past - jnp.arange(past, dtype=jnp.int32)
    bias_rows = p["rel_bias"].astype(F32)[rel_bucket_of(dist, past)]
    bias_rows = bias_rows.reshape(n_pages, 1, page * H)
    mask_rows = jnp.repeat(mask[:, :past], H, axis=1).reshape(B, n_pages, 1, page * H)
    pool = cache_k.shape[1]
    ck = cache_k.reshape(cache_k.shape[0], pool, page * H, ATTN_HEAD_DIM)
    cv = cache_v.reshape(cache_v.shape[0], pool, page * H, ATTN_HEAD_DIM)
    hd = lambda z: z.reshape(B, H, ATTN_HEAD_DIM)
    bias_new = p["rel_bias"].astype(F32)[0].reshape(H, 1)
    mask_new = jnp.broadcast_to(mask[:, past:past + 1], (B, LANES)).reshape(B, 1, LANES)
    o = dsa_attend_sample(hd(q), ck, cv, page_table, j, bias_rows, mask_rows, hd(k), hd(v), bias_new, mask_new)
    out = matmul(o.reshape(M, D).astype(BF16), p["attn_w_o"], (j,))
    return out, k.reshape(B, 1, H, ATTN_HEAD_DIM), v.reshape(B, 1, H, ATTN_HEAD_DIM), k_idx.reshape(B, 1, IDX_DIM)


def _run(x, mods, sample, p, caches, states):
    B, T, D = x.shape
    M = B * T
    depth = len(mods)
    alpha = (2.0 * depth) ** 0.25
    modulate = lambda z, sc, sh: (z.reshape(B, T, D) * (1.0 + sc)[:, None] + sh[:, None]).reshape(M, D)
    xf = x.reshape(M, D).astype(F32)
    h = modulate(xf, mods[0][1], mods[0][0])
    hb = h.astype(BF16)
    st = ([], [], [])
    for i in range(depth):
        sh1, sc1, g1, sh2, sc2, g2 = mods[i]
        kind, j = i % 3, i // 3
        if kind == 0:
            if i > 0:
                h = modulate(xf, sc1, sh1)
            s0, w0 = (states["shift"][j], states["wkv"][j]) if sample else (jnp.zeros((B, D), F32), None)
            out, shift_new, wkv_new = _rwkv_layer(h, B, T, s0, w0, j, p)
            st[0].append((wkv_new, shift_new))
        elif kind == 1:
            x0r, x0i = (states["s5_re"][j], states["s5_im"][j]) if sample else (None, None)
            out, hr, hi = _s5_layer(hb, B, T, x0r, x0i, j, p)
            st[1].append((hr, hi))
        else:
            if sample:
                out, kn, vn, kin = _dsa_sample_layer(hb, B, j, p, *caches)
            else:
                out, kn, vn, kin = _dsa_prompt_layer(hb, B, T, j, p)
            st[2].append((kn, vn, kin))
        xf, hb = residual_ln(xf, out, 1.0 + g1, p["ln_g"][i, 0], p["ln_b"][i, 0], 1.0 + sc2, sh2,
                             rows_per_batch=T, alpha=alpha)
        f = matmul(matmul(hb, p["mlp_w1"], (i,), act="relu2", out_dtype=BF16), p["mlp_w2"], (i,), tk=2048, tn=1024)
        if i + 1 < depth:
            nsc, nsh = 1.0 + mods[i + 1][1], mods[i + 1][0]
        else:
            nsc, nsh = jnp.ones_like(sc1), jnp.zeros_like(sh1)
        xf, hb = residual_ln(xf, f, 1.0 + g2, p["ln_g"][i, 1], p["ln_b"][i, 1], nsc, nsh,
                             rows_per_batch=T, alpha=alpha)
    stk = lambda kind, k: jnp.stack([s[k] for s in st[kind]])
    return (xf.reshape(B, T, D), stk(0, 0), stk(0, 1), stk(1, 0), stk(1, 1), stk(2, 0), stk(2, 1), stk(2, 2))


def kernel(x_prompt, x_sample, cache_k, cache_v, cache_kidx, state_wkv, state_shift, state_s5_re, state_s5_im,
           page_table, c_prompt, c_sample, ada_w, ada_b, ln_g, ln_b, mlp_w1, mlp_w2,
           rwkv_mu, rwkv_w_rkv, rwkv_w_o, rwkv_w0, rwkv_w1, rwkv_w2, rwkv_a0, rwkv_a1, rwkv_a2,
           rwkv_g1, rwkv_g2, rwkv_k_k, rwkv_k_a, rwkv_r_k, rwkv_gn_g, rwkv_gn_b,
           s5_w_in, s5_lambda_re, s5_lambda_im, s5_log_dt, s5_b_re, s5_b_im, s5_c_re, s5_c_im, s5_d,
           s5_w_glu, s5_w_out, attn_w_qkv, attn_w_o, idx_w_q, idx_w_k, idx_w_w, rel_bias):
    p = dict(ln_g=ln_g, ln_b=ln_b, mlp_w1=mlp_w1, mlp_w2=mlp_w2, rwkv_mu=rwkv_mu, rwkv_w_rkv=rwkv_w_rkv,
             rwkv_w_o=rwkv_w_o, rwkv_w0=rwkv_w0, rwkv_w1=rwkv_w1, rwkv_w2=rwkv_w2, rwkv_a0=rwkv_a0,
             rwkv_a1=rwkv_a1, rwkv_a2=rwkv_a2, rwkv_g1=rwkv_g1, rwkv_g2=rwkv_g2, rwkv_k_k=rwkv_k_k,
             rwkv_k_a=rwkv_k_a, rwkv_r_k=rwkv_r_k, rwkv_gn_g=rwkv_gn_g, rwkv_gn_b=rwkv_gn_b,
             s5_w_in=s5_w_in, s5_lambda_re=s5_lambda_re, s5_lambda_im=s5_lambda_im, s5_log_dt=s5_log_dt,
             s5_b_re=s5_b_re, s5_b_im=s5_b_im, s5_c_re=s5_c_re, s5_c_im=s5_c_im, s5_d=s5_d,
             s5_w_glu=s5_w_glu, s5_w_out=s5_w_out, attn_w_qkv=attn_w_qkv, attn_w_o=attn_w_o,
             idx_w_q=idx_w_q, idx_w_k=idx_w_k, idx_w_w=idx_w_w, rel_bias=rel_bias)
    depth = ada_w.shape[0]
    Bp, Bs = c_prompt.shape[0], c_sample.shape[0]
    c_all = jax.nn.silu(jnp.concatenate([c_prompt, c_sample], axis=0).astype(F32))
    pad = -(-c_all.shape[0] // 16) * 16 - c_all.shape[0]
    c_all = jnp.pad(c_all, ((0, pad), (0, 0)))
    mods_p, mods_s = [], []
    for i in range(depth):
        mod = matmul(c_all, ada_w, (i,)) + ada_b[i]
        six = jnp.split(mod, 6, axis=-1)
        mods_p.append([z[:Bp] for z in six])
        mods_s.append([z[Bp:Bp + Bs] for z in six])
    states = dict(wkv=state_wkv, shift=state_shift, s5_re=state_s5_re, s5_im=state_s5_im)
    caches = (cache_k, cache_v, cache_kidx, page_table)
    out_p = _run(x_prompt, mods_p, False, p, caches, states)
    out_s = _run(x_sample, mods_s, True, p, caches, states)
    return (out_p[0], out_s[0]) + out_p[1:] + out_s[1:]
```

```python
import functools
import math

import jax
import jax.numpy as jnp
import numpy as np
from jax import lax
from jax.experimental import pallas as pl
from jax.experimental.pallas import tpu as pltpu

F32 = jnp.float32
BF16 = jnp.bfloat16

LANES = 128
SUBLANES = 8
VMEM_LIMIT_BYTES = 56 * 1024 * 1024

LN_EPS = 1e-5
GN_EPS = 64e-5
RWKV_HEAD_DIM = 64
S5_GROUP = 16
S5_SUB = 16
ATTN_HEAD_DIM = 128
IDX_DIM = 128
TOPK_MAX = 256
REL_BUCKETS = 32
REL_MAX_DIST = 1024
NEG_BIG = -1e30


def _cparams(sem):
    return pltpu.CompilerParams(dimension_semantics=sem, vmem_limit_bytes=VMEM_LIMIT_BYTES)


def _act(p, act):
    if act is None:
        return p
    if act == "relu2":
        r = jnp.maximum(p, 0.0)
        return r * r
    if act == "tanh":
        return jnp.tanh(p)
    if act == "sigmoid":
        return jax.nn.sigmoid(p)
    raise ValueError(act)


def _mm_kernel(a_ref, w_ref, o_ref, *scratch, nk, act):
    p = jnp.dot(a_ref[...].astype(BF16), w_ref[...].astype(BF16), preferred_element_type=F32)
    if nk == 1:
        o_ref[...] = _act(p, act).astype(o_ref.dtype)
        return
    acc_ref, = scratch
    k = pl.program_id(2)

    @pl.when(k == 0)
    def _():
        acc_ref[...] = p

    @pl.when(k > 0)
    def _():
        acc_ref[...] += p

    @pl.when(k == nk - 1)
    def _():
        o_ref[...] = _act(acc_ref[...], act).astype(o_ref.dtype)


def _pick(n, pref):
    if n <= pref:
        return n
    t = (pref // LANES) * LANES
    while t >= LANES:
        if n % t == 0:
            return t
        t -= LANES
    return n


def matmul(a, w, widx=(), *, act=None, out_dtype=F32, n_off=0, n_size=None, tm=1024, tn=512, tk=4096):
    M, K = a.shape
    Kw, N = w.shape[-2:]
    assert K == Kw and len(widx) == w.ndim - 2
    n_size = N if n_size is None else n_size
    tm = _pick(M, tm) if M % SUBLANES == 0 else M
    tn = _pick(n_size, tn)
    tk = _pick(K, tk)
    assert M % tm == 0 and n_size % tn == 0 and K % tk == 0 and n_off % tn == 0
    nk = K // tk
    joff = n_off // tn
    lead = tuple(widx)
    w_spec = pl.BlockSpec((None,) * len(lead) + (tk, tn), lambda i, j, k: lead + (k, j + joff))
    return pl.pallas_call(
        functools.partial(_mm_kernel, nk=nk, act=act),
        grid=(M // tm, n_size // tn, nk),
        in_specs=[pl.BlockSpec((tm, tk), lambda i, j, k: (i, k)), w_spec],
        out_specs=pl.BlockSpec((tm, tn), lambda i, j, k: (i, j)),
        out_shape=jax.ShapeDtypeStruct((M, n_size), out_dtype),
        scratch_shapes=[pltpu.VMEM((tm, tn), F32)] if nk > 1 else [],
        compiler_params=_cparams(("parallel", "parallel", "arbitrary")),
    )(a, w)


def _allsum_sublanes(p):
    p = p + pltpu.roll(p, 4, 0)
    p = p + pltpu.roll(p, 2, 0)
    return p + pltpu.roll(p, 1, 0)


def _wkv_kernel(r_ref, wl_ref, k_ref, v_ref, al_ref, g_ref, par_ref, s0_ref, z_ref, sT_ref,
                s_ref, op_ref, vt_ref, yt_ref, sc_ref, in_ref, zs_ref, *, tc, n, nb, nh):
    c = pl.program_id(1)
    n8 = n // SUBLANES
    lw = nb * nh

    @pl.when(c == 0)
    def _():
        s_ref[...] = s0_ref[...]

    for o, ref in enumerate((r_ref, wl_ref, k_ref, v_ref, al_ref, g_ref)):
        for q in range(nb):
            in_ref[o, q] = ref[q].reshape(tc, nh, n)
    R_, WL_, K_, V_, AL_, G_ = range(6)

    w0, a0, k_k, k_a, r_k, gn_g, gn_b = (par_ref[q] for q in range(7))
    cat = lambda o, t: in_ref[o, :, t].reshape(lw, n).T
    fold = lambda x: x.reshape(n8, SUBLANES, lw)
    allsum = lambda x: _allsum_sublanes(jnp.sum(fold(x), axis=0))

    def prep(t, carry):
        r = cat(R_, t)
        k = cat(K_, t)
        w_log = -jax.nn.softplus(-(w0 + cat(WL_, t))) - 0.5
        w = jnp.exp(-jnp.exp(w_log))
        a = jax.nn.sigmoid(a0 + cat(AL_, t))
        kk = k * k_k
        nrm = jnp.maximum(jnp.sqrt(allsum(kk * kk)), 1e-12)
        kk3 = fold(kk) / nrm[None]
        kh = k * (1.0 + (a - 1.0) * k_a)
        b3 = kk3 * fold(a)
        op_ref[t, 0] = -kk3
        op_ref[t, 1] = fold(w * r)
        op_ref[t, 2] = fold(w)
        op_ref[t, 3] = b3
        op_ref[t, 4] = fold(kh)
        vt_ref[t] = cat(V_, t)
        sc_ref[t, 0] = _allsum_sublanes(jnp.sum(b3 * fold(r), axis=0))
        sc_ref[t, 1] = allsum(kh * r)
        sc_ref[t, 2] = allsum(r * kh * r_k)
        return carry

    lax.fori_loop(0, tc, prep, 0, unroll=min(tc, 4))

    def step(t, carry):
        br = sc_ref[t, 0, 0:1]
        kr = sc_ref[t, 1, 0:1]

        def row(i, carry2):
            si = s_ref[i]
            sa = _allsum_sublanes(jnp.sum(si * op_ref[t, 0], axis=0))
            y0 = _allsum_sublanes(jnp.sum(si * op_ref[t, 1], axis=0))
            vi = vt_ref[t, pl.ds(i, 1), :]
            s_ref[i] = si * op_ref[t, 2] + sa[None] * op_ref[t, 3] + vi[None] * op_ref[t, 4]
            yt_ref[t, pl.ds(i, 1), :] = y0[0:1] + sa[0:1] * br + vi * kr
            return carry2

        lax.fori_loop(0, n, row, 0, unroll=16)
        return carry

    lax.fori_loop(0, tc, step, 0)

    def post(t, carry):
        y = yt_ref[t]
        mu = allsum(y) * (1.0 / n)
        yc = fold(y) - mu[None]
        var = _allsum_sublanes(jnp.sum(yc * yc, axis=0)) * (1.0 / n)
        yn = (yc * lax.rsqrt(var + GN_EPS)[None]).reshape(n, lw) * gn_g + gn_b
        z = yn + (fold(vt_ref[t]) * sc_ref[t, 2][None]).reshape(n, lw)
        zt = z.T * in_ref[G_, :, t].reshape(lw, n)
        for q in range(nb):
            zs_ref[q, t] = zt[q * nh:(q + 1) * nh]
        return carry

    lax.fori_loop(0, tc, post, 0, unroll=min(tc, 4))
    for q in range(nb):
        z_ref[q] = zs_ref[q].reshape(tc, nh * n).astype(z_ref.dtype)

    @pl.when(c == pl.num_programs(1) - 1)
    def _():
        sT_ref[...] = s_ref[...]


def wkv_scan(r, wl, k, v, al, g, par, s0, *, tc, nb):
    B, T, D = r.shape
    n = par.shape[1]
    nh = D // n
    n8 = n // SUBLANES
    lw = nb * nh
    assert T % tc == 0 and B % nb == 0
    op = pl.BlockSpec((nb, tc, D), lambda l, c: (l, c, 0))
    st = pl.BlockSpec((n, n8, SUBLANES, lw), lambda l, c: (0, 0, 0, l))
    return pl.pallas_call(
        functools.partial(_wkv_kernel, tc=tc, n=n, nb=nb, nh=nh),
        grid=(B // nb, T // tc),
        in_specs=[op, op, op, op, op, op, pl.BlockSpec((7, n, lw), lambda l, c: (0, 0, 0)), st],
        out_specs=[op, st],
        out_shape=[jax.ShapeDtypeStruct((B, T, D), BF16), jax.ShapeDtypeStruct(s0.shape, F32)],
        scratch_shapes=[pltpu.VMEM((n, n8, SUBLANES, lw), F32), pltpu.VMEM((tc, 5, n8, SUBLANES, lw), F32),
                        pltpu.VMEM((tc, n, lw), F32), pltpu.VMEM((tc, n, lw), F32),
                        pltpu.VMEM((tc, 3, SUBLANES, lw), F32),
                        pltpu.VMEM((6, nb, tc, nh, n), F32), pltpu.VMEM((nb, tc, nh, n), F32)],
        compiler_params=_cparams(("parallel", "arbitrary")),
    )(r, wl, k, v, al, g, par, s0)


_HI = lax.Precision.HIGHEST


def _s5_fold(lam_re, lam_im, log_dt, b_re, b_im, c_re, c_im, C):
    G, P, I = b_re.shape
    dt = jnp.exp(log_dt.astype(F32))[:, None]
    lr, li = lam_re.astype(F32), lam_im.astype(F32)
    mag = jnp.exp(lr * dt)
    abar_re, abar_im = mag * jnp.cos(li * dt), mag * jnp.sin(li * dt)
    den = lr * lr + li * li
    nr, ni = abar_re - 1.0, abar_im
    coef_re = (nr * lr + ni * li) / den
    coef_im = (ni * lr - nr * li) / den
    br_, bi_ = b_re.astype(F32), b_im.astype(F32)
    bb_re = coef_re[..., None] * br_ - coef_im[..., None] * bi_
    bb_im = coef_re[..., None] * bi_ + coef_im[..., None] * br_
    cr, ci = c_re.astype(F32), c_im.astype(F32)
    pw_re, pw_im = [jnp.ones_like(abar_re)], [jnp.zeros_like(abar_im)]
    for _ in range(C):
        pr, pi = pw_re[-1], pw_im[-1]
        pw_re.append(pr * abar_re - pi * abar_im)
        pw_im.append(pr * abar_im + pi * abar_re)
    a_re, a_im = jnp.stack(pw_re, 1), jnp.stack(pw_im, 1)
    ab_re = a_re[:, :C, :, None] * bb_re[:, None] - a_im[:, :C, :, None] * bb_im[:, None]
    ab_im = a_re[:, :C, :, None] * bb_im[:, None] + a_im[:, :C, :, None] * bb_re[:, None]
    kern = (jnp.einsum("gjp,gtpi->gtij", cr, ab_re, precision=_HI)
            - jnp.einsum("gjp,gtpi->gtij", ci, ab_im, precision=_HI))
    ar1, ai1 = a_re[:, 1:], a_im[:, 1:]
    v_re = cr[:, None] * ar1[:, :, None, :] - ci[:, None] * ai1[:, :, None, :]
    v_im = -cr[:, None] * ai1[:, :, None, :] - ci[:, None] * ar1[:, :, None, :]
    v_re = jnp.transpose(v_re, (0, 3, 1, 2)).reshape(G, P, C * I)
    v_im = jnp.transpose(v_im, (0, 3, 1, 2)).reshape(G, P, C * I)
    return kern, ab_re, ab_im, v_re, v_im, a_re[:, C][:, None], a_im[:, C][:, None]


def _s5_pergroup(folded):
    kern, ab_re, ab_im, v_re, v_im, a_re, a_im = folded
    G, C, P, I = ab_re.shape
    tt = np.arange(C)[None, :] - np.arange(C)[:, None]
    m = kern[:, np.clip(tt, 0, C - 1)]
    m = jnp.where((tt >= 0)[None, :, :, None, None], m, 0.0)
    m = jnp.transpose(m, (0, 1, 3, 2, 4)).reshape(G, C * I, C * I)
    rev = np.arange(C - 1, -1, -1)
    w_re = jnp.transpose(ab_re[:, rev], (0, 1, 3, 2)).reshape(G, C * I, P)
    w_im = jnp.transpose(ab_im[:, rev], (0, 1, 3, 2)).reshape(G, C * I, P)
    return m, w_re, w_im, v_re, v_im, a_re, a_im


def _s5_kernel(u_ref, m_ref, wre_ref, wim_ref, vre_ref, vim_ref, are_ref, aim_ref, x0re_ref, x0im_ref,
               y_ref, xre_ref, xim_ref, bur_ref, bui_ref, *, gb, nc, bp):
    for g in range(gb):
        u = u_ref[g]
        bur_ref[g] = jnp.dot(u, wre_ref[g], precision=_HI, preferred_element_type=F32)
        bui_ref[g] = jnp.dot(u, wim_ref[g], precision=_HI, preferred_element_type=F32)
    ar = are_ref[...]
    ai = aim_ref[...]

    def chunk(c, carry):
        xr, xi = carry
        rows = pl.ds(pl.multiple_of(c * bp, bp), bp)
        br = bur_ref[:, rows, :]
        bi = bui_ref[:, rows, :]
        bur_ref[:, rows, :] = xr
        bui_ref[:, rows, :] = xi
        return ar * xr - ai * xi + br, ar * xi + ai * xr + bi

    xr, xi = lax.fori_loop(0, nc, chunk, (x0re_ref[...], x0im_ref[...]))
    xre_ref[...] = xr
    xim_ref[...] = xi
    for g in range(gb):
        y = jnp.dot(u_ref[g], m_ref[g], precision=_HI, preferred_element_type=F32)
        y += jnp.dot(bur_ref[g], vre_ref[g], precision=_HI, preferred_element_type=F32)
        y += jnp.dot(bui_ref[g], vim_ref[g], precision=_HI, preferred_element_type=F32)
        y_ref[g] = y


def s5_scan(u, folded, x0_re, x0_im, *, gb):
    m, w_re, w_im, v_re, v_im, a_re, a_im = folded
    G, R, CI = u.shape
    P = w_re.shape[-1]
    bp = x0_re.shape[1]
    nc = R // bp
    assert G % gb == 0 and bp % SUBLANES == 0
    blk = lambda *s: pl.BlockSpec((gb,) + s, lambda g: (g,) + (0,) * len(s))
    return pl.pallas_call(
        functools.partial(_s5_kernel, gb=gb, nc=nc, bp=bp),
        grid=(G // gb,),
        in_specs=[blk(R, CI), blk(CI, CI), blk(CI, P), blk(CI, P), blk(P, CI), blk(P, CI),
                  blk(1, P), blk(1, P), blk(bp, P), blk(bp, P)],
        out_specs=[blk(R, CI), blk(bp, P), blk(bp, P)],
        out_shape=[jax.ShapeDtypeStruct((G, R, CI), F32), jax.ShapeDtypeStruct((G, bp, P), F32),
                   jax.ShapeDtypeStruct((G, bp, P), F32)],
        scratch_shapes=[pltpu.VMEM((gb, R, P), F32), pltpu.VMEM((gb, R, P), F32)],
        compiler_params=_cparams(("parallel",)),
    )(u, m, w_re, w_im, v_re, v_im, a_re, a_im, x0_re, x0_im)


def _s5_blockdiag(folded, gl):
    kern, ab_re, ab_im, v_re, v_im, a_re, a_im = folded
    G, C, P, I = ab_re.shape
    NG = G // gl
    kern2 = jnp.transpose(kern.reshape(NG, gl, C, I, I), (0, 1, 3, 2, 4)).reshape(NG, gl * I, C * I)
    rev = np.arange(C - 1, -1, -1)
    ab = jnp.stack([ab_re[:, rev], ab_im[:, rev]], axis=2)
    w2 = jnp.transpose(ab.reshape(NG, gl, C, 2, P, I), (0, 1, 5, 2, 3, 4)).reshape(NG, gl * I, C * 2 * P)
    v2 = jnp.stack([v_re, v_im], axis=0).reshape(2, NG, gl * P, C * I)
    v2 = jnp.transpose(v2, (1, 0, 2, 3)).reshape(NG, 2 * gl * P, C * I)
    r_tau, r_j = np.divmod(np.arange(C * I), I)
    cu, cr = np.divmod(np.arange((2 * C - 1) * gl * I), gl * I)
    sebig = ((cu[None, :] - (C - 1) == r_tau[:, None]) & ((cr % I)[None, :] == r_j[:, None]))
    rc = np.arange(2 * P)
    cc, cp = np.divmod(np.arange(2 * gl * P), gl * P)
    e2 = (cc[None, :] == (rc // P)[:, None]) & ((cp % P)[None, :] == (rc % P)[:, None])
    lanes = lambda a: a.reshape(NG, 1, gl * P)
    return (kern2, w2, v2, jnp.asarray(sebig, BF16), jnp.asarray(e2, BF16), lanes(a_re), lanes(a_im))


def _s5n_kernel(u_ref, d_ref, k2_ref, w2_ref, v2_ref, se_ref, e2_ref, are_ref, aim_ref, x0_ref, z_ref, xT_ref,
                m_ref, w_ref, v_ref, xcat_ref, bu_ref, xprev_ref, y_ref, *, B, nc, C, nsplit, gp, I, P):
    R = B * nc
    tsub = C // nsplit
    nq = 2 * gp // LANES
    K = C * LANES

    grp = lambda shape, axis, width: lax.broadcasted_iota(jnp.int32, shape, axis) // width
    same_m = grp((LANES, K), 0, I) == (lax.broadcasted_iota(jnp.int32, (LANES, K), 1) % LANES) // I
    same_w = grp((LANES, 2 * gp), 0, I) == (lax.broadcasted_iota(jnp.int32, (LANES, 2 * gp), 1) % gp) // P
    k2 = k2_ref[...].astype(BF16)
    for s in range(C):
        sel = se_ref[:, (C - 1 - s) * LANES:(C - 1 - s) * LANES + K]
        ms = jnp.dot(k2, sel, preferred_element_type=F32)
        m_ref[s * LANES:(s + 1) * LANES, :] = jnp.where(same_m, ms, 0.0).astype(BF16)
        ws = jnp.dot(w2_ref[:, s * 2 * P:(s + 1) * 2 * P].astype(BF16), e2_ref[...], preferred_element_type=F32)
        w_ref[s * LANES:(s + 1) * LANES, :] = jnp.where(same_w, ws, 0.0).astype(BF16)
    same_v = ((lax.broadcasted_iota(jnp.int32, (2 * gp, K), 0) % gp) // P
              == (lax.broadcasted_iota(jnp.int32, (2 * gp, K), 1) % LANES) // I)
    vs = jnp.dot(v2_ref[...].astype(BF16), se_ref[:, (C - 1) * LANES:(C - 1) * LANES + K], preferred_element_type=F32)
    v_ref[...] = jnp.where(same_v, vs, 0.0).astype(BF16)

    for s in range(C):
        xcat_ref[:, s * LANES:(s + 1) * LANES] = u_ref[pl.ds(s, R, stride=C), :].astype(BF16)
    bu = jnp.dot(xcat_ref[...], w_ref[...], preferred_element_type=F32)
    for q in range(nq):
        bu_ref[q] = bu[:, q * LANES:(q + 1) * LANES]
    ar = are_ref[...]
    ai = aim_ref[...]

    def block(c, carry):
        xr, xi = carry
        rows = pl.ds(c, B, stride=nc)
        x = jnp.concatenate([xr, xi], axis=-1)
        for q in range(nq):
            xprev_ref.at[q][rows, :] = x[:, q * LANES:(q + 1) * LANES]
        bu_c = jnp.concatenate([bu_ref.at[q][rows, :] for q in range(nq)], axis=-1)
        return ar * xr - ai * xi + bu_c[:, :gp], ar * xi + ai * xr + bu_c[:, gp:]

    x0 = x0_ref[...]
    xr, xi = lax.fori_loop(0, nc, block, (x0[:, :gp], x0[:, gp:]))
    xT_ref[...] = jnp.concatenate([xr, xi], axis=-1)

    xprev = jnp.concatenate([xprev_ref[q] for q in range(nq)], axis=-1).astype(BF16)
    for hf in range(nsplit):
        cols = slice(hf * tsub * LANES, (hf + 1) * tsub * LANES)
        y = jnp.dot(xcat_ref[...], m_ref[:, cols], preferred_element_type=F32)
        y = y + jnp.dot(xprev, v_ref[:, cols], preferred_element_type=F32)
        for tt in range(tsub):
            y_ref[pl.ds(hf * tsub + tt, R, stride=C), :] = y[:, tt * LANES:(tt + 1) * LANES]
    val = y_ref[...] + d_ref[...] * u_ref[...]
    z_ref[...] = jax.nn.gelu(val).astype(z_ref.dtype)


def s5_natural(u, d, bd, x0, *, B, C, nsplit=2):
    kern2, w2, v2, sebig, e2, a_re, a_im = bd
    M, D = u.shape
    NG, gi, ci = kern2.shape
    I = ci // C
    P = e2.shape[0] // 2
    K = C * LANES
    gp = a_re.shape[-1]
    nc = M // (B * C)
    R = B * nc
    assert D == NG * LANES and gi == LANES and 2 * P == LANES and C % nsplit == 0
    per_g = lambda *s: pl.BlockSpec((None,) + s, lambda n: (n,) + (0,) * len(s))
    const = lambda a: pl.BlockSpec(a.shape, lambda n: (0,) * a.ndim)
    return pl.pallas_call(
        functools.partial(_s5n_kernel, B=B, nc=nc, C=C, nsplit=nsplit, gp=gp, I=I, P=P),
        grid=(NG,),
        in_specs=[pl.BlockSpec((M, LANES), lambda n: (0, n)),
                  pl.BlockSpec((1, LANES), lambda n: (0, n)),
                  per_g(gi, ci), per_g(gi, C * 2 * P), per_g(2 * gp, ci), const(sebig), const(e2),
                  per_g(1, gp), per_g(1, gp), per_g(B, 2 * gp)],
        out_specs=[pl.BlockSpec((M, LANES), lambda n: (0, n)), per_g(B, 2 * gp)],
        out_shape=[jax.ShapeDtypeStruct((M, D), BF16), jax.ShapeDtypeStruct((NG, B, 2 * gp), F32)],
        scratch_shapes=[pltpu.VMEM((K, K), BF16), pltpu.VMEM((K, 2 * gp), BF16), pltpu.VMEM((2 * gp, K), BF16),
                        pltpu.VMEM((R, K), BF16), pltpu.VMEM((2 * gp // LANES, R, LANES), F32),
                        pltpu.VMEM((2 * gp // LANES, R, LANES), F32),
                        pltpu.VMEM((M, LANES), F32)],
        compiler_params=_cparams(("parallel",)),
    )(u, d, kern2, w2, v2, sebig, e2, a_re, a_im, x0)


def _ln_kernel(x_ref, f_ref, gate_ref, g_ref, b_ref, sc_ref, sh_ref, xo_ref, ho_ref, *, alpha):
    y = alpha * x_ref[...] + gate_ref[...] * f_ref[...]
    mu = jnp.mean(y, axis=-1, keepdims=True)
    yc = y - mu
    var = jnp.mean(yc * yc, axis=-1, keepdims=True)
    xn = yc * lax.rsqrt(var + LN_EPS) * g_ref[...] + b_ref[...]
    xo_ref[...] = xn
    ho_ref[...] = (xn * sc_ref[...] + sh_ref[...]).astype(ho_ref.dtype)


def residual_ln(x, f, gate, g, b, scale, shift, *, rows_per_batch, alpha, tm=256):
    M, D = x.shape
    B = gate.shape[0]
    if rows_per_batch == 1:
        tm = M
        mod = pl.BlockSpec((tm, D), lambda i: (i, 0))
        mods = (gate, scale, shift)
    else:
        tm = min(tm, rows_per_batch)
        assert rows_per_batch % tm == 0
        per = rows_per_batch // tm
        mod = pl.BlockSpec((None, 1, D), lambda i: (i // per, 0, 0))
        mods = tuple(z.reshape(B, 1, D) for z in (gate, scale, shift))
    row = pl.BlockSpec((tm, D), lambda i: (i, 0))
    vec = pl.BlockSpec((1, D), lambda i: (0, 0))
    return pl.pallas_call(
        functools.partial(_ln_kernel, alpha=alpha),
        grid=(M // tm,),
        in_specs=[row, row, mod, vec, vec, mod, mod],
        out_specs=[row, row],
        out_shape=[jax.ShapeDtypeStruct((M, D), F32), jax.ShapeDtypeStruct((M, D), BF16)],
        compiler_params=_cparams(("parallel",)),
    )(x, f, mods[0], g.reshape(1, D), b.reshape(1, D), mods[1], mods[2])


INT_MIN = -(2 ** 31)


def _order_key(x):
    bits = pltpu.bitcast(x, jnp.int32)
    return bits ^ ((bits >> 31) & 0x7FFFFFFF)


def _topk_member(key_ref, k_top):
    rows, L = key_ref.shape
    kf = jnp.float32(k_top)

    def count(pred):
        return jnp.sum(pred.astype(F32), axis=-1, keepdims=True)

    def bit_step(it, res):
        cand = res | (jnp.int32(1) << (31 - it))
        cnt = count(key_ref[...] >= (cand ^ INT_MIN))
        return jnp.where(cnt >= kf, cand, res)

    res = lax.fori_loop(0, 32, bit_step, jnp.zeros((rows, 1), jnp.int32))
    thr = res ^ INT_MIN
    key = key_ref[...]
    gt = key > thr
    eq = key == thr
    need = kf - count(gt)
    pos = lax.broadcasted_iota(jnp.int32, (rows, L), 1)
    nbits = max(1, (L - 1).bit_length())

    def pos_step(it, lim):
        cand = lim | (jnp.int32(1) << (nbits - 1 - it))
        cnt = count((key_ref[...] == thr) & (pos < cand))
        return jnp.where(cnt < need, cand, lim)

    lim = lax.fori_loop(0, nbits, pos_step, jnp.zeros((rows, 1), jnp.int32))
    return gt | (eq & (pos <= lim))


def _idx_kernel(q_ref, k_ref, w_ref, mask_ref, acc_ref, key_ref, *, tq, n_heads, k_top):
    i = pl.program_id(1)
    S = k_ref.shape[0]

    def tile(ii):
        se = (ii + 1) * tq
        kb = k_ref[:se, :].astype(BF16)
        acc = acc_ref.at[:, :se]
        keys = key_ref.at[:, :se]
        acc[...] = jnp.zeros((tq, se), F32)

        def weighted(h):
            q = q_ref[:, pl.ds(pl.multiple_of(h * IDX_DIM, IDX_DIM), IDX_DIM)]
            s = lax.dot_general(q, kb, (((1,), (1,)), ((), ())), preferred_element_type=F32)
            wcol = pltpu.roll(w_ref[...], LANES - h, 1)[:, :1]
            return wcol * jnp.maximum(s, 0.0)

        def heads(hh, carry):
            part = weighted(hh * hpp)
            for e in range(1, hpp):
                part = part + weighted(hh * hpp + e)
            acc[...] += part
            return carry

        hpp = 2 if n_heads % 2 == 0 else 1
        lax.fori_loop(0, n_heads // hpp, heads, 0)
        t_pos = ii * tq + lax.broadcasted_iota(jnp.int32, (tq, se), 0)
        s_pos = lax.broadcasted_iota(jnp.int32, (tq, se), 1)
        causal = s_pos <= t_pos
        keys[...] = _order_key(jnp.where(causal, acc[...], -jnp.inf))
        member = _topk_member(keys, k_top)
        mask_ref[:, :se] = jnp.where(member & causal, 0.0, NEG_BIG)
        if se < S:
            mask_ref[:, se:] = jnp.full((tq, S - se), NEG_BIG, F32)

    for ii in range(S // tq):
        pl.when(i == ii)(functools.partial(tile, ii))


def dsa_select_prompt(q_idx, k_idx, w_idx, *, B, T, k_top, tq=256):
    n_heads = q_idx.shape[1] // IDX_DIM
    tq = min(tq, T)
    nq = T // tq
    return pl.pallas_call(
        functools.partial(_idx_kernel, tq=tq, n_heads=n_heads, k_top=k_top),
        grid=(B, nq),
        in_specs=[pl.BlockSpec((tq, q_idx.shape[1]), lambda b, i: (b * nq + i, 0)),
                  pl.BlockSpec((T, IDX_DIM), lambda b, i: (b, 0)),
                  pl.BlockSpec((tq, LANES), lambda b, i: (b * nq + i, 0))],
        out_specs=pl.BlockSpec((tq, T), lambda b, i: (b * nq + i, 0)),
        out_shape=jax.ShapeDtypeStruct((B * T, T), F32),
        scratch_shapes=[pltpu.VMEM((tq, T), F32), pltpu.VMEM((tq, T), jnp.int32)],
        compiler_params=_cparams(("parallel", "parallel")),
    )(q_idx, k_idx, w_idx)


def _bucket_thresholds(max_dist):
    exact = REL_BUCKETS // 2
    d = np.arange(max_dist + 1)
    large = exact + np.floor(np.log(np.maximum(d, 1) / exact) / math.log(REL_MAX_DIST / exact)
                             * (REL_BUCKETS - exact) + 1e-9).astype(np.int64)
    bucket = np.where(d < exact, d, np.minimum(large, REL_BUCKETS - 1))
    return [int(np.argmax(bucket >= k)) if (bucket >= k).any() else max_dist + 1 for k in range(1, REL_BUCKETS)]


def rel_bucket_of(dist, max_dist):
    out = jnp.zeros(dist.shape, jnp.int32)
    for thr in _bucket_thresholds(max_dist):
        out = out + (dist >= thr).astype(jnp.int32)
    return out


def _attn_kernel(q_ref, k_ref, v_ref, mask_ref, bkt_ref, tab_ref, o_ref, *, scale, hb):
    tq = q_ref.shape[0]
    S = k_ref.shape[0]
    Dh = ATTN_HEAD_DIM
    i = pl.program_id(1)

    def tile(ii):
        se = (ii + 1) * tq
        for e in range(hb):
            cols = slice(e * Dh, (e + 1) * Dh)
            q = q_ref[:, cols].astype(BF16)
            kb = k_ref[:se, cols].astype(BF16)
            logits = lax.dot_general(q, kb, (((1,), (1,)), ((), ())), preferred_element_type=F32) * scale
            tab = jnp.broadcast_to(tab_ref[e], (tq, LANES))
            bias = jnp.concatenate(
                [jnp.take_along_axis(tab, bkt_ref[:, c * LANES:(c + 1) * LANES], axis=1)
                 for c in range(se // LANES)], axis=1)
            logits = logits + bias + mask_ref[:, :se]
            m = jnp.max(logits, axis=-1, keepdims=True)
            p = jnp.exp(logits - m)
            l = jnp.sum(p, axis=-1, keepdims=True)
            o = jnp.dot(p.astype(BF16), v_ref[:se, cols].astype(BF16), preferred_element_type=F32)
            o_ref[:, cols] = (o / l).astype(o_ref.dtype)

    for ii in range(S // tq):
        pl.when(i == ii)(functools.partial(tile, ii))


def dsa_attend_prompt(q, k, v, mask, bucket, bias_tab, *, B, T, tq=256, hb=1):
    H = q.shape[1] // ATTN_HEAD_DIM
    tq = min(tq, T)
    nq = T // tq
    hb = math.gcd(H, hb)
    w = hb * ATTN_HEAD_DIM
    return pl.pallas_call(
        functools.partial(_attn_kernel, scale=ATTN_HEAD_DIM ** -0.5, hb=hb),
        grid=(B, nq, H // hb),
        in_specs=[pl.BlockSpec((tq, w), lambda b, i, h: (b * nq + i, h)),
                  pl.BlockSpec((T, w), lambda b, i, h: (b, h)),
                  pl.BlockSpec((T, w), lambda b, i, h: (b, h)),
                  pl.BlockSpec((tq, T), lambda b, i, h: (b * nq + i, 0)),
                  pl.BlockSpec((tq, T), lambda b, i, h: (i, 0)),
                  pl.BlockSpec((hb, 1, LANES), lambda b, i, h: (h, 0, 0))],
        out_specs=pl.BlockSpec((tq, w), lambda b, i, h: (b * nq + i, h)),
        out_shape=jax.ShapeDtypeStruct(q.shape, BF16),
        compiler_params=_cparams(("parallel", "parallel", "arbitrary")),
    )(q, k, v, mask, bucket, bias_tab)


def _page_score_kernel(pt_ref, q_ref, w_ref, *refs):
    *kidx_refs, sc_ref = refs
    page = kidx_refs[0].shape[0]
    for e, kidx_ref in enumerate(kidx_refs):
        kb = kidx_ref[...].astype(BF16)
        s = lax.dot_general(q_ref[...], kb, (((1,), (1,)), ((), ())), preferred_element_type=F32)
        sc_ref[:, e * page:(e + 1) * page] = jnp.sum(w_ref[...] * jnp.maximum(s, 0.0), axis=0, keepdims=True)


def dsa_page_scores(q_idx, w_idx, cache_kidx, page_table, j, pps=8):
    B, HI, _ = q_idx.shape
    page = cache_kidx.shape[2]
    n_pages = page_table.shape[1]
    pps = math.gcd(n_pages, pps)
    page_spec = lambda e: pl.BlockSpec((None, None, page, IDX_DIM), lambda b, p, pt: (j, pt[b, p * pps + e], 0, 0))
    grid_spec = pltpu.PrefetchScalarGridSpec(
        num_scalar_prefetch=1,
        grid=(B, n_pages // pps),
        in_specs=[pl.BlockSpec((None, HI, IDX_DIM), lambda b, p, pt: (b, 0, 0)),
                  pl.BlockSpec((None, HI, 1), lambda b, p, pt: (b, 0, 0))] + [page_spec(e) for e in range(pps)],
        out_specs=pl.BlockSpec((None, 1, pps * page), lambda b, p, pt: (b, 0, p)),
    )
    return pl.pallas_call(
        _page_score_kernel,
        grid_spec=grid_spec,
        out_shape=jax.ShapeDtypeStruct((B, 1, n_pages * page), F32),
        compiler_params=_cparams(("parallel", "arbitrary")),
    )(page_table, q_idx, w_idx, *([cache_kidx] * pps))


def _sample_select_kernel(sc_ref, q_ref, w_ref, knew_ref, mask_ref, key_ref, *, past, k_top):
    B = sc_ref.shape[0]
    key_ref[:, :past] = _order_key(sc_ref[...])
    lane = lax.broadcasted_iota(jnp.int32, (1, LANES), 1)
    for b in range(B):
        kn = knew_ref[b].astype(BF16).astype(F32)
        s = jnp.sum(q_ref[b].astype(F32) * kn, axis=-1, keepdims=True)
        sc_new = jnp.sum(w_ref[b] * jnp.maximum(s, 0.0), axis=0, keepdims=True)
        tail = jnp.where(lane == 0, sc_new, -jnp.inf)
        key_ref[b:b + 1, past:] = _order_key(tail)
    member = _topk_member(key_ref, k_top)
    mask_ref[...] = jnp.where(member, 0.0, NEG_BIG)


def dsa_select_sample(scores, q_idx, w_idx, k_idx_new, *, k_top):
    B, past = scores.shape
    L = past + LANES
    return pl.pallas_call(
        functools.partial(_sample_select_kernel, past=past, k_top=k_top),
        out_shape=jax.ShapeDtypeStruct((B, L), F32),
        scratch_shapes=[pltpu.VMEM((B, L), jnp.int32)],
        compiler_params=pltpu.CompilerParams(vmem_limit_bytes=VMEM_LIMIT_BYTES),
    )(scores, q_idx, w_idx, k_idx_new)


def _page_attn_kernel(pt_ref, q_ref, *refs, scale, H, pps):
    k_refs, v_refs = refs[:pps], refs[pps:2 * pps]
    bias_ref, mask_ref, knew_ref, vnew_ref, bnew_ref, mnew_ref, o_ref, m_ref, l_ref, acc_ref = refs[2 * pps:]
    p_idx = pl.program_id(1)

    @pl.when(p_idx == 0)
    def _():
        m_ref[...] = jnp.full_like(m_ref, NEG_BIG)
        l_ref[...] = jnp.zeros_like(l_ref)
        acc_ref[...] = jnp.zeros_like(acc_ref)

    q = q_ref[...].astype(BF16)
    cols = k_refs[0].shape[0]
    row_h = lax.broadcasted_iota(jnp.int32, (H, cols), 0)
    col_h = lax.broadcasted_iota(jnp.int32, (H, cols), 1) % H
    for e in range(pps):
        la = lax.dot_general(q, k_refs[e][...].astype(BF16), (((1,), (1,)), ((), ())), preferred_element_type=F32)
        la = la * scale + bias_ref[e]
        valid = (row_h == col_h) & (mask_ref[e] == 0.0)
        la = jnp.where(valid, la, NEG_BIG)
        m_old = m_ref[...]
        m_new = jnp.maximum(m_old, jnp.max(la, axis=-1, keepdims=True))
        alpha = jnp.exp(m_old - m_new)
        p = jnp.where(valid, jnp.exp(la - m_new), 0.0)
        l_ref[...] = alpha * l_ref[...] + jnp.sum(p, axis=-1, keepdims=True)
        acc_ref[...] = alpha * acc_ref[...] + jnp.dot(p.astype(BF16), v_refs[e][...].astype(BF16),
                                                      preferred_element_type=F32)
        m_ref[...] = m_new

    @pl.when(p_idx == pl.num_programs(1) - 1)
    def _():
        kn = knew_ref[...].astype(BF16).astype(F32)
        s_new = jnp.sum(q.astype(F32) * kn, axis=-1, keepdims=True) * scale + bnew_ref[...]
        ok = mnew_ref[...][:, :1] == 0.0
        s_new = jnp.where(ok, s_new, NEG_BIG)
        m_o = m_ref[...]
        m_f = jnp.maximum(m_o, s_new)
        a_f = jnp.exp(m_o - m_f)
        p_new = jnp.where(ok, jnp.exp(s_new - m_f), 0.0)
        l_f = a_f * l_ref[...] + p_new
        acc = a_f * acc_ref[...] + p_new.astype(BF16).astype(F32) * vnew_ref[...].astype(BF16).astype(F32)
        o_ref[...] = acc / l_f


def dsa_attend_sample(q, cache_k, cache_v, page_table, j, bias_rows, mask_rows, k_new, v_new, bias_new, mask_new):
    B, H, Dh = q.shape
    cols = cache_k.shape[2]
    n_pages = page_table.shape[1]
    pps = math.gcd(n_pages, 2)
    per_b = lambda *s: pl.BlockSpec((None,) + s, lambda b, p, pt: (b,) + (0,) * len(s))
    page_spec = lambda e: pl.BlockSpec((None, None, cols, Dh), lambda b, p, pt: (j, pt[b, p * pps + e], 0, 0))
    pages = [page_spec(e) for e in range(pps)]
    grid_spec = pltpu.PrefetchScalarGridSpec(
        num_scalar_prefetch=1,
        grid=(B, n_pages // pps),
        in_specs=[per_b(H, Dh)] + pages + pages + [
                  pl.BlockSpec((pps, 1, cols), lambda b, p, pt: (p, 0, 0)),
                  pl.BlockSpec((None, pps, 1, cols), lambda b, p, pt: (b, p, 0, 0)),
                  per_b(H, Dh), per_b(H, Dh),
                  pl.BlockSpec((H, 1), lambda b, p, pt: (0, 0)),
                  per_b(1, LANES)],
        out_specs=per_b(H, Dh),
        scratch_shapes=[pltpu.VMEM((H, 1), F32), pltpu.VMEM((H, 1), F32), pltpu.VMEM((H, Dh), F32)],
    )
    return pl.pallas_call(
        functools.partial(_page_attn_kernel, scale=Dh ** -0.5, H=H, pps=pps),
        grid_spec=grid_spec,
        out_shape=jax.ShapeDtypeStruct((B, H, Dh), F32),
        compiler_params=_cparams(("parallel", "arbitrary")),
    )(page_table, q, *([cache_k] * pps), *([cache_v] * pps), bias_rows, mask_rows, k_new, v_new, bias_new, mask_new)


def _rwkv_layer(h, B, T, shift0, wkv0, j, p):
    M, D = h.shape
    N = RWKV_HEAD_DIM
    H = D // N
    h3 = h.reshape(B, T, D)
    prev = jnp.concatenate([shift0[:, None].astype(F32), h3[:, :-1]], axis=1)
    d = prev - h3
    mu = p["rwkv_mu"][j]
    mix = [(h3 + d * mu[i]).astype(BF16).reshape(M, D) for i in range(6)]
    seq = lambda z: z.reshape(B, T, D)
    r = matmul(mix[0], p["rwkv_w_rkv"], (j, 0))
    k = matmul(mix[2], p["rwkv_w_rkv"], (j, 1))
    v = matmul(mix[3], p["rwkv_w_rkv"], (j, 2))
    wl = matmul(matmul(mix[1], p["rwkv_w1"], (j,), act="tanh", out_dtype=BF16), p["rwkv_w2"], (j,))
    al = matmul(matmul(mix[4], p["rwkv_a1"], (j,), out_dtype=BF16), p["rwkv_a2"], (j,))
    g = matmul(matmul(mix[5], p["rwkv_g1"], (j,), act="sigmoid", out_dtype=BF16), p["rwkv_g2"], (j,))
    nb = min(B, max(1, LANES // H))
    vec = lambda z: jnp.tile(z.astype(F32).reshape(H, N).T, (1, nb))
    par = jnp.stack([vec(p[q][j]) for q in ("rwkv_w0", "rwkv_a0", "rwkv_k_k", "rwkv_k_a", "rwkv_r_k",
                                            "rwkv_gn_g", "rwkv_gn_b")])
    if wkv0 is None:
        s0 = jnp.zeros((N, N // SUBLANES, SUBLANES, B * H), F32)
    else:
        s0 = jnp.transpose(wkv0.astype(F32), (2, 3, 0, 1)).reshape(N, N // SUBLANES, SUBLANES, B * H)
    z, s_last = wkv_scan(seq(r), seq(wl), seq(k), seq(v), seq(al), seq(g), par, s0, tc=math.gcd(T, 16), nb=nb)
    wkv_new = jnp.transpose(s_last.reshape(N, N, B, H), (2, 3, 0, 1))
    out = matmul(z.reshape(M, D), p["rwkv_w_o"], (j,))
    return out, h3[:, -1], wkv_new


def _s5_layer(hb, B, T, x0_re, x0_im, j, p):
    M, D = hb.shape
    I = S5_GROUP
    G = D // I
    P = p["s5_lambda_re"].shape[-1]
    C = S5_SUB if T % S5_SUB == 0 else 1
    nc = T // C
    bp = -(-B // SUBLANES) * SUBLANES
    u = matmul(hb, p["s5_w_in"], (j,))
    folded = _s5_fold(p["s5_lambda_re"][j], p["s5_lambda_im"][j], p["s5_log_dt"][j], p["s5_b_re"][j],
                      p["s5_b_im"][j], p["s5_c_re"][j], p["s5_c_im"][j], C)
    gl = LANES // I
    if C > 1 and G % gl == 0:
        NG = G // gl
        if x0_re is None:
            x0 = jnp.zeros((NG, B, 2 * gl * P), F32)
        else:
            st = lambda z: jnp.transpose(z.astype(F32).reshape(B, NG, gl * P), (1, 0, 2))
            x0 = jnp.concatenate([st(x0_re), st(x0_im)], axis=-1)
        z, xT = s5_natural(u, p["s5_d"][j].astype(F32).reshape(1, D), _s5_blockdiag(folded, gl), x0, B=B, C=C)
        fin = lambda q: jnp.transpose(q.reshape(NG, B, gl, P), (1, 0, 2, 3)).reshape(B, G, P)
        hr, hi = fin(xT[..., :gl * P]), fin(xT[..., gl * P:])
    else:
        ug = jnp.transpose(u.reshape(B, nc, C, G, I), (3, 1, 0, 2, 4))
        ug = jnp.pad(ug, ((0, 0), (0, 0), (0, bp - B), (0, 0), (0, 0))).reshape(G, nc * bp, C * I)
        if x0_re is None:
            x0r = x0i = jnp.zeros((G, bp, P), F32)
        else:
            st = lambda z: jnp.pad(jnp.transpose(z.astype(F32), (1, 0, 2)), ((0, 0), (0, bp - B), (0, 0)))
            x0r, x0i = st(x0_re), st(x0_im)
        yg, xr, xi = s5_scan(ug, _s5_pergroup(folded), x0r, x0i, gb=8)
        y = jnp.transpose(yg.reshape(G, nc, bp, C, I)[:, :, :B], (2, 1, 3, 0, 4)).reshape(M, D)
        z = jax.nn.gelu(y + p["s5_d"][j].astype(F32) * u).astype(BF16)
        fin = lambda q: jnp.transpose(q[:, :B], (1, 0, 2))
        hr, hi = fin(xr), fin(xi)
    ab = matmul(z, p["s5_w_glu"], (j,))
    glu = (ab[:, :D] * jax.nn.sigmoid(ab[:, D:])).astype(BF16)
    out = matmul(glu, p["s5_w_out"], (j,))
    return out, hr, hi


def _dsa_project(hb, j, p):
    D = hb.shape[1]
    q = matmul(hb, p["attn_w_qkv"], (j,), n_off=0, n_size=D)
    k = matmul(hb, p["attn_w_qkv"], (j,), n_off=D, n_size=D)
    v = matmul(hb, p["attn_w_qkv"], (j,), n_off=2 * D, n_size=D)
    q_idx = matmul(hb, p["idx_w_q"], (j,), out_dtype=BF16)
    k_idx = matmul(hb, p["idx_w_k"], (j,))
    n_ih = p["idx_w_w"].shape[-1]
    w_idx = matmul(hb, p["idx_w_w"], (j,)) * ((n_ih * IDX_DIM) ** -0.5)
    return q, k, v, q_idx, k_idx, w_idx


def _dsa_prompt_layer(hb, B, T, j, p):
    M, D = hb.shape
    H = D // ATTN_HEAD_DIM
    q, k, v, q_idx, k_idx, w_idx = _dsa_project(hb, j, p)
    k_top = max(1, min(TOPK_MAX, T // 4))
    w_pad = jnp.pad(w_idx, ((0, 0), (0, LANES - w_idx.shape[1])))
    mask = dsa_select_prompt(q_idx, k_idx, w_pad, B=B, T=T, k_top=k_top)
    pos = jnp.arange(T, dtype=jnp.int32)
    bucket = rel_bucket_of(pos[:, None] - pos[None, :], T)
    tab = jnp.pad(p["rel_bias"].astype(F32).T, ((0, 0), (0, LANES - REL_BUCKETS))).reshape(H, 1, LANES)
    o = dsa_attend_prompt(q, k, v, mask, bucket, tab, B=B, T=T)
    out = matmul(o, p["attn_w_o"], (j,))
    return out, k.reshape(B, T, H, ATTN_HEAD_DIM), v.reshape(B, T, H, ATTN_HEAD_DIM), k_idx.reshape(B, T, IDX_DIM)


def _dsa_sample_layer(hb, B, j, p, cache_k, cache_v, cache_kidx, page_table):
    M, D = hb.shape
    H = D // ATTN_HEAD_DIM
    q, k, v, q_idx, k_idx, w_idx = _dsa_project(hb, j, p)
    page = cache_kidx.shape[2]
    n_pages = page_table.shape[1]
    past = n_pages * page
    n_ih = w_idx.shape[1]
    q_idx3 = q_idx.reshape(B, n_ih, IDX_DIM)
    w_idx3 = w_idx.reshape(B, n_ih, 1)
    scores = dsa_page_scores(q_idx3, w_idx3, cache_kidx, page_table, j).reshape(B, past)
    k_top = max(1, min(TOPK_MAX, (past + 1) // 4))
    mask = dsa_select_sample(scores, q_idx3, w_idx3, k_idx.reshape(B, 1, IDX_DIM), k_top=k_top)
    dist = past - jnp.arange(past, dtype=jnp.int32)
    bias_rows = p["rel_bias"].astype(F32)[rel_bucket_of(dist, past)]
    bias_rows = bias_rows.reshape(n_pages, 1, page * H)
    mask_rows = jnp.repeat(mask[:, :past], H, axis=1).reshape(B, n_pages, 1, page * H)
    pool = cache_k.shape[1]
    ck = cache_k.reshape(cache_k.shape[0], pool, page * H, ATTN_HEAD_DIM)
    cv = cache_v.reshape(cache_v.shape[0], pool, page * H, ATTN_HEAD_DIM)
    hd = lambda z: z.reshape(B, H, ATTN_HEAD_DIM)
    bias_new = p["rel_bias"].astype(F32)[0].reshape(H, 1)
    mask_new = jnp.broadcast_to(mask[:, past:past + 1], (B, LANES)).reshape(B, 1, LANES)
    o = dsa_attend_sample(hd(q), ck, cv, page_table, j, bias_rows, mask_rows, hd(k), hd(v), bias_new, mask_new)
    out = matmul(o.reshape(M, D).astype(BF16), p["attn_w_o"], (j,))
    return out, k.reshape(B, 1, H, ATTN_HEAD_DIM), v.reshape(B, 1, H, ATTN_HEAD_DIM), k_idx.reshape(B, 1, IDX_DIM)


def _run(x, mods, sample, p, caches, states):
    B, T, D = x.shape
    M = B * T
    depth = len(mods)
    alpha = (2.0 * depth) ** 0.25
    modulate = lambda z, sc, sh: (z.reshape(B, T, D) * (1.0 + sc)[:, None] + sh[:, None]).reshape(M, D)
    xf = x.reshape(M, D).astype(F32)
    h = modulate(xf, mods[0][1], mods[0][0])
    hb = h.astype(BF16)
    st = ([], [], [])
    for i in range(depth):
        sh1, sc1, g1, sh2, sc2, g2 = mods[i]
        kind, j = i % 3, i // 3
        if kind == 0:
            if i > 0:
                h = modulate(xf, sc1, sh1)
            s0, w0 = (states["shift"][j], states["wkv"][j]) if sample else (jnp.zeros((B, D), F32), None)
            out, shift_new, wkv_new = _rwkv_layer(h, B, T, s0, w0, j, p)
            st[0].append((wkv_new, shift_new))
        elif kind == 1:
            x0r, x0i = (states["s5_re"][j], states["s5_im"][j]) if sample else (None, None)
            out, hr, hi = _s5_layer(hb, B, T, x0r, x0i, j, p)
            st[1].append((hr, hi))
        else:
            if sample:
                out, kn, vn, kin = _dsa_sample_layer(hb, B, j, p, *caches)
            else:
                out, kn, vn, kin = _dsa_prompt_layer(hb, B, T, j, p)
            st[2].append((kn, vn, kin))
        xf, hb = residual_ln(xf, out, 1.0 + g1, p["ln_g"][i, 0], p["ln_b"][i, 0], 1.0 + sc2, sh2,
                             rows_per_batch=T, alpha=alpha)
        f = matmul(matmul(hb, p["mlp_w1"], (i,), act="relu2", out_dtype=BF16), p["mlp_w2"], (i,), tk=2048, tn=1024)
        if i + 1 < depth:
            nsc, nsh = 1.0 + mods[i + 1][1], mods[i + 1][0]
        else:
            nsc, nsh = jnp.ones_like(sc1), jnp.zeros_like(sh1)
        xf, hb = residual_ln(xf, f, 1.0 + g2, p["ln_g"][i, 1], p["ln_b"][i, 1], nsc, nsh,
                             rows_per_batch=T, alpha=alpha)
    stk = lambda kind, k: jnp.stack([s[k] for s in st[kind]])
    return (xf.reshape(B, T, D), stk(0, 0), stk(0, 1), stk(1, 0), stk(1, 1), stk(2, 0), stk(2, 1), stk(2, 2))


def kernel(x_prompt, x_sample, cache_k, cache_v, cache_kidx, state_wkv, state_shift, state_s5_re, state_s5_im,
           page_table, c_prompt, c_sample, ada_w, ada_b, ln_g, ln_b, mlp_w1, mlp_w2,
           rwkv_mu, rwkv_w_rkv, rwkv_w_o, rwkv_w0, rwkv_w1, rwkv_w2, rwkv_a0, rwkv_a1, rwkv_a2,
           rwkv_g1, rwkv_g2, rwkv_k_k, rwkv_k_a, rwkv_r_k, rwkv_gn_g, rwkv_gn_b,
           s5_w_in, s5_lambda_re, s5_lambda_im, s5_log_dt, s5_b_re, s5_b_im, s5_c_re, s5_c_im, s5_d,
           s5_w_glu, s5_w_out, attn_w_qkv, attn_w_o, idx_w_q, idx_w_k, idx_w_w, rel_bias):
    p = dict(ln_g=ln_g, ln_b=ln_b, mlp_w1=mlp_w1, mlp_w2=mlp_w2, rwkv_mu=rwkv_mu, rwkv_w_rkv=rwkv_w_rkv,
             rwkv_w_o=rwkv_w_o, rwkv_w0=rwkv_w0, rwkv_w1=rwkv_w1, rwkv_w2=rwkv_w2, rwkv_a0=rwkv_a0,
             rwkv_a1=rwkv_a1, rwkv_a2=rwkv_a2, rwkv_g1=rwkv_g1, rwkv_g2=rwkv_g2, rwkv_k_k=rwkv_k_k,
             rwkv_k_a=rwkv_k_a, rwkv_r_k=rwkv_r_k, rwkv_gn_g=rwkv_gn_g, rwkv_gn_b=rwkv_gn_b,
             s5_w_in=s5_w_in, s5_lambda_re=s5_lambda_re, s5_lambda_im=s5_lambda_im, s5_log_dt=s5_log_dt,
             s5_b_re=s5_b_re, s5_b_im=s5_b_im, s5_c_re=s5_c_re, s5_c_im=s5_c_im, s5_d=s5_d,
             s5_w_glu=s5_w_glu, s5_w_out=s5_w_out, attn_w_qkv=attn_w_qkv, attn_w_o=attn_w_o,
             idx_w_q=idx_w_q, idx_w_k=idx_w_k, idx_w_w=idx_w_w, rel_bias=rel_bias)
    depth = ada_w.shape[0]
    Bp, Bs = c_prompt.shape[0], c_sample.shape[0]
    c_all = jax.nn.silu(jnp.concatenate([c_prompt, c_sample], axis=0).astype(F32))
    pad = -(-c_all.shape[0] // 16) * 16 - c_all.shape[0]
    c_all = jnp.pad(c_all, ((0, pad), (0, 0)))
    mods_p, mods_s = [], []
    for i in range(depth):
        mod = matmul(c_all, ada_w, (i,)) + ada_b[i]
        six = jnp.split(mod, 6, axis=-1)
        mods_p.append([z[:Bp] for z in six])
        mods_s.append([z[Bp:Bp + Bs] for z in six])
    states = dict(wkv=state_wkv, shift=state_shift, s5_re=state_s5_re, s5_im=state_s5_im)
    caches = (cache_k, cache_v, cache_kidx, page_table)
    out_p = _run(x_prompt, mods_p, False, p, caches, states)
    out_s = _run(x_sample, mods_s, True, p, caches, states)
    return (out_p[0], out_s[0]) + out_p[1:] + out_s[1:]
```

```python
import functools
import math

import jax
import jax.numpy as jnp
import numpy as np
from jax import lax
from jax.experimental import pallas as pl
from jax.experimental.pallas import tpu as pltpu

F32 = jnp.float32
BF16 = jnp.bfloat16

LANES = 128
SUBLANES = 8
VMEM_LIMIT_BYTES = 56 * 1024 * 1024

LN_EPS = 1e-5
GN_EPS = 64e-5
RWKV_HEAD_DIM = 64
S5_GROUP = 16
S5_SUB = 16
ATTN_HEAD_DIM = 128
IDX_DIM = 128
TOPK_MAX = 256
REL_BUCKETS = 32
REL_MAX_DIST = 1024
NEG_BIG = -1e30


def _cparams(sem):
    return pltpu.CompilerParams(dimension_semantics=sem, vmem_limit_bytes=VMEM_LIMIT_BYTES)


def _act(p, act):
    if act is None:
        return p
    if act == "relu2":
        r = jnp.maximum(p, 0.0)
        return r * r
    if act == "tanh":
        return jnp.tanh(p)
    if act == "sigmoid":
        return jax.nn.sigmoid(p)
    raise ValueError(act)


def _mm_kernel(a_ref, w_ref, o_ref, *scratch, nk, act):
    p = jnp.dot(a_ref[...].astype(BF16), w_ref[...].astype(BF16), preferred_element_type=F32)
    if nk == 1:
        o_ref[...] = _act(p, act).astype(o_ref.dtype)
        return
    acc_ref, = scratch
    k = pl.program_id(2)

    @pl.when(k == 0)
    def _():
        acc_ref[...] = p

    @pl.when(k > 0)
    def _():
        acc_ref[...] += p

    @pl.when(k == nk - 1)
    def _():
        o_ref[...] = _act(acc_ref[...], act).astype(o_ref.dtype)


def _pick(n, pref):
    if n <= pref:
        return n
    t = (pref // LANES) * LANES
    while t >= LANES:
        if n % t == 0:
            return t
        t -= LANES
    return n


def matmul(a, w, widx=(), *, act=None, out_dtype=F32, n_off=0, n_size=None, tm=1024, tn=512, tk=4096):
    M, K = a.shape
    Kw, N = w.shape[-2:]
    assert K == Kw and len(widx) == w.ndim - 2
    n_size = N if n_size is None else n_size
    tm = _pick(M, tm) if M % SUBLANES == 0 else M
    tn = _pick(n_size, tn)
    tk = _pick(K, tk)
    assert M % tm == 0 and n_size % tn == 0 and K % tk == 0 and n_off % tn == 0
    nk = K // tk
    joff = n_off // tn
    lead = tuple(widx)
    w_spec = pl.BlockSpec((None,) * len(lead) + (tk, tn), lambda i, j, k: lead + (k, j + joff))
    return pl.pallas_call(
        functools.partial(_mm_kernel, nk=nk, act=act),
        grid=(M // tm, n_size // tn, nk),
        in_specs=[pl.BlockSpec((tm, tk), lambda i, j, k: (i, k)), w_spec],
        out_specs=pl.BlockSpec((tm, tn), lambda i, j, k: (i, j)),
        out_shape=jax.ShapeDtypeStruct((M, n_size), out_dtype),
        scratch_shapes=[pltpu.VMEM((tm, tn), F32)] if nk > 1 else [],
        compiler_params=_cparams(("parallel", "parallel", "arbitrary")),
    )(a, w)


def _allsum_sublanes(p):
    p = p + pltpu.roll(p, 4, 0)
    p = p + pltpu.roll(p, 2, 0)
    return p + pltpu.roll(p, 1, 0)


def _wkv_kernel(r_ref, wl_ref, k_ref, v_ref, al_ref, g_ref, par_ref, s0_ref, z_ref, sT_ref,
                s_ref, op_ref, vt_ref, yt_ref, sc_ref, in_ref, zs_ref, *, tc, n, nb, nh):
    c = pl.program_id(1)
    n8 = n // SUBLANES
    lw = nb * nh

    @pl.when(c == 0)
    def _():
        s_ref[...] = s0_ref[...]

    for o, ref in enumerate((r_ref, wl_ref, k_ref, v_ref, al_ref, g_ref)):
        for q in range(nb):
            in_ref[o, q] = ref[q].reshape(tc, nh, n)
    R_, WL_, K_, V_, AL_, G_ = range(6)

    w0, a0, k_k, k_a, r_k, gn_g, gn_b = (par_ref[q] for q in range(7))
    cat = lambda o, t: in_ref[o, :, t].reshape(lw, n).T
    fold = lambda x: x.reshape(n8, SUBLANES, lw)
    allsum = lambda x: _allsum_sublanes(jnp.sum(fold(x), axis=0))

    def prep(t, carry):
        r = cat(R_, t)
        k = cat(K_, t)
        w_log = -jax.nn.softplus(-(w0 + cat(WL_, t))) - 0.5
        w = jnp.exp(-jnp.exp(w_log))
        a = jax.nn.sigmoid(a0 + cat(AL_, t))
        kk = k * k_k
        nrm = jnp.maximum(jnp.sqrt(allsum(kk * kk)), 1e-12)
        kk3 = fold(kk) / nrm[None]
        kh = k * (1.0 + (a - 1.0) * k_a)
        b3 = kk3 * fold(a)
        op_ref[t, 0] = -kk3
        op_ref[t, 1] = fold(w * r)
        op_ref[t, 2] = fold(w)
        op_ref[t, 3] = b3
        op_ref[t, 4] = fold(kh)
        vt_ref[t] = cat(V_, t)
        sc_ref[t, 0] = _allsum_sublanes(jnp.sum(b3 * fold(r), axis=0))
        sc_ref[t, 1] = allsum(kh * r)
        sc_ref[t, 2] = allsum(r * kh * r_k)
        return carry

    lax.fori_loop(0, tc, prep, 0, unroll=min(tc, 4))

    def step(t, carry):
        br = sc_ref[t, 0, 0:1]
        kr = sc_ref[t, 1, 0:1]

        def row(i, carry2):
            si = s_ref[i]
            sa = _allsum_sublanes(jnp.sum(si * op_ref[t, 0], axis=0))
            y0 = _allsum_sublanes(jnp.sum(si * op_ref[t, 1], axis=0))
            vi = vt_ref[t, pl.ds(i, 1), :]
            s_ref[i] = si * op_ref[t, 2] + sa[None] * op_ref[t, 3] + vi[None] * op_ref[t, 4]
            yt_ref[t, pl.ds(i, 1), :] = y0[0:1] + sa[0:1] * br + vi * kr
            return carry2

        lax.fori_loop(0, n, row, 0, unroll=16)
        return carry

    lax.fori_loop(0, tc, step, 0)

    def post(t, carry):
        y = yt_ref[t]
        mu = allsum(y) * (1.0 / n)
        yc = fold(y) - mu[None]
        var = _allsum_sublanes(jnp.sum(yc * yc, axis=0)) * (1.0 / n)
        yn = (yc * lax.rsqrt(var + GN_EPS)[None]).reshape(n, lw) * gn_g + gn_b
        z = yn + (fold(vt_ref[t]) * sc_ref[t, 2][None]).reshape(n, lw)
        zt = z.T * in_ref[G_, :, t].reshape(lw, n)
        for q in range(nb):
            zs_ref[q, t] = zt[q * nh:(q + 1) * nh]
        return carry

    lax.fori_loop(0, tc, post, 0, unroll=min(tc, 4))
    for q in range(nb):
        z_ref[q] = zs_ref[q].reshape(tc, nh * n).astype(z_ref.dtype)

    @pl.when(c == pl.num_programs(1) - 1)
    def _():
        sT_ref[...] = s_ref[...]


def wkv_scan(r, wl, k, v, al, g, par, s0, *, tc, nb):
    B, T, D = r.shape
    n = par.shape[1]
    nh = D // n
    n8 = n // SUBLANES
    lw = nb * nh
    assert T % tc == 0 and B % nb == 0
    op = pl.BlockSpec((nb, tc, D), lambda l, c: (l, c, 0))
    st = pl.BlockSpec((n, n8, SUBLANES, lw), lambda l, c: (0, 0, 0, l))
    return pl.pallas_call(
        functools.partial(_wkv_kernel, tc=tc, n=n, nb=nb, nh=nh),
        grid=(B // nb, T // tc),
        in_specs=[op, op, op, op, op, op, pl.BlockSpec((7, n, lw), lambda l, c: (0, 0, 0)), st],
        out_specs=[op, st],
        out_shape=[jax.ShapeDtypeStruct((B, T, D), BF16), jax.ShapeDtypeStruct(s0.shape, F32)],
        scratch_shapes=[pltpu.VMEM((n, n8, SUBLANES, lw), F32), pltpu.VMEM((tc, 5, n8, SUBLANES, lw), F32),
                        pltpu.VMEM((tc, n, lw), F32), pltpu.VMEM((tc, n, lw), F32),
                        pltpu.VMEM((tc, 3, SUBLANES, lw), F32),
                        pltpu.VMEM((6, nb, tc, nh, n), F32), pltpu.VMEM((nb, tc, nh, n), F32)],
        compiler_params=_cparams(("parallel", "arbitrary")),
    )(r, wl, k, v, al, g, par, s0)


_HI = lax.Precision.HIGHEST


def _s5_fold(lam_re, lam_im, log_dt, b_re, b_im, c_re, c_im, C):
    G, P, I = b_re.shape
    dt = jnp.exp(log_dt.astype(F32))[:, None]
    lr, li = lam_re.astype(F32), lam_im.astype(F32)
    mag = jnp.exp(lr * dt)
    abar_re, abar_im = mag * jnp.cos(li * dt), mag * jnp.sin(li * dt)
    den = lr * lr + li * li
    nr, ni = abar_re - 1.0, abar_im
    coef_re = (nr * lr + ni * li) / den
    coef_im = (ni * lr - nr * li) / den
    br_, bi_ = b_re.astype(F32), b_im.astype(F32)
    bb_re = coef_re[..., None] * br_ - coef_im[..., None] * bi_
    bb_im = coef_re[..., None] * bi_ + coef_im[..., None] * br_
    cr, ci = c_re.astype(F32), c_im.astype(F32)
    pw_re, pw_im = [jnp.ones_like(abar_re)], [jnp.zeros_like(abar_im)]
    for _ in range(C):
        pr, pi = pw_re[-1], pw_im[-1]
        pw_re.append(pr * abar_re - pi * abar_im)
        pw_im.append(pr * abar_im + pi * abar_re)
    a_re, a_im = jnp.stack(pw_re, 1), jnp.stack(pw_im, 1)
    ab_re = a_re[:, :C, :, None] * bb_re[:, None] - a_im[:, :C, :, None] * bb_im[:, None]
    ab_im = a_re[:, :C, :, None] * bb_im[:, None] + a_im[:, :C, :, None] * bb_re[:, None]
    kern = (jnp.einsum("gjp,gtpi->gtij", cr, ab_re, precision=_HI)
            - jnp.einsum("gjp,gtpi->gtij", ci, ab_im, precision=_HI))
    ar1, ai1 = a_re[:, 1:], a_im[:, 1:]
    v_re = cr[:, None] * ar1[:, :, None, :] - ci[:, None] * ai1[:, :, None, :]
    v_im = -cr[:, None] * ai1[:, :, None, :] - ci[:, None] * ar1[:, :, None, :]
    v_re = jnp.transpose(v_re, (0, 3, 1, 2)).reshape(G, P, C * I)
    v_im = jnp.transpose(v_im, (0, 3, 1, 2)).reshape(G, P, C * I)
    return kern, ab_re, ab_im, v_re, v_im, a_re[:, C][:, None], a_im[:, C][:, None]


def _s5_pergroup(folded):
    kern, ab_re, ab_im, v_re, v_im, a_re, a_im = folded
    G, C, P, I = ab_re.shape
    tt = np.arange(C)[None, :] - np.arange(C)[:, None]
    m = kern[:, np.clip(tt, 0, C - 1)]
    m = jnp.where((tt >= 0)[None, :, :, None, None], m, 0.0)
    m = jnp.transpose(m, (0, 1, 3, 2, 4)).reshape(G, C * I, C * I)
    rev = np.arange(C - 1, -1, -1)
    w_re = jnp.transpose(ab_re[:, rev], (0, 1, 3, 2)).reshape(G, C * I, P)
    w_im = jnp.transpose(ab_im[:, rev], (0, 1, 3, 2)).reshape(G, C * I, P)
    return m, w_re, w_im, v_re, v_im, a_re, a_im


def _s5_kernel(u_ref, m_ref, wre_ref, wim_ref, vre_ref, vim_ref, are_ref, aim_ref, x0re_ref, x0im_ref,
               y_ref, xre_ref, xim_ref, bur_ref, bui_ref, *, gb, nc, bp):
    for g in range(gb):
        u = u_ref[g]
        bur_ref[g] = jnp.dot(u, wre_ref[g], precision=_HI, preferred_element_type=F32)
        bui_ref[g] = jnp.dot(u, wim_ref[g], precision=_HI, preferred_element_type=F32)
    ar = are_ref[...]
    ai = aim_ref[...]

    def chunk(c, carry):
        xr, xi = carry
        rows = pl.ds(pl.multiple_of(c * bp, bp), bp)
        br = bur_ref[:, rows, :]
        bi = bui_ref[:, rows, :]
        bur_ref[:, rows, :] = xr
        bui_ref[:, rows, :] = xi
        return ar * xr - ai * xi + br, ar * xi + ai * xr + bi

    xr, xi = lax.fori_loop(0, nc, chunk, (x0re_ref[...], x0im_ref[...]))
    xre_ref[...] = xr
    xim_ref[...] = xi
    for g in range(gb):
        y = jnp.dot(u_ref[g], m_ref[g], precision=_HI, preferred_element_type=F32)
        y += jnp.dot(bur_ref[g], vre_ref[g], precision=_HI, preferred_element_type=F32)
        y += jnp.dot(bui_ref[g], vim_ref[g], precision=_HI, preferred_element_type=F32)
        y_ref[g] = y


def s5_scan(u, folded, x0_re, x0_im, *, gb):
    m, w_re, w_im, v_re, v_im, a_re, a_im = folded
    G, R, CI = u.shape
    P = w_re.shape[-1]
    bp = x0_re.shape[1]
    nc = R // bp
    assert G % gb == 0 and bp % SUBLANES == 0
    blk = lambda *s: pl.BlockSpec((gb,) + s, lambda g: (g,) + (0,) * len(s))
    return pl.pallas_call(
        functools.partial(_s5_kernel, gb=gb, nc=nc, bp=bp),
        grid=(G // gb,),
        in_specs=[blk(R, CI), blk(CI, CI), blk(CI, P), blk(CI, P), blk(P, CI), blk(P, CI),
                  blk(1, P), blk(1, P), blk(bp, P), blk(bp, P)],
        out_specs=[blk(R, CI), blk(bp, P), blk(bp, P)],
        out_shape=[jax.ShapeDtypeStruct((G, R, CI), F32), jax.ShapeDtypeStruct((G, bp, P), F32),
                   jax.ShapeDtypeStruct((G, bp, P), F32)],
        scratch_shapes=[pltpu.VMEM((gb, R, P), F32), pltpu.VMEM((gb, R, P), F32)],
        compiler_params=_cparams(("parallel",)),
    )(u, m, w_re, w_im, v_re, v_im, a_re, a_im, x0_re, x0_im)


def _s5_blockdiag(folded, gl):
    kern, ab_re, ab_im, v_re, v_im, a_re, a_im = folded
    G, C, P, I = ab_re.shape
    NG = G // gl
    kern2 = jnp.transpose(kern.reshape(NG, gl, C, I, I), (0, 1, 3, 2, 4)).reshape(NG, gl * I, C * I)
    rev = np.arange(C - 1, -1, -1)
    ab = jnp.stack([ab_re[:, rev], ab_im[:, rev]], axis=2)
    w2 = jnp.transpose(ab.reshape(NG, gl, C, 2, P, I), (0, 1, 5, 2, 3, 4)).reshape(NG, gl * I, C * 2 * P)
    v2 = jnp.stack([v_re, v_im], axis=0).reshape(2, NG, gl * P, C * I)
    v2 = jnp.transpose(v2, (1, 0, 2, 3)).reshape(NG, 2 * gl * P, C * I)
    r_tau, r_j = np.divmod(np.arange(C * I), I)
    cu, cr = np.divmod(np.arange((2 * C - 1) * gl * I), gl * I)
    sebig = ((cu[None, :] - (C - 1) == r_tau[:, None]) & ((cr % I)[None, :] == r_j[:, None]))
    rc = np.arange(2 * P)
    cc, cp = np.divmod(np.arange(2 * gl * P), gl * P)
    e2 = (cc[None, :] == (rc // P)[:, None]) & ((cp % P)[None, :] == (rc % P)[:, None])
    lanes = lambda a: a.reshape(NG, 1, gl * P)
    return (kern2, w2, v2, jnp.asarray(sebig, BF16), jnp.asarray(e2, BF16), lanes(a_re), lanes(a_im))


def _s5n_kernel(u_ref, d_ref, k2_ref, w2_ref, v2_ref, se_ref, e2_ref, are_ref, aim_ref, x0_ref, z_ref, xT_ref,
                m_ref, w_ref, v_ref, xcat_ref, bu_ref, xprev_ref, y_ref, *, B, nc, C, nsplit, gp, I, P):
    R = B * nc
    tsub = C // nsplit
    nq = 2 * gp // LANES
    K = C * LANES

    grp = lambda shape, axis, width: lax.broadcasted_iota(jnp.int32, shape, axis) // width
    same_m = grp((LANES, K), 0, I) == (lax.broadcasted_iota(jnp.int32, (LANES, K), 1) % LANES) // I
    same_w = grp((LANES, 2 * gp), 0, I) == (lax.broadcasted_iota(jnp.int32, (LANES, 2 * gp), 1) % gp) // P
    k2 = k2_ref[...].astype(BF16)
    for s in range(C):
        sel = se_ref[:, (C - 1 - s) * LANES:(C - 1 - s) * LANES + K]
        ms = jnp.dot(k2, sel, preferred_element_type=F32)
        m_ref[s * LANES:(s + 1) * LANES, :] = jnp.where(same_m, ms, 0.0).astype(BF16)
        ws = jnp.dot(w2_ref[:, s * 2 * P:(s + 1) * 2 * P].astype(BF16), e2_ref[...], preferred_element_type=F32)
        w_ref[s * LANES:(s + 1) * LANES, :] = jnp.where(same_w, ws, 0.0).astype(BF16)
    same_v = ((lax.broadcasted_iota(jnp.int32, (2 * gp, K), 0) % gp) // P
              == (lax.broadcasted_iota(jnp.int32, (2 * gp, K), 1) % LANES) // I)
    vs = jnp.dot(v2_ref[...].astype(BF16), se_ref[:, (C - 1) * LANES:(C - 1) * LANES + K], preferred_element_type=F32)
    v_ref[...] = jnp.where(same_v, vs, 0.0).astype(BF16)

    for s in range(C):
        xcat_ref[:, s * LANES:(s + 1) * LANES] = u_ref[pl.ds(s, R, stride=C), :].astype(BF16)
    bu = jnp.dot(xcat_ref[...], w_ref[...], preferred_element_type=F32)
    for q in range(nq):
        bu_ref[q] = bu[:, q * LANES:(q + 1) * LANES]
    ar = are_ref[...]
    ai = aim_ref[...]

    def block(c, carry):
        xr, xi = carry
        rows = pl.ds(c, B, stride=nc)
        x = jnp.concatenate([xr, xi], axis=-1)
        for q in range(nq):
            xprev_ref.at[q][rows, :] = x[:, q * LANES:(q + 1) * LANES]
        bu_c = jnp.concatenate([bu_ref.at[q][rows, :] for q in range(nq)], axis=-1)
        return ar * xr - ai * xi + bu_c[:, :gp], ar * xi + ai * xr + bu_c[:, gp:]

    x0 = x0_ref[...]
    xr, xi = lax.fori_loop(0, nc, block, (x0[:, :gp], x0[:, gp:]))
    xT_ref[...] = jnp.concatenate([xr, xi], axis=-1)

    xprev = jnp.concatenate([xprev_ref[q] for q in range(nq)], axis=-1).astype(BF16)
    for hf in range(nsplit):
        cols = slice(hf * tsub * LANES, (hf + 1) * tsub * LANES)
        y = jnp.dot(xcat_ref[...], m_ref[:, cols], preferred_element_type=F32)
        y = y + jnp.dot(xprev, v_ref[:, cols], preferred_element_type=F32)
        for tt in range(tsub):
            y_ref[pl.ds(hf * tsub + tt, R, stride=C), :] = y[:, tt * LANES:(tt + 1) * LANES]
    val = y_ref[...] + d_ref[...] * u_ref[...]
    z_ref[...] = jax.nn.gelu(val).astype(z_ref.dtype)


def s5_natural(u, d, bd, x0, *, B, C, nsplit=2):
    kern2, w2, v2, sebig, e2, a_re, a_im = bd
    M, D = u.shape
    NG, gi, ci = kern2.shape
    I = ci // C
    P = e2.shape[0] // 2
    K = C * LANES
    gp = a_re.shape[-1]
    nc = M // (B * C)
    R = B * nc
    assert D == NG * LANES and gi == LANES and 2 * P == LANES and C % nsplit == 0
    per_g = lambda *s: pl.BlockSpec((None,) + s, lambda n: (n,) + (0,) * len(s))
    const = lambda a: pl.BlockSpec(a.shape, lambda n: (0,) * a.ndim)
    return pl.pallas_call(
        functools.partial(_s5n_kernel, B=B, nc=nc, C=C, nsplit=nsplit, gp=gp, I=I, P=P),
        grid=(NG,),
        in_specs=[pl.BlockSpec((M, LANES), lambda n: (0, n)),
                  pl.BlockSpec((1, LANES), lambda n: (0, n)),
                  per_g(gi, ci), per_g(gi, C * 2 * P), per_g(2 * gp, ci), const(sebig), const(e2),
                  per_g(1, gp), per_g(1, gp), per_g(B, 2 * gp)],
        out_specs=[pl.BlockSpec((M, LANES), lambda n: (0, n)), per_g(B, 2 * gp)],
        out_shape=[jax.ShapeDtypeStruct((M, D), BF16), jax.ShapeDtypeStruct((NG, B, 2 * gp), F32)],
        scratch_shapes=[pltpu.VMEM((K, K), BF16), pltpu.VMEM((K, 2 * gp), BF16), pltpu.VMEM((2 * gp, K), BF16),
                        pltpu.VMEM((R, K), BF16), pltpu.VMEM((2 * gp // LANES, R, LANES), F32),
                        pltpu.VMEM((2 * gp // LANES, R, LANES), F32),
                        pltpu.VMEM((M, LANES), F32)],
        compiler_params=_cparams(("parallel",)),
    )(u, d, kern2, w2, v2, sebig, e2, a_re, a_im, x0)


def _ln_kernel(x_ref, f_ref, gate_ref, g_ref, b_ref, sc_ref, sh_ref, xo_ref, ho_ref, *, alpha):
    y = alpha * x_ref[...] + gate_ref[...] * f_ref[...]
    mu = jnp.mean(y, axis=-1, keepdims=True)
    yc = y - mu
    var = jnp.mean(yc * yc, axis=-1, keepdims=True)
    xn = yc * lax.rsqrt(var + LN_EPS) * g_ref[...] + b_ref[...]
    xo_ref[...] = xn
    ho_ref[...] = (xn * sc_ref[...] + sh_ref[...]).astype(ho_ref.dtype)


def residual_ln(x, f, gate, g, b, scale, shift, *, rows_per_batch, alpha, tm=256):
    M, D = x.shape
    B = gate.shape[0]
    if rows_per_batch == 1:
        tm = M
        mod = pl.BlockSpec((tm, D), lambda i: (i, 0))
        mods = (gate, scale, shift)
    else:
        tm = min(tm, rows_per_batch)
        assert rows_per_batch % tm == 0
        per = rows_per_batch // tm
        mod = pl.BlockSpec((None, 1, D), lambda i: (i // per, 0, 0))
        mods = tuple(z.reshape(B, 1, D) for z in (gate, scale, shift))
    row = pl.BlockSpec((tm, D), lambda i: (i, 0))
    vec = pl.BlockSpec((1, D), lambda i: (0, 0))
    return pl.pallas_call(
        functools.partial(_ln_kernel, alpha=alpha),
        grid=(M // tm,),
        in_specs=[row, row, mod, vec, vec, mod, mod],
        out_specs=[row, row],
        out_shape=[jax.ShapeDtypeStruct((M, D), F32), jax.ShapeDtypeStruct((M, D), BF16)],
        compiler_params=_cparams(("parallel",)),
    )(x, f, mods[0], g.reshape(1, D), b.reshape(1, D), mods[1], mods[2])


INT_MIN = -(2 ** 31)


def _order_key(x):
    bits = pltpu.bitcast(x, jnp.int32)
    return bits ^ ((bits >> 31) & 0x7FFFFFFF)


def _topk_member(key_ref, k_top):
    rows, L = key_ref.shape
    kf = jnp.float32(k_top)

    def count(pred):
        return jnp.sum(pred.astype(F32), axis=-1, keepdims=True)

    def bit_step(it, res):
        cand = res | (jnp.int32(1) << (31 - it))
        cnt = count(key_ref[...] >= (cand ^ INT_MIN))
        return jnp.where(cnt >= kf, cand, res)

    res = lax.fori_loop(0, 32, bit_step, jnp.zeros((rows, 1), jnp.int32))
    thr = res ^ INT_MIN
    key = key_ref[...]
    gt = key > thr
    eq = key == thr
    need = kf - count(gt)
    pos = lax.broadcasted_iota(jnp.int32, (rows, L), 1)
    nbits = max(1, (L - 1).bit_length())

    def pos_step(it, lim):
        cand = lim | (jnp.int32(1) << (nbits - 1 - it))
        cnt = count((key_ref[...] == thr) & (pos < cand))
        return jnp.where(cnt < need, cand, lim)

    lim = lax.fori_loop(0, nbits, pos_step, jnp.zeros((rows, 1), jnp.int32))
    return gt | (eq & (pos <= lim))


def _idx_kernel(q_ref, k_ref, w_ref, mask_ref, acc_ref, key_ref, *, tq, n_heads, k_top):
    i = pl.program_id(1)
    S = k_ref.shape[0]

    def tile(ii):
        se = (ii + 1) * tq
        kb = k_ref[:se, :].astype(BF16)
        acc = acc_ref.at[:, :se]
        keys = key_ref.at[:, :se]
        acc[...] = jnp.zeros((tq, se), F32)

        def weighted(h):
            q = q_ref[:, pl.ds(pl.multiple_of(h * IDX_DIM, IDX_DIM), IDX_DIM)]
            s = lax.dot_general(q, kb, (((1,), (1,)), ((), ())), preferred_element_type=F32)
            wcol = pltpu.roll(w_ref[...], LANES - h, 1)[:, :1]
            return wcol * jnp.maximum(s, 0.0)

        def heads(hh, carry):
            part = weighted(hh * hpp)
            for e in range(1, hpp):
                part = part + weighted(hh * hpp + e)
            acc[...] += part
            return carry

        hpp = 2 if n_heads % 2 == 0 else 1
        lax.fori_loop(0, n_heads // hpp, heads, 0)
        t_pos = ii * tq + lax.broadcasted_iota(jnp.int32, (tq, se), 0)
        s_pos = lax.broadcasted_iota(jnp.int32, (tq, se), 1)
        causal = s_pos <= t_pos
        keys[...] = _order_key(jnp.where(causal, acc[...], -jnp.inf))
        member = _topk_member(keys, k_top)
        mask_ref[:, :se] = jnp.where(member & causal, 0.0, NEG_BIG)
        if se < S:
            mask_ref[:, se:] = jnp.full((tq, S - se), NEG_BIG, F32)

    for ii in range(S // tq):
        pl.when(i == ii)(functools.partial(tile, ii))


def dsa_select_prompt(q_idx, k_idx, w_idx, *, B, T, k_top, tq=256):
    n_heads = q_idx.shape[1] // IDX_DIM
    tq = min(tq, T)
    nq = T // tq
    return pl.pallas_call(
        functools.partial(_idx_kernel, tq=tq, n_heads=n_heads, k_top=k_top),
        grid=(B, nq),
        in_specs=[pl.BlockSpec((tq, q_idx.shape[1]), lambda b, i: (b * nq + i, 0)),
                  pl.BlockSpec((T, IDX_DIM), lambda b, i: (b, 0)),
                  pl.BlockSpec((tq, LANES), lambda b, i: (b * nq + i, 0))],
        out_specs=pl.BlockSpec((tq, T), lambda b, i: (b * nq + i, 0)),
        out_shape=jax.ShapeDtypeStruct((B * T, T), F32),
        scratch_shapes=[pltpu.VMEM((tq, T), F32), pltpu.VMEM((tq, T), jnp.int32)],
        compiler_params=_cparams(("parallel", "parallel")),
    )(q_idx, k_idx, w_idx)


def _bucket_thresholds(max_dist):
    exact = REL_BUCKETS // 2
    d = np.arange(max_dist + 1)
    large = exact + np.floor(np.log(np.maximum(d, 1) / exact) / math.log(REL_MAX_DIST / exact)
                             * (REL_BUCKETS - exact) + 1e-9).astype(np.int64)
    bucket = np.where(d < exact, d, np.minimum(large, REL_BUCKETS - 1))
    return [int(np.argmax(bucket >= k)) if (bucket >= k).any() else max_dist + 1 for k in range(1, REL_BUCKETS)]


def rel_bucket_of(dist, max_dist):
    out = jnp.zeros(dist.shape, jnp.int32)
    for thr in _bucket_thresholds(max_dist):
        out = out + (dist >= thr).astype(jnp.int32)
    return out


def _attn_kernel(q_ref, k_ref, v_ref, mask_ref, bkt_ref, tab_ref, o_ref, *, scale, hb):
    tq = q_ref.shape[0]
    S = k_ref.shape[0]
    Dh = ATTN_HEAD_DIM
    i = pl.program_id(1)

    def tile(ii):
        se = (ii + 1) * tq
        for e in range(hb):
            cols = slice(e * Dh, (e + 1) * Dh)
            q = q_ref[:, cols].astype(BF16)
            kb = k_ref[:se, cols].astype(BF16)
            logits = lax.dot_general(q, kb, (((1,), (1,)), ((), ())), preferred_element_type=F32) * scale
            tab = jnp.broadcast_to(tab_ref[e], (tq, LANES))
            bias = jnp.concatenate(
                [jnp.take_along_axis(tab, bkt_ref[:, c * LANES:(c + 1) * LANES], axis=1)
                 for c in range(se // LANES)], axis=1)
            logits = logits + bias + mask_ref[:, :se]
            m = jnp.max(logits, axis=-1, keepdims=True)
            p = jnp.exp(logits - m)
            l = jnp.sum(p, axis=-1, keepdims=True)
            o = jnp.dot(p.astype(BF16), v_ref[:se, cols].astype(BF16), preferred_element_type=F32)
            o_ref[:, cols] = (o / l).astype(o_ref.dtype)

    for ii in range(S // tq):
        pl.when(i == ii)(functools.partial(tile, ii))


def dsa_attend_prompt(q, k, v, mask, bucket, bias_tab, *, B, T, tq=256, hb=1):
    H = q.shape[1] // ATTN_HEAD_DIM
    tq = min(tq, T)
    nq = T // tq
    hb = math.gcd(H, hb)
    w = hb * ATTN_HEAD_DIM
    return pl.pallas_call(
        functools.partial(_attn_kernel, scale=ATTN_HEAD_DIM ** -0.5, hb=hb),
        grid=(B, nq, H // hb),
        in_specs=[pl.BlockSpec((tq, w), lambda b, i, h: (b * nq + i, h)),
                  pl.BlockSpec((T, w), lambda b, i, h: (b, h)),
                  pl.BlockSpec((T, w), lambda b, i, h: (b, h)),
                  pl.BlockSpec((tq, T), lambda b, i, h: (b * nq + i, 0)),
                  pl.BlockSpec((tq, T), lambda b, i, h: (i, 0)),
                  pl.BlockSpec((hb, 1, LANES), lambda b, i, h: (h, 0, 0))],
        out_specs=pl.BlockSpec((tq, w), lambda b, i, h: (b * nq + i, h)),
        out_shape=jax.ShapeDtypeStruct(q.shape, BF16),
        compiler_params=_cparams(("parallel", "parallel", "arbitrary")),
    )(q, k, v, mask, bucket, bias_tab)


def _page_score_kernel(pt_ref, q_ref, w_ref, *refs):
    *kidx_refs, sc_ref = refs
    page = kidx_refs[0].shape[0]
    for e, kidx_ref in enumerate(kidx_refs):
        kb = kidx_ref[...].astype(BF16)
        s = lax.dot_general(q_ref[...], kb, (((1,), (1,)), ((), ())), preferred_element_type=F32)
        sc_ref[:, e * page:(e + 1) * page] = jnp.sum(w_ref[...] * jnp.maximum(s, 0.0), axis=0, keepdims=True)


def dsa_page_scores(q_idx, w_idx, cache_kidx, page_table, j, pps=8):
    B, HI, _ = q_idx.shape
    page = cache_kidx.shape[2]
    n_pages = page_table.shape[1]
    pps = math.gcd(n_pages, pps)
    page_spec = lambda e: pl.BlockSpec((None, None, page, IDX_DIM), lambda b, p, pt: (j, pt[b, p * pps + e], 0, 0))
    grid_spec = pltpu.PrefetchScalarGridSpec(
        num_scalar_prefetch=1,
        grid=(B, n_pages // pps),
        in_specs=[pl.BlockSpec((None, HI, IDX_DIM), lambda b, p, pt: (b, 0, 0)),
                  pl.BlockSpec((None, HI, 1), lambda b, p, pt: (b, 0, 0))] + [page_spec(e) for e in range(pps)],
        out_specs=pl.BlockSpec((None, 1, pps * page), lambda b, p, pt: (b, 0, p)),
    )
    return pl.pallas_call(
        _page_score_kernel,
        grid_spec=grid_spec,
        out_shape=jax.ShapeDtypeStruct((B, 1, n_pages * page), F32),
        compiler_params=_cparams(("parallel", "arbitrary")),
    )(page_table, q_idx, w_idx, *([cache_kidx] * pps))


def _sample_select_kernel(sc_ref, q_ref, w_ref, knew_ref, mask_ref, key_ref, *, past, k_top):
    B = sc_ref.shape[0]
    key_ref[:, :past] = _order_key(sc_ref[...])
    lane = lax.broadcasted_iota(jnp.int32, (1, LANES), 1)
    for b in range(B):
        kn = knew_ref[b].astype(BF16).astype(F32)
        s = jnp.sum(q_ref[b].astype(F32) * kn, axis=-1, keepdims=True)
        sc_new = jnp.sum(w_ref[b] * jnp.maximum(s, 0.0), axis=0, keepdims=True)
        tail = jnp.where(lane == 0, sc_new, -jnp.inf)
        key_ref[b:b + 1, past:] = _order_key(tail)
    member = _topk_member(key_ref, k_top)
    mask_ref[...] = jnp.where(member, 0.0, NEG_BIG)


def dsa_select_sample(scores, q_idx, w_idx, k_idx_new, *, k_top):
    B, past = scores.shape
    L = past + LANES
    return pl.pallas_call(
        functools.partial(_sample_select_kernel, past=past, k_top=k_top),
        out_shape=jax.ShapeDtypeStruct((B, L), F32),
        scratch_shapes=[pltpu.VMEM((B, L), jnp.int32)],
        compiler_params=pltpu.CompilerParams(vmem_limit_bytes=VMEM_LIMIT_BYTES),
    )(scores, q_idx, w_idx, k_idx_new)


def _page_attn_kernel(pt_ref, q_ref, *refs, scale, H, pps):
    k_refs, v_refs = refs[:pps], refs[pps:2 * pps]
    bias_ref, mask_ref, knew_ref, vnew_ref, bnew_ref, mnew_ref, o_ref, m_ref, l_ref, acc_ref = refs[2 * pps:]
    p_idx = pl.program_id(1)

    @pl.when(p_idx == 0)
    def _():
        m_ref[...] = jnp.full_like(m_ref, NEG_BIG)
        l_ref[...] = jnp.zeros_like(l_ref)
        acc_ref[...] = jnp.zeros_like(acc_ref)

    q = q_ref[...].astype(BF16)
    cols = k_refs[0].shape[0]
    row_h = lax.broadcasted_iota(jnp.int32, (H, cols), 0)
    col_h = lax.broadcasted_iota(jnp.int32, (H, cols), 1) % H
    for e in range(pps):
        la = lax.dot_general(q, k_refs[e][...].astype(BF16), (((1,), (1,)), ((), ())), preferred_element_type=F32)
        la = la * scale + bias_ref[e]
        valid = (row_h == col_h) & (mask_ref[e] == 0.0)
        la = jnp.where(valid, la, NEG_BIG)
        m_old = m_ref[...]
        m_new = jnp.maximum(m_old, jnp.max(la, axis=-1, keepdims=True))
        alpha = jnp.exp(m_old - m_new)
        p = jnp.where(valid, jnp.exp(la - m_new), 0.0)
        l_ref[...] = alpha * l_ref[...] + jnp.sum(p, axis=-1, keepdims=True)
        acc_ref[...] = alpha * acc_ref[...] + jnp.dot(p.astype(BF16), v_refs[e][...].astype(BF16),
                                                      preferred_element_type=F32)
        m_ref[...] = m_new

    @pl.when(p_idx == pl.num_programs(1) - 1)
    def _():
        kn = knew_ref[...].astype(BF16).astype(F32)
        s_new = jnp.sum(q.astype(F32) * kn, axis=-1, keepdims=True) * scale + bnew_ref[...]
        ok = mnew_ref[...][:, :1] == 0.0
        s_new = jnp.where(ok, s_new, NEG_BIG)
        m_o = m_ref[...]
        m_f = jnp.maximum(m_o, s_new)
        a_f = jnp.exp(m_o - m_f)
        p_new = jnp.where(ok, jnp.exp(s_new - m_f), 0.0)
        l_f = a_f * l_ref[...] + p_new
        acc = a_f * acc_ref[...] + p_new.astype(BF16).astype(F32) * vnew_ref[...].astype(BF16).astype(F32)
        o_ref[...] = acc / l_f


def dsa_attend_sample(q, cache_k, cache_v, page_table, j, bias_rows, mask_rows, k_new, v_new, bias_new, mask_new):
    B, H, Dh = q.shape
    cols = cache_k.shape[2]
    n_pages = page_table.shape[1]
    pps = math.gcd(n_pages, 2)
    per_b = lambda *s: pl.BlockSpec((None,) + s, lambda b, p, pt: (b,) + (0,) * len(s))
    page_spec = lambda e: pl.BlockSpec((None, None, cols, Dh), lambda b, p, pt: (j, pt[b, p * pps + e], 0, 0))
    pages = [page_spec(e) for e in range(pps)]
    grid_spec = pltpu.PrefetchScalarGridSpec(
        num_scalar_prefetch=1,
        grid=(B, n_pages // pps),
        in_specs=[per_b(H, Dh)] + pages + pages + [
                  pl.BlockSpec((pps, 1, cols), lambda b, p, pt: (p, 0, 0)),
                  pl.BlockSpec((None, pps, 1, cols), lambda b, p, pt: (b, p, 0, 0)),
                  per_b(H, Dh), per_b(H, Dh),
                  pl.BlockSpec((H, 1), lambda b, p, pt: (0, 0)),
                  per_b(1, LANES)],
        out_specs=per_b(H, Dh),
        scratch_shapes=[pltpu.VMEM((H, 1), F32), pltpu.VMEM((H, 1), F32), pltpu.VMEM((H, Dh), F32)],
    )
    return pl.pallas_call(
        functools.partial(_page_attn_kernel, scale=Dh ** -0.5, H=H, pps=pps),
        grid_spec=grid_spec,
        out_shape=jax.ShapeDtypeStruct((B, H, Dh), F32),
        compiler_params=_cparams(("parallel", "arbitrary")),
    )(page_table, q, *([cache_k] * pps), *([cache_v] * pps), bias_rows, mask_rows, k_new, v_new, bias_new, mask_new)


def _rwkv_layer(h, B, T, shift0, wkv0, j, p):
    M, D = h.shape
    N = RWKV_HEAD_DIM
    H = D // N
    h3 = h.reshape(B, T, D)
    prev = jnp.concatenate([shift0[:, None].astype(F32), h3[:, :-1]], axis=1)
    d = prev - h3
    mu = p["rwkv_mu"][j]
    mix = [(h3 + d * mu[i]).astype(BF16).reshape(M, D) for i in range(6)]
    seq = lambda z: z.reshape(B, T, D)
    r = matmul(mix[0], p["rwkv_w_rkv"], (j, 0))
    k = matmul(mix[2], p["rwkv_w_rkv"], (j, 1))
    v = matmul(mix[3], p["rwkv_w_rkv"], (j, 2))
    wl = matmul(matmul(mix[1], p["rwkv_w1"], (j,), act="tanh", out_dtype=BF16), p["rwkv_w2"], (j,))
    al = matmul(matmul(mix[4], p["rwkv_a1"], (j,), out_dtype=BF16), p["rwkv_a2"], (j,))
    g = matmul(matmul(mix[5], p["rwkv_g1"], (j,), act="sigmoid", out_dtype=BF16), p["rwkv_g2"], (j,))
    nb = min(B, max(1, LANES // H))
    vec = lambda z: jnp.tile(z.astype(F32).reshape(H, N).T, (1, nb))
    par = jnp.stack([vec(p[q][j]) for q in ("rwkv_w0", "rwkv_a0", "rwkv_k_k", "rwkv_k_a", "rwkv_r_k",
                                            "rwkv_gn_g", "rwkv_gn_b")])
    if wkv0 is None:
        s0 = jnp.zeros((N, N // SUBLANES, SUBLANES, B * H), F32)
    else:
        s0 = jnp.transpose(wkv0.astype(F32), (2, 3, 0, 1)).reshape(N, N // SUBLANES, SUBLANES, B * H)
    z, s_last = wkv_scan(seq(r), seq(wl), seq(k), seq(v), seq(al), seq(g), par, s0, tc=math.gcd(T, 16), nb=nb)
    wkv_new = jnp.transpose(s_last.reshape(N, N, B, H), (2, 3, 0, 1))
    out = matmul(z.reshape(M, D), p["rwkv_w_o"], (j,))
    return out, h3[:, -1], wkv_new


def _s5_layer(hb, B, T, x0_re, x0_im, j, p):
    M, D = hb.shape
    I = S5_GROUP
    G = D // I
    P = p["s5_lambda_re"].shape[-1]
    C = S5_SUB if T % S5_SUB == 0 else 1
    nc = T // C
    bp = -(-B // SUBLANES) * SUBLANES
    u = matmul(hb, p["s5_w_in"], (j,))
    folded = _s5_fold(p["s5_lambda_re"][j], p["s5_lambda_im"][j], p["s5_log_dt"][j], p["s5_b_re"][j],
                      p["s5_b_im"][j], p["s5_c_re"][j], p["s5_c_im"][j], C)
    gl = LANES // I
    if C > 1 and G % gl == 0:
        NG = G // gl
        if x0_re is None:
            x0 = jnp.zeros((NG, B, 2 * gl * P), F32)
        else:
            st = lambda z: jnp.transpose(z.astype(F32).reshape(B, NG, gl * P), (1, 0, 2))
            x0 = jnp.concatenate([st(x0_re), st(x0_im)], axis=-1)
        z, xT = s5_natural(u, p["s5_d"][j].astype(F32).reshape(1, D), _s5_blockdiag(folded, gl), x0, B=B, C=C)
        fin = lambda q: jnp.transpose(q.reshape(NG, B, gl, P), (1, 0, 2, 3)).reshape(B, G, P)
        hr, hi = fin(xT[..., :gl * P]), fin(xT[..., gl * P:])
    else:
        ug = jnp.transpose(u.reshape(B, nc, C, G, I), (3, 1, 0, 2, 4))
        ug = jnp.pad(ug, ((0, 0), (0, 0), (0, bp - B), (0, 0), (0, 0))).reshape(G, nc * bp, C * I)
        if x0_re is None:
            x0r = x0i = jnp.zeros((G, bp, P), F32)
        else:
            st = lambda z: jnp.pad(jnp.transpose(z.astype(F32), (1, 0, 2)), ((0, 0), (0, bp - B), (0, 0)))
            x0r, x0i = st(x0_re), st(x0_im)
        yg, xr, xi = s5_scan(ug, _s5_pergroup(folded), x0r, x0i, gb=8)
        y = jnp.transpose(yg.reshape(G, nc, bp, C, I)[:, :, :B], (2, 1, 3, 0, 4)).reshape(M, D)
        z = jax.nn.gelu(y + p["s5_d"][j].astype(F32) * u).astype(BF16)
        fin = lambda q: jnp.transpose(q[:, :B], (1, 0, 2))
        hr, hi = fin(xr), fin(xi)
    ab = matmul(z, p["s5_w_glu"], (j,))
    glu = (ab[:, :D] * jax.nn.sigmoid(ab[:, D:])).astype(BF16)
    out = matmul(glu, p["s5_w_out"], (j,))
    return out, hr, hi


def _dsa_project(hb, j, p):
    D = hb.shape[1]
    q = matmul(hb, p["attn_w_qkv"], (j,), n_off=0, n_size=D)
    k = matmul(hb, p["attn_w_qkv"], (j,), n_off=D, n_size=D)
    v = matmul(hb, p["attn_w_qkv"], (j,), n_off=2 * D, n_size=D)
    q_idx = matmul(hb, p["idx_w_q"], (j,), out_dtype=BF16)
    k_idx = matmul(hb, p["idx_w_k"], (j,))
    n_ih = p["idx_w_w"].shape[-1]
    w_idx = matmul(hb, p["idx_w_w"], (j,)) * ((n_ih * IDX_DIM) ** -0.5)
    return q, k, v, q_idx, k_idx, w_idx


def _dsa_prompt_layer(hb, B, T, j, p):
    M, D = hb.shape
    H = D // ATTN_HEAD_DIM
    q, k, v, q_idx, k_idx, w_idx = _dsa_project(hb, j, p)
    k_top = max(1, min(TOPK_MAX, T // 4))
    w_pad = jnp.pad(w_idx, ((0, 0), (0, LANES - w_idx.shape[1])))
    mask = dsa_select_prompt(q_idx, k_idx, w_pad, B=B, T=T, k_top=k_top)
    pos = jnp.arange(T, dtype=jnp.int32)
    bucket = rel_bucket_of(pos[:, None] - pos[None, :], T)
    tab = jnp.pad(p["rel_bias"].astype(F32).T, ((0, 0), (0, LANES - REL_BUCKETS))).reshape(H, 1, LANES)
    o = dsa_attend_prompt(q, k.astype(BF16), v.astype(BF16), mask, bucket, tab, B=B, T=T)
    out = matmul(o, p["attn_w_o"], (j,))
    return out, k.reshape(B, T, H, ATTN_HEAD_DIM), v.reshape(B, T, H, ATTN_HEAD_DIM), k_idx.reshape(B, T, IDX_DIM)


def _dsa_sample_layer(hb, B, j, p, cache_k, cache_v, cache_kidx, page_table):
    M, D = hb.shape
    H = D // ATTN_HEAD_DIM
    q, k, v, q_idx, k_idx, w_idx = _dsa_project(hb, j, p)
    page = cache_kidx.shape[2]
    n_pages = page_table.shape[1]
    past = n_pages * page
    n_ih = w_idx.shape[1]
    q_idx3 = q_idx.reshape(B, n_ih, IDX_DIM)
    w_idx3 = w_idx.reshape(B, n_ih, 1)
    scores = dsa_page_scores(q_idx3, w_idx3, cache_kidx, page_table, j).reshape(B, past)
    k_top = max(1, min(TOPK_MAX, (past + 1) // 4))
    mask = dsa_select_sample(scores, q_idx3, w_idx3, k_idx.reshape(B, 1, IDX_DIM), k_top=k_top)
    dist = past - jnp.arange(past, dtype=jnp.int32)
    bias_rows = p["rel_bias"].astype(F32)[rel_bucket_of(dist, past)]
    bias_rows = bias_rows.reshape(n_pages, 1, page * H)
    mask_rows = jnp.repeat(mask[:, :past], H, axis=1).reshape(B, n_pages, 1, page * H)
    pool = cache_k.shape[1]
    ck = cache_k.reshape(cache_k.shape[0], pool, page * H, ATTN_HEAD_DIM)
    cv = cache_v.reshape(cache_v.shape[0], pool, page * H, ATTN_HEAD_DIM)
    hd = lambda z: z.reshape(B, H, ATTN_HEAD_DIM)
    bias_new = p["rel_bias"].astype(F32)[0].reshape(H, 1)
    mask_new = jnp.broadcast_to(mask[:, past:past + 1], (B, LANES)).reshape(B, 1, LANES)
    o = dsa_attend_sample(hd(q), ck, cv, page_table, j, bias_rows, mask_rows, hd(k), hd(v), bias_new, mask_new)
    out = matmul(o.reshape(M, D).astype(BF16), p["attn_w_o"], (j,))
    return out, k.reshape(B, 1, H, ATTN_HEAD_DIM), v.reshape(B, 1, H, ATTN_HEAD_DIM), k_idx.reshape(B, 1, IDX_DIM)


def _run(x, mods, sample, p, caches, states):
    B, T, D = x.shape
    M = B * T
    depth = len(mods)
    alpha = (2.0 * depth) ** 0.25
    modulate = lambda z, sc, sh: (z.reshape(B, T, D) * (1.0 + sc)[:, None] + sh[:, None]).reshape(M, D)
    xf = x.reshape(M, D).astype(F32)
    h = modulate(xf, mods[0][1], mods[0][0])
    hb = h.astype(BF16)
    st = ([], [], [])
    for i in range(depth):
        sh1, sc1, g1, sh2, sc2, g2 = mods[i]
        kind, j = i % 3, i // 3
        if kind == 0:
            if i > 0:
                h = modulate(xf, sc1, sh1)
            s0, w0 = (states["shift"][j], states["wkv"][j]) if sample else (jnp.zeros((B, D), F32), None)
            out, shift_new, wkv_new = _rwkv_layer(h, B, T, s0, w0, j, p)
            st[0].append((wkv_new, shift_new))
        elif kind == 1:
            x0r, x0i = (states["s5_re"][j], states["s5_im"][j]) if sample else (None, None)
            out, hr, hi = _s5_layer(hb, B, T, x0r, x0i, j, p)
            st[1].append((hr, hi))
        else:
            if sample:
                out, kn, vn, kin = _dsa_sample_layer(hb, B, j, p, *caches)
            else:
                out, kn, vn, kin = _dsa_prompt_layer(hb, B, T, j, p)
            st[2].append((kn, vn, kin))
        xf, hb = residual_ln(xf, out, 1.0 + g1, p["ln_g"][i, 0], p["ln_b"][i, 0], 1.0 + sc2, sh2,
                             rows_per_batch=T, alpha=alpha)
        f = matmul(matmul(hb, p["mlp_w1"], (i,), act="relu2", out_dtype=BF16), p["mlp_w2"], (i,), tk=2048, tn=1024)
        if i + 1 < depth:
            nsc, nsh = 1.0 + mods[i + 1][1], mods[i + 1][0]
        else:
            nsc, nsh = jnp.ones_like(sc1), jnp.zeros_like(sh1)
        xf, hb = residual_ln(xf, f, 1.0 + g2, p["ln_g"][i, 1], p["ln_b"][i, 1], nsc, nsh,
                             rows_per_batch=T, alpha=alpha)
    stk = lambda kind, k: jnp.stack([s[k] for s in st[kind]])
    return (xf.reshape(B, T, D), stk(0, 0), stk(0, 1), stk(1, 0), stk(1, 1), stk(2, 0), stk(2, 1), stk(2, 2))


def kernel(x_prompt, x_sample, cache_k, cache_v, cache_kidx, state_wkv, state_shift, state_s5_re, state_s5_im,
           page_table, c_prompt, c_sample, ada_w, ada_b, ln_g, ln_b, mlp_w1, mlp_w2,
           rwkv_mu, rwkv_w_rkv, rwkv_w_o, rwkv_w0, rwkv_w1, rwkv_w2, rwkv_a0, rwkv_a1, rwkv_a2,
           rwkv_g1, rwkv_g2, rwkv_k_k, rwkv_k_a, rwkv_r_k, rwkv_gn_g, rwkv_gn_b,
           s5_w_in, s5_lambda_re, s5_lambda_im, s5_log_dt, s5_b_re, s5_b_im, s5_c_re, s5_c_im, s5_d,
           s5_w_glu, s5_w_out, attn_w_qkv, attn_w_o, idx_w_q, idx_w_k, idx_w_w, rel_bias):
    p = dict(ln_g=ln_g, ln_b=ln_b, mlp_w1=mlp_w1, mlp_w2=mlp_w2, rwkv_mu=rwkv_mu, rwkv_w_rkv=rwkv_w_rkv,
             rwkv_w_o=rwkv_w_o, rwkv_w0=rwkv_w0, rwkv_w1=rwkv_w1, rwkv_w2=rwkv_w2, rwkv_a0=rwkv_a0,
             rwkv_a1=rwkv_a1, rwkv_a2=rwkv_a2, rwkv_g1=rwkv_g1, rwkv_g2=rwkv_g2, rwkv_k_k=rwkv_k_k,
             rwkv_k_a=rwkv_k_a, rwkv_r_k=rwkv_r_k, rwkv_gn_g=rwkv_gn_g, rwkv_gn_b=rwkv_gn_b,
             s5_w_in=s5_w_in, s5_lambda_re=s5_lambda_re, s5_lambda_im=s5_lambda_im, s5_log_dt=s5_log_dt,
             s5_b_re=s5_b_re, s5_b_im=s5_b_im, s5_c_re=s5_c_re, s5_c_im=s5_c_im, s5_d=s5_d,
             s5_w_glu=s5_w_glu, s5_w_out=s5_w_out, attn_w_qkv=attn_w_qkv, attn_w_o=attn_w_o,
             idx_w_q=idx_w_q, idx_w_k=idx_w_k, idx_w_w=idx_w_w, rel_bias=rel_bias)
    depth = ada_w.shape[0]
    Bp, Bs = c_prompt.shape[0], c_sample.shape[0]
    c_all = jax.nn.silu(jnp.concatenate([c_prompt, c_sample], axis=0).astype(F32))
    pad = -(-c_all.shape[0] // 16) * 16 - c_all.shape[0]
    c_all = jnp.pad(c_all, ((0, pad), (0, 0)))
    mods_p, mods_s = [], []
    for i in range(depth):
        mod = matmul(c_all, ada_w, (i,)) + ada_b[i]
        six = jnp.split(mod, 6, axis=-1)
        mods_p.append([z[:Bp] for z in six])
        mods_s.append([z[Bp:Bp + Bs] for z in six])
    states = dict(wkv=state_wkv, shift=state_shift, s5_re=state_s5_re, s5_im=state_s5_im)
    caches = (cache_k, cache_v, cache_kidx, page_table)
    out_p = _run(x_prompt, mods_p, False, p, caches, states)
    out_s = _run(x_sample, mods_s, True, p, caches, states)
    return (out_p[0], out_s[0]) + out_p[1:] + out_s[1:]
```
